```python
import math
import jax
import jax.numpy as jnp
from jax import lax
import numpy as np

D_MODEL = 1024
BATCH = 8
SEQ = 2048
DEPTH = 4

PLE_DIM = 256
MIX_WIDTH = 256
N_BRANCH = 4
S5_WIDTH = MIX_WIDTH
S5_GROUP = 16
S5_GROUPS = S5_WIDTH // S5_GROUP
S5_STATE = 64
MLA_HEADS = 4
MLA_NOPE = 64
MLA_ROPE = 32
MLA_V = 64
MLA_Q_LORA = 192
MLA_KV_LORA = 128
Q_BLOCK = 128
HG_HEADS = 4
HG_K = 64
HG_V = 64
HG_CHUNK = 16
RET_HEADS = 4
RET_K = 64
RET_V = 64
RET_CHUNK = 64
D_FF = 3584
N_EXPERTS = 8
TOP_K = 2
D_FF_EXPERT = 3584
MOE_BLOCK = 128
N_DENSE = (DEPTH + 1) // 2
N_MOE = DEPTH // 2
ROPE_BASE = 10000.0
EPS = 1e-5
NEG_INF = -1e30
ALPHA = (2 * DEPTH) ** 0.25
BETA = (8 * DEPTH) ** -0.25
IN_SPLITS = (S5_WIDTH, MLA_Q_LORA, MLA_KV_LORA, MLA_ROPE,
             HG_HEADS * HG_K, HG_HEADS * HG_K, HG_HEADS * HG_V, HG_HEADS * HG_V,
             RET_HEADS * RET_K, RET_HEADS * RET_K, RET_HEADS * RET_V, RET_HEADS * RET_V,
             N_BRANCH * D_MODEL)
D_IN = sum(IN_SPLITS)

kernel_name = 'hybrid_gated_s5_mla_hgrn2_retnet_deepnorm'


def layer_norm(x, g, b):
    xf = x.astype(jnp.float32)
    mu = jnp.mean(xf, axis=-1, keepdims=True)
    var = jnp.mean(jnp.square(xf - mu), axis=-1, keepdims=True)
    return ((xf - mu) * lax.rsqrt(var + EPS) * g + b).astype(x.dtype)


def rms_norm(x, g):
    xf = x.astype(jnp.float32)
    return (xf * lax.rsqrt(jnp.mean(xf * xf, axis=-1, keepdims=True) + EPS) * g).astype(x.dtype)


def split_columns(h, sizes):
    out, start = [], 0
    for n in sizes:
        out.append(h[..., start:start + n])
        start += n
    return out


def rope_tables(positions, dim):
    half = dim // 2
    inv_freq = ROPE_BASE ** (-jnp.arange(half, dtype=jnp.float32) / half)
    ang = positions.astype(jnp.float32)[..., None] * inv_freq
    return jnp.cos(ang)[:, :, None, :], jnp.sin(ang)[:, :, None, :]


def apply_rope(x, cos, sin):
    half = x.shape[-1] // 2
    x1, x2 = x[..., :half], x[..., half:]
    return jnp.concatenate([x1 * cos - x2 * sin, x2 * cos + x1 * sin], axis=-1).astype(x.dtype)


def _complex_affine_combine(left, right):
    a1r, a1i, b1r, b1i = left
    a2r, a2i, b2r, b2i = right
    return (a2r * a1r - a2i * a1i, a2r * a1i + a2i * a1r,
            a2r * b1r - a2i * b1i + b2r, a2r * b1i + a2i * b1r + b2i)


def s5_mixer(u, a_re, a_im, log_dt, b_re, b_im, c_re, c_im, d_skip, w_glu):
    f32 = jnp.float32
    bsz, s, _ = u.shape
    uf = u.astype(f32)
    ug = uf.reshape(bsz, s, S5_GROUPS, S5_GROUP)
    dt = jnp.exp(log_dt.astype(f32))[:, None]
    lr, li = a_re.astype(f32), a_im.astype(f32)
    mag = jnp.exp(lr * dt)
    abar_re, abar_im = mag * jnp.cos(li * dt), mag * jnp.sin(li * dt)
    den = lr * lr + li * li
    num_re, num_im = abar_re - 1.0, abar_im
    coef_re = (num_re * lr + num_im * li) / den
    coef_im = (num_im * lr - num_re * li) / den
    br, bi = b_re.astype(f32), b_im.astype(f32)
    bbar_re = coef_re[..., None] * br - coef_im[..., None] * bi
    bbar_im = coef_re[..., None] * bi + coef_im[..., None] * br
    bu_re = jnp.einsum('bsgc,gpc->bsgp', ug, bbar_re)
    bu_im = jnp.einsum('bsgc,gpc->bsgp', ug, bbar_im)
    ar = jnp.broadcast_to(abar_re, bu_re.shape)
    ai = jnp.broadcast_to(abar_im, bu_re.shape)
    _, _, s_re, s_im = lax.associative_scan(_complex_affine_combine, (ar, ai, bu_re, bu_im), axis=1)
    y = (jnp.einsum('bsgp,gcp->bsgc', s_re, c_re.astype(f32))
         - jnp.einsum('bsgp,gcp->bsgc', s_im, c_im.astype(f32)))
    y = y.reshape(bsz, s, S5_WIDTH) + d_skip.astype(f32) * uf
    y = jax.nn.gelu(y)
    return (y * jax.nn.sigmoid(y @ w_glu.astype(f32))).astype(u.dtype)


def causal_mla_attention(q_nope, q_pe, k_nope, k_pe, v):
    bsz, s, h, _ = q_nope.shape
    nb = s // Q_BLOCK
    scale = (MLA_NOPE + MLA_ROPE) ** -0.5
    qn = q_nope.reshape(bsz, nb, Q_BLOCK, h, MLA_NOPE).swapaxes(0, 1)
    qp = q_pe.reshape(bsz, nb, Q_BLOCK, h, MLA_ROPE).swapaxes(0, 1)
    k_idx = jnp.arange(s)

    def one_block(args):
        blk, qn_b, qp_b = args
        sc = (jnp.einsum('bqhd,bkhd->bhqk', qn_b, k_nope)
              + jnp.einsum('bqhr,bkr->bhqk', qp_b, k_pe)).astype(jnp.float32) * scale
        q_idx = blk * Q_BLOCK + jnp.arange(Q_BLOCK)
        sc = jnp.where(k_idx[None, :] <= q_idx[:, None], sc, NEG_INF)
        w = jax.nn.softmax(sc, axis=-1).astype(v.dtype)
        return jnp.einsum('bhqk,bkhd->bqhd', w, v)

    out = lax.map(one_block, (jnp.arange(nb), qn, qp))
    return out.swapaxes(0, 1).reshape(bsz, s, h * MLA_V)


def mla_mixer(c_q, c_kv, k_rope, cos, sin, q_norm_g, kv_norm_g, w_uq, w_ukv):
    bsz, s, _ = c_q.shape
    q = (rms_norm(c_q, q_norm_g) @ w_uq).reshape(bsz, s, MLA_HEADS, MLA_NOPE + MLA_ROPE)
    q_nope, q_pe = q[..., :MLA_NOPE], apply_rope(q[..., MLA_NOPE:], cos, sin)
    kv = (rms_norm(c_kv, kv_norm_g) @ w_ukv).reshape(bsz, s, MLA_HEADS, MLA_NOPE + MLA_V)
    k_nope, v = kv[..., :MLA_NOPE], kv[..., MLA_NOPE:]
    k_pe = apply_rope(k_rope[:, :, None, :], cos, sin)[:, :, 0, :]
    return causal_mla_attention(q_nope, q_pe, k_nope, k_pe, v).astype(c_q.dtype)


def hgrn2_mixer(q, f, i_in, g, lb, norm_g):
    f32 = jnp.float32
    bsz, s, _ = q.shape
    n = s // HG_CHUNK
    lb = lb.astype(f32)
    ff = f.astype(f32)
    log_f = jnp.logaddexp(jnp.log(lb), jnp.log1p(-lb) + jax.nn.log_sigmoid(ff))
    k = (1.0 - lb) * jax.nn.sigmoid(-ff)
    qf = jax.nn.silu(q.astype(f32))

    def chunks(t, d):
        return t.reshape(bsz, n, HG_CHUNK, HG_HEADS, d).transpose(0, 3, 1, 2, 4)

    qc, kc, lfc = chunks(qf, HG_K), chunks(k, HG_K), chunks(log_f, HG_K)
    vc = chunks(i_in.astype(f32), HG_V)
    bcum = jnp.cumsum(lfc, axis=3)
    tri = jnp.tril(jnp.ones((HG_CHUNK, HG_CHUNK), dtype=bool))
    diff = bcum[..., :, None, :] - bcum[..., None, :, :]
    decay = jnp.exp(jnp.where(tri[:, :, None], diff, -jnp.inf))
    attn = jnp.einsum('bhntk,bhnsk,bhntsk->bhnts', qc, kc, decay)
    o_intra = jnp.einsum('bhnts,bhnsv->bhntv', attn, vc)
    b_last = bcum[..., -1:, :]
    kv_chunk = jnp.einsum('bhnck,bhncv->bhnkv', kc * jnp.exp(b_last - bcum), vc)
    chunk_decay = jnp.exp(b_last[..., 0, :])

    def step(state, inp):
        dec, kv = inp
        return dec[..., None] * state + kv, state

    init = jnp.zeros((bsz, HG_HEADS, HG_K, HG_V), f32)
    _, s_prev = lax.scan(step, init, (jnp.moveaxis(chunk_decay, 2, 0), jnp.moveaxis(kv_chunk, 2, 0)))
    s_prev = jnp.moveaxis(s_prev, 0, 2)
    o_inter = jnp.einsum('bhnck,bhnkv->bhncv', qc * jnp.exp(bcum), s_prev)
    o = (o_intra + o_inter).transpose(0, 2, 3, 1, 4).reshape(bsz, s, HG_HEADS, HG_V)
    o = rms_norm(o, norm_g.astype(f32).reshape(HG_HEADS, HG_V)).reshape(bsz, s, HG_HEADS * HG_V)
    return (o * jax.nn.silu(g.astype(f32))).astype(q.dtype)


def retention_mixer(q, k, v, g, cos, sin, gn_g, gn_b):
    f32 = jnp.float32
    bsz, s, _ = q.shape
    n = s // RET_CHUNK
    qh = apply_rope(q.astype(f32).reshape(bsz, s, RET_HEADS, RET_K), cos, sin) * RET_K ** -0.5
    kh = apply_rope(k.astype(f32).reshape(bsz, s, RET_HEADS, RET_K), cos, sin)
    vh = v.astype(f32).reshape(bsz, s, RET_HEADS, RET_V)

    def chunks(t):
        return t.reshape(bsz, n, RET_CHUNK, RET_HEADS, t.shape[-1]).transpose(0, 3, 1, 2, 4)

    qc, kc, vc = chunks(qh), chunks(kh), chunks(vh)
    log_gamma = jnp.log(1.0 - 2.0 ** (-5.0 - jnp.arange(RET_HEADS, dtype=f32)))
    t_idx = jnp.arange(RET_CHUNK, dtype=f32)
    rel = t_idx[:, None] - t_idx[None, :]
    dmat = jnp.where(rel >= 0, jnp.exp(jnp.maximum(rel, 0.0)[None] * log_gamma[:, None, None]), 0.0)
    scores = jnp.einsum('bhntd,bhnsd->bhnts', qc, kc) * dmat[None, :, None]
    o_intra = jnp.einsum('bhnts,bhnsv->bhntv', scores, vc)
    k_decay = jnp.exp((RET_CHUNK - 1 - t_idx)[None, :] * log_gamma[:, None])
    kv_chunk = jnp.einsum('bhnck,bhncv->bhnkv', kc * k_decay[None, :, None, :, None], vc)
    chunk_decay = jnp.exp(RET_CHUNK * log_gamma)[None, :, None, None]

    def step(state, kv):
        return chunk_decay * state + kv, state

    init = jnp.zeros((bsz, RET_HEADS, RET_K, RET_V), f32)
    _, s_prev = lax.scan(step, init, jnp.moveaxis(kv_chunk, 2, 0))
    s_prev = jnp.moveaxis(s_prev, 0, 2)
    q_decay = jnp.exp((t_idx + 1.0)[None, :] * log_gamma[:, None])
    o_inter = jnp.einsum('bhnck,bhnkv->bhncv', qc * q_decay[None, :, None, :, None], s_prev)
    o = (o_intra + o_inter).transpose(0, 2, 3, 1, 4).reshape(bsz, s, RET_HEADS, RET_V)
    o = layer_norm(o, gn_g.astype(f32).reshape(RET_HEADS, RET_V), gn_b.astype(f32).reshape(RET_HEADS, RET_V))
    return (jax.nn.silu(g.astype(f32)) * o.reshape(bsz, s, RET_HEADS * RET_V)).astype(q.dtype)


def swiglu(x, w_gate, w_up, w_down):
    return (jax.nn.silu(x @ w_gate) * (x @ w_up)) @ w_down


def moe_swiglu(x, w_router, w_gate, w_up, w_down):
    bsz, s, d = x.shape
    t = bsz * s
    xf = x.reshape(t, d)
    logits = (xf @ w_router).astype(jnp.float32)
    top_val, top_idx = lax.top_k(logits, TOP_K)
    top_w = jax.nn.softmax(top_val, axis=-1)
    m = t * TOP_K
    e_flat = top_idx.reshape(m)
    tok_flat = jnp.repeat(jnp.arange(t, dtype=jnp.int32), TOP_K)
    w_flat = top_w.reshape(m)
    order = jnp.argsort(e_flat)
    e_sorted = e_flat[order]
    counts = jnp.bincount(e_flat, length=N_EXPERTS)
    padded = (counts + MOE_BLOCK - 1) // MOE_BLOCK * MOE_BLOCK
    pad_end = jnp.cumsum(padded)
    pad_start = pad_end - padded
    start = jnp.cumsum(counts) - counts
    dest = pad_start[e_sorted] + jnp.arange(m) - start[e_sorted]
    n_blk = (m + MOE_BLOCK - 1) // MOE_BLOCK + N_EXPERTS
    p_len = n_blk * MOE_BLOCK
    slot_tok = jnp.full((p_len,), t, jnp.int32).at[dest].set(tok_flat[order])
    slot_w = jnp.zeros((p_len,), jnp.float32).at[dest].set(w_flat[order])
    blk_start = jnp.arange(n_blk) * MOE_BLOCK
    blk_expert = jnp.minimum(jnp.sum(pad_end[None, :] <= blk_start[:, None], axis=1), N_EXPERTS - 1)
    x_pad = jnp.concatenate([xf, jnp.zeros((1, d), xf.dtype)], axis=0)
    xb = x_pad[slot_tok].reshape(n_blk, MOE_BLOCK, d)

    def expert_block(args):
        xe, e = args
        return (jax.nn.silu(xe @ w_gate[e]) * (xe @ w_up[e])) @ w_down[e]

    yb = lax.map(expert_block, (xb, blk_expert)).reshape(p_len, d)
    y = jnp.zeros((t + 1, d), yb.dtype).at[slot_tok].add(yb * slot_w[:, None].astype(yb.dtype))
    return y[:t].reshape(bsz, s, d).astype(x.dtype)


def setup_inputs(seed: int = 0) -> dict:
    key = jax.random.key(seed)
    ks = iter(jax.random.split(key, 40))
    f32 = jnp.float32

    def nrm(shape, scale):
        return jax.random.normal(next(ks), shape, f32) * scale

    def gain(shape):
        return 1.0 + nrm(shape, 0.01)

    x = nrm((BATCH, SEQ, D_MODEL), 1.0)
    p = nrm((DEPTH, BATCH, SEQ, PLE_DIM), 1.0)
    positions = (jax.random.randint(next(ks), (BATCH, 1), 0, 1024, dtype=jnp.int32)
                 + jnp.arange(SEQ, dtype=jnp.int32)[None, :])
    w_in = nrm((DEPTH, D_MODEL, D_IN), D_MODEL ** -0.5)
    s5_a_re = -0.5 + nrm((DEPTH, S5_GROUPS, S5_STATE), 0.01)
    s5_a_im = math.pi * jnp.arange(S5_STATE, dtype=f32) + nrm((DEPTH, S5_GROUPS, S5_STATE), 0.01)
    s5_log_dt = jax.random.uniform(next(ks), (DEPTH, S5_GROUPS), f32, math.log(0.001), math.log(0.1))
    s5_b_re = nrm((DEPTH, S5_GROUPS, S5_STATE, S5_GROUP), (2 * S5_GROUP) ** -0.5)
    s5_b_im = nrm((DEPTH, S5_GROUPS, S5_STATE, S5_GROUP), (2 * S5_GROUP) ** -0.5)
    s5_c_re = nrm((DEPTH, S5_GROUPS, S5_GROUP, S5_STATE), 0.5)
    s5_c_im = nrm((DEPTH, S5_GROUPS, S5_GROUP, S5_STATE), 0.5)
    s5_d = nrm((DEPTH, S5_WIDTH), 1.0)
    s5_w_glu = nrm((DEPTH, S5_WIDTH, S5_WIDTH), S5_WIDTH ** -0.5)
    mla_q_norm = gain((DEPTH, MLA_Q_LORA))
    mla_kv_norm = gain((DEPTH, MLA_KV_LORA))
    mla_w_uq = nrm((DEPTH, MLA_Q_LORA, MLA_HEADS * (MLA_NOPE + MLA_ROPE)), MLA_Q_LORA ** -0.5)
    mla_w_ukv = nrm((DEPTH, MLA_KV_LORA, MLA_HEADS * (MLA_NOPE + MLA_V)), MLA_KV_LORA ** -0.5)
    hg_lb_raw = nrm((DEPTH, HG_HEADS * HG_K), 0.1)
    hg_norm = gain((DEPTH, HG_HEADS * HG_V))
    ret_gn_g = gain((DEPTH, RET_HEADS * RET_V))
    ret_gn_b = nrm((DEPTH, RET_HEADS * RET_V), 0.01)
    w_branch = nrm((DEPTH, N_BRANCH, MIX_WIDTH, D_MODEL), BETA * MIX_WIDTH ** -0.5)
    w_o = nrm((DEPTH, D_MODEL, D_MODEL), BETA * D_MODEL ** -0.5)
    ln1_g = gain((DEPTH, D_MODEL))
    ln1_b = nrm((DEPTH, D_MODEL), 0.01)
    ff_w_gate = nrm((N_DENSE, D_MODEL, D_FF), BETA * D_MODEL ** -0.5)
    ff_w_up = nrm((N_DENSE, D_MODEL, D_FF), BETA * D_MODEL ** -0.5)
    ff_w_down = nrm((N_DENSE, D_FF, D_MODEL), BETA * D_FF ** -0.5)
    moe_router = nrm((N_MOE, D_MODEL, N_EXPERTS), D_MODEL ** -0.5)
    moe_w_gate = nrm((N_MOE, N_EXPERTS, D_MODEL, D_FF_EXPERT), BETA * D_MODEL ** -0.5)
    moe_w_up = nrm((N_MOE, N_EXPERTS, D_MODEL, D_FF_EXPERT), BETA * D_MODEL ** -0.5)
    moe_w_down = nrm((N_MOE, N_EXPERTS, D_FF_EXPERT, D_MODEL), BETA * D_FF_EXPERT ** -0.5)
    ple_w_gate = nrm((DEPTH, D_MODEL, D_MODEL), D_MODEL ** -0.5)
    ple_w_proj = nrm((DEPTH, PLE_DIM, D_MODEL), PLE_DIM ** -0.5)
    ln2_g = gain((DEPTH, D_MODEL))
    ln2_b = nrm((DEPTH, D_MODEL), 0.01)
    return {'x': x, 'p': p, 'positions': positions, 'w_in': w_in,
            's5_a_re': s5_a_re, 's5_a_im': s5_a_im, 's5_log_dt': s5_log_dt,
            's5_b_re': s5_b_re, 's5_b_im': s5_b_im, 's5_c_re': s5_c_re, 's5_c_im': s5_c_im,
            's5_d': s5_d, 's5_w_glu': s5_w_glu,
            'mla_q_norm': mla_q_norm, 'mla_kv_norm': mla_kv_norm, 'mla_w_uq': mla_w_uq, 'mla_w_ukv': mla_w_ukv,
            'hg_lb_raw': hg_lb_raw, 'hg_norm': hg_norm, 'ret_gn_g': ret_gn_g, 'ret_gn_b': ret_gn_b,
            'w_branch': w_branch, 'w_o': w_o, 'ln1_g': ln1_g, 'ln1_b': ln1_b,
            'ff_w_gate': ff_w_gate, 'ff_w_up': ff_w_up, 'ff_w_down': ff_w_down,
            'moe_router': moe_router, 'moe_w_gate': moe_w_gate, 'moe_w_up': moe_w_up, 'moe_w_down': moe_w_down,
            'ple_w_gate': ple_w_gate, 'ple_w_proj': ple_w_proj, 'ln2_g': ln2_g, 'ln2_b': ln2_b}


def reference(x, p, positions, w_in, s5_a_re, s5_a_im, s5_log_dt, s5_b_re, s5_b_im, s5_c_re, s5_c_im,
              s5_d, s5_w_glu, mla_q_norm, mla_kv_norm, mla_w_uq, mla_w_ukv, hg_lb_raw, hg_norm,
              ret_gn_g, ret_gn_b, w_branch, w_o, ln1_g, ln1_b, ff_w_gate, ff_w_up, ff_w_down,
              moe_router, moe_w_gate, moe_w_up, moe_w_down, ple_w_gate, ple_w_proj, ln2_g, ln2_b):
    bsz, s, _ = x.shape
    cos_m, sin_m = rope_tables(positions, MLA_ROPE)
    cos_r, sin_r = rope_tables(positions, RET_K)
    lb_all = jnp.cumsum(jax.nn.softmax(hg_lb_raw.astype(jnp.float32), axis=0), axis=0)
    lb_all = lb_all - lb_all[0]
    for i in range(DEPTH):
        h = x @ w_in[i]
        (u_s5, c_q, c_kv, k_rope, hq, hf, hi, hg, rq, rk, rv, rg, gate_logits) = split_columns(h, IN_SPLITS)
        y_a = s5_mixer(u_s5, s5_a_re[i], s5_a_im[i], s5_log_dt[i], s5_b_re[i], s5_b_im[i],
                       s5_c_re[i], s5_c_im[i], s5_d[i], s5_w_glu[i])
        y_b = mla_mixer(c_q, c_kv, k_rope, cos_m, sin_m, mla_q_norm[i], mla_kv_norm[i], mla_w_uq[i], mla_w_ukv[i])
        y_c = hgrn2_mixer(hq, hf, hi, hg, lb_all[i], hg_norm[i])
        y_d = retention_mixer(rq, rk, rv, rg, cos_r, sin_r, ret_gn_g[i], ret_gn_b[i])
        branches = jnp.einsum('bsnc,ncd->bsnd', jnp.stack([y_a, y_b, y_c, y_d], axis=2), w_branch[i])
        gates = jax.nn.sigmoid(gate_logits.reshape(bsz, s, N_BRANCH, D_MODEL))
        mixed = jnp.sum(gates * branches, axis=2) @ w_o[i]
        x = layer_norm(ALPHA * x + mixed, ln1_g[i], ln1_b[i])
        if i % 2 == 0:
            f = swiglu(x, ff_w_gate[i // 2], ff_w_up[i // 2], ff_w_down[i // 2])
        else:
            f = moe_swiglu(x, moe_router[i // 2], moe_w_gate[i // 2], moe_w_up[i // 2], moe_w_down[i // 2])
        ple = jax.nn.sigmoid(x @ ple_w_gate[i]) * (p[i] @ ple_w_proj[i])
        x = layer_norm(ALPHA * x + f + ple, ln2_g[i], ln2_b[i])
    return x
```

```python
import functools
import math

import numpy as np
import jax
import jax.numpy as jnp
from jax import lax
from jax.experimental import pallas as pl
from jax.experimental.pallas import tpu as pltpu

F32 = jnp.float32
BF16 = jnp.bfloat16

D_MODEL = 1024
DEPTH = 4
PLE_DIM = 256
MIX_WIDTH = 256
N_BRANCH = 4
S5_GROUP = 16
S5_GROUPS = 16
S5_STATE = 64
MLA_HEADS = 4
MLA_NOPE = 64
MLA_ROPE = 32
MLA_V = 64
MLA_Q_LORA = 192
MLA_KV_LORA = 128
HEADS = 4
HEAD_DIM = 64
D_FF = 3584
N_EXPERTS = 8
ROPE_BASE = 10000.0
EPS = 1e-5
NEG_INF = -1e30
ALPHA = (2 * DEPTH) ** 0.25

LANES = 128
SUBLANES = 8
VMEM_LIMIT = 56 * 1024 * 1024

TM_PROJ = 512
TS_S5 = 64
TQ_ATT = 512
TT_HG = 128
TT_RET = 256
TM_FFN = 1024
TF_FFN = 512
TM_MOE = 1024
TM_ROUTE = 512
TD_DISPATCH = 512


def _cparams(*sem):
    return pltpu.CompilerParams(dimension_semantics=sem, vmem_limit_bytes=VMEM_LIMIT)


def _const_spec(shape):
    nd = len(shape)
    return pl.BlockSpec(shape, lambda *_: (0,) * nd, pipeline_mode=pl.Buffered(1))


def _dot(a, b):
    return jnp.dot(a, b, preferred_element_type=F32)


def _dot_nt(a, b):
    return lax.dot_general(a, b, (((1,), (1,)), ((), ())), preferred_element_type=F32)


def _dot_tn(a, b):
    return lax.dot_general(a, b, (((0,), (0,)), ((), ())), preferred_element_type=F32)


def _split2(x):
    hi = x.astype(BF16)
    lo = (x - hi.astype(F32)).astype(BF16)
    return hi, lo


def _split3(x):
    h1 = x.astype(BF16)
    r1 = x - h1.astype(F32)
    h2 = r1.astype(BF16)
    h3 = (r1 - h2.astype(F32)).astype(BF16)
    return h1, h2, h3


def _dot_exact_rhs(x, m):
    hi, lo = _split2(x)
    return _dot(hi, m) + _dot(lo, m)


def _sigmoid(x):
    return 1.0 / (1.0 + jnp.exp(-x))


def _layer_norm_rows(z, g, b):
    mu = jnp.mean(z, axis=-1, keepdims=True)
    zc = z - mu
    var = jnp.mean(zc * zc, axis=-1, keepdims=True)
    return zc * lax.rsqrt(var + EPS) * g + b


def _rope_kernel(pos_ref, fm_ref, fr_ref, cm_ref, sm_ref, cr_ref, sr_ref):
    pos = pos_ref[...]
    ang_m = pos * fm_ref[...]
    lane = lax.broadcasted_iota(jnp.int32, ang_m.shape, 1)
    in_pe = (lane >= MLA_NOPE) & (lane < MLA_NOPE + MLA_ROPE)
    cm_ref[...] = jnp.where(lane < MLA_NOPE, 1.0, jnp.where(in_pe, jnp.cos(ang_m), 0.0))
    sm_ref[...] = jnp.where(in_pe, jnp.sin(ang_m), 0.0)
    ang_r = pos * fr_ref[...]
    cr_ref[...] = jnp.cos(ang_r)
    sr_ref[...] = jnp.sin(ang_r)


def rope_tables(positions):
    t = positions.size
    tm = TM_PROJ
    pos = positions.reshape(t, 1).astype(F32)
    inv_m = ROPE_BASE ** (-jnp.arange(MLA_ROPE // 2, dtype=F32) / (MLA_ROPE // 2))
    inv_r = ROPE_BASE ** (-jnp.arange(HEAD_DIM // 2, dtype=F32) / (HEAD_DIM // 2))
    fm = jnp.zeros((1, LANES), F32).at[0, MLA_NOPE:MLA_NOPE + MLA_ROPE].set(jnp.concatenate([inv_m, inv_m]))
    fr = jnp.tile(jnp.concatenate([inv_r, inv_r]), HEADS).reshape(1, HEADS * HEAD_DIM)
    row = lambda w: pl.BlockSpec((tm, w), lambda i: (i, 0))
    return pl.pallas_call(
        _rope_kernel,
        grid=(t // tm,),
        in_specs=[row(1), _const_spec((1, LANES)), _const_spec((1, 256))],
        out_specs=[row(LANES), row(LANES), row(256), row(256)],
        out_shape=[jax.ShapeDtypeStruct((t, LANES), F32), jax.ShapeDtypeStruct((t, LANES), F32),
                   jax.ShapeDtypeStruct((t, 256), F32), jax.ShapeDtypeStruct((t, 256), F32)],
        compiler_params=_cparams("parallel"),
        name="rope_tables",
    )(pos, fm, fr)


W_MLA_IN = 640
W_RET_IN = 1536
_N_CHUNK = 256


def _inproj_kernel(x_ref, *refs):
    n = len(refs) // 2
    xb = x_ref[...].astype(BF16)
    for w_ref, o_ref in zip(refs[:n], refs[n:]):
        width = w_ref.shape[1]
        for c0 in range(0, width, _N_CHUNK):
            c1 = min(c0 + _N_CHUNK, width)
            o_ref[:, c0:c1] = _dot(xb, w_ref[:, c0:c1])


def in_projection(x, weights):
    t = x.shape[0]
    tm = TM_PROJ
    return pl.pallas_call(
        _inproj_kernel,
        grid=(t // tm,),
        in_specs=[pl.BlockSpec((tm, D_MODEL), lambda i: (i, 0))] + [_const_spec(w.shape) for w in weights],
        out_specs=[pl.BlockSpec((tm, w.shape[1]), lambda i: (i, 0)) for w in weights],
        out_shape=[jax.ShapeDtypeStruct((t, w.shape[1]), F32) for w in weights],
        compiler_params=_cparams("parallel"),
        name="in_projection",
    )(x, *weights)


S5_HALF = S5_GROUPS * S5_STATE


def _s5_kernel(u_ref, bmat_ref, ar_ref, ai_ref, cmat_ref, d_ref, wglu_ref, y_ref, bu_ref, hs_ref, h_ref):
    nb, ts, width = u_ref.shape
    n_half = S5_HALF // LANES

    @pl.when(pl.program_id(0) == 0)
    def _():
        h_ref[...] = jnp.zeros_like(h_ref)

    u = u_ref[...].reshape(nb * ts, width)
    ub = u.astype(BF16)
    for c in range(0, 2 * n_half, 2):
        res = _dot(ub, bmat_ref[:, c * LANES:(c + 2) * LANES])
        bu_ref[c] = res[:, :LANES]
        bu_ref[c + 1] = res[:, LANES:]
    ar = [jnp.broadcast_to(ar_ref[:, c * LANES:(c + 1) * LANES], (nb, LANES)) for c in range(n_half)]
    ai = [jnp.broadcast_to(ai_ref[:, c * LANES:(c + 1) * LANES], (nb, LANES)) for c in range(n_half)]
    h_re = [h_ref[c] for c in range(n_half)]
    h_im = [h_ref[n_half + c] for c in range(n_half)]
    for n in range(ts):
        rows = pl.ds(n, nb, stride=ts)
        for c in range(n_half):
            new_re = ar[c] * h_re[c] - ai[c] * h_im[c] + bu_ref[c, rows, :]
            new_im = ar[c] * h_im[c] + ai[c] * h_re[c] + bu_ref[n_half + c, rows, :]
            h_re[c], h_im[c] = new_re, new_im
            hs_ref[c, rows, :] = new_re
            hs_ref[n_half + c, rows, :] = new_im
    for c in range(n_half):
        h_ref[c] = h_re[c]
        h_ref[n_half + c] = h_im[c]
    hs = jnp.concatenate([hs_ref[c].astype(BF16) for c in range(2 * n_half)], axis=-1)
    y = _dot(hs, cmat_ref[...]) + d_ref[...] * u
    y = jax.nn.gelu(y)
    y = y * _sigmoid(_dot(y.astype(BF16), wglu_ref[...]))
    y_ref[...] = y.reshape(nb, ts, width)


def s5_mixer(u, bsz, seq, prm):
    ts = TS_S5
    u3 = u.reshape(bsz, seq, MIX_WIDTH)
    blk = pl.BlockSpec((bsz, ts, MIX_WIDTH), lambda i: (0, i, 0))
    y = pl.pallas_call(
        _s5_kernel,
        grid=(seq // ts,),
        in_specs=[blk, _const_spec((MIX_WIDTH, 2 * S5_HALF)), _const_spec((1, S5_HALF)), _const_spec((1, S5_HALF)),
                  _const_spec((2 * S5_HALF, MIX_WIDTH)), _const_spec((1, MIX_WIDTH)),
                  _const_spec((MIX_WIDTH, MIX_WIDTH))],
        out_specs=blk,
        out_shape=jax.ShapeDtypeStruct((bsz, seq, MIX_WIDTH), F32),
        scratch_shapes=[pltpu.VMEM((2 * S5_HALF // LANES, bsz * ts, LANES), F32),
                        pltpu.VMEM((2 * S5_HALF // LANES, bsz * ts, LANES), F32),
                        pltpu.VMEM((2 * S5_HALF // LANES, bsz, LANES), F32)],
        compiler_params=_cparams("arbitrary"),
        name="s5_mixer",
    )(u3, prm["bmat"], prm["a_re"], prm["a_im"], prm["cmat"], prm["d"], prm["w_glu"])
    return y.reshape(bsz * seq, MIX_WIDTH)


def s5_params(a_re, a_im, log_dt, b_re, b_im, c_re, c_im, d_skip, w_glu):
    dt = jnp.exp(log_dt)[:, None]
    mag = jnp.exp(a_re * dt)
    abar_re, abar_im = mag * jnp.cos(a_im * dt), mag * jnp.sin(a_im * dt)
    den = a_re * a_re + a_im * a_im
    num_re, num_im = abar_re - 1.0, abar_im
    coef_re = (num_re * a_re + num_im * a_im) / den
    coef_im = (num_im * a_re - num_re * a_im) / den
    bbar_re = coef_re[..., None] * b_re - coef_im[..., None] * b_im
    bbar_im = coef_re[..., None] * b_im + coef_im[..., None] * b_re
    eye = jnp.eye(S5_GROUPS, dtype=F32)
    b_bd_re = jnp.einsum("gpc,gh->gchp", bbar_re, eye).reshape(MIX_WIDTH, S5_HALF)
    b_bd_im = jnp.einsum("gpc,gh->gchp", bbar_im, eye).reshape(MIX_WIDTH, S5_HALF)
    c_bd_re = jnp.einsum("gcp,gh->gphc", c_re, eye).reshape(S5_HALF, MIX_WIDTH)
    c_bd_im = jnp.einsum("gcp,gh->gphc", c_im, eye).reshape(S5_HALF, MIX_WIDTH)
    return {
        "bmat": jnp.concatenate([b_bd_re, b_bd_im], axis=1).astype(BF16),
        "cmat": jnp.concatenate([c_bd_re, -c_bd_im], axis=0).astype(BF16),
        "a_re": abar_re.reshape(1, S5_HALF), "a_im": abar_im.reshape(1, S5_HALF),
        "d": d_skip.reshape(1, MIX_WIDTH), "w_glu": w_glu.astype(BF16),
    }


MLA_QK_PAD = 128


def _mla_prep_kernel(h_ref, cos_ref, sin_ref, gq_ref, gkv_ref, wq_ref, wqr_ref, wk_ref, wv_ref,
                     q_ref, k_ref, v_ref):
    h = h_ref[...]
    cos = cos_ref[...]
    sin = sin_ref[...]
    c_q = h[:, 0:256]
    c_kv = h[:, 256:384]
    k_pe = h[:, 384:512] * cos + h[:, 512:640] * sin
    qn = c_q * lax.rsqrt(jnp.sum(c_q * c_q, axis=-1, keepdims=True) / MLA_Q_LORA + EPS) * gq_ref[...]
    kvn = c_kv * lax.rsqrt(jnp.mean(c_kv * c_kv, axis=-1, keepdims=True) + EPS) * gkv_ref[...]
    qn = qn.astype(BF16)
    kvn = kvn.astype(BF16)
    scale = (MLA_NOPE + MLA_ROPE) ** -0.5
    for hd in range(MLA_HEADS):
        q = _dot(qn, wq_ref[hd]) * cos + _dot(qn, wqr_ref[hd]) * sin
        q_ref[hd] = (q * scale).astype(BF16)
        k_ref[hd] = (_dot(kvn, wk_ref[hd]) + k_pe).astype(BF16)
        v_ref[hd] = _dot(kvn, wv_ref[hd]).astype(BF16)


def _flash_kernel(q_ref, k_ref, v_ref, o_ref, m_ref, l_ref, acc_ref):
    qi = pl.program_id(1)
    ki = pl.program_id(2)
    tq = q_ref.shape[1]
    tk = k_ref.shape[1]

    @pl.when(ki == 0)
    def _():
        m_ref[...] = jnp.full_like(m_ref, NEG_INF)
        l_ref[...] = jnp.zeros_like(l_ref)
        acc_ref[...] = jnp.zeros_like(acc_ref)

    def update(masked):
        if masked:
            row = lax.broadcasted_iota(jnp.int32, (tq, tk), 0)
            col = lax.broadcasted_iota(jnp.int32, (tq, tk), 1)
            keep = col <= row
        for hd in range(MLA_HEADS):
            s = _dot_nt(q_ref[hd], k_ref[hd])
            if masked:
                s = jnp.where(keep, s, NEG_INF)
            m_old = m_ref[hd]
            m_new = jnp.maximum(m_old, jnp.max(s, axis=-1, keepdims=True))
            p = jnp.exp(s - m_new)
            corr = jnp.exp(m_old - m_new)
            l_ref[hd] = corr * l_ref[hd] + jnp.sum(p, axis=-1, keepdims=True)
            acc_ref[hd] = corr * acc_ref[hd] + _dot(p.astype(BF16), v_ref[hd])
            m_ref[hd] = m_new

    @pl.when(ki < qi)
    def _():
        update(False)

    @pl.when(ki == qi)
    def _():
        update(True)
        o_ref[...] = jnp.concatenate([acc_ref[hd] / l_ref[hd] for hd in range(MLA_HEADS)], axis=-1)


def mla_mixer(h_mla, cos_m, sin_m, bsz, seq, prm):
    t = bsz * seq
    tm = TM_PROJ
    row = lambda w: pl.BlockSpec((tm, w), lambda i: (i, 0))
    hrow = lambda w: pl.BlockSpec((MLA_HEADS, tm, w), lambda i: (0, i, 0))
    q, k, v = pl.pallas_call(
        _mla_prep_kernel,
        grid=(t // tm,),
        in_specs=[row(W_MLA_IN), row(LANES), row(LANES), _const_spec((1, 256)), _const_spec((1, MLA_KV_LORA)),
                  _const_spec((MLA_HEADS, 256, MLA_QK_PAD)), _const_spec((MLA_HEADS, 256, MLA_QK_PAD)),
                  _const_spec((MLA_HEADS, MLA_KV_LORA, MLA_QK_PAD)), _const_spec((MLA_HEADS, MLA_KV_LORA, MLA_V))],
        out_specs=[hrow(MLA_QK_PAD), hrow(MLA_QK_PAD), hrow(MLA_V)],
        out_shape=[jax.ShapeDtypeStruct((MLA_HEADS, t, MLA_QK_PAD), BF16),
                   jax.ShapeDtypeStruct((MLA_HEADS, t, MLA_QK_PAD), BF16),
                   jax.ShapeDtypeStruct((MLA_HEADS, t, MLA_V), BF16)],
        compiler_params=_cparams("parallel"),
        name="mla_prep",
    )(h_mla, cos_m, sin_m, prm["gq"], prm["gkv"], prm["wq"], prm["wqr"], prm["wk"], prm["wv"])

    tq = TQ_ATT
    nq = seq // tq
    q_spec = pl.BlockSpec((MLA_HEADS, tq, MLA_QK_PAD), lambda b, i, j: (0, b * nq + i, 0))
    k_spec = pl.BlockSpec((MLA_HEADS, tq, MLA_QK_PAD), lambda b, i, j: (0, b * nq + jnp.minimum(i, j), 0))
    v_spec = pl.BlockSpec((MLA_HEADS, tq, MLA_V), lambda b, i, j: (0, b * nq + jnp.minimum(i, j), 0))
    return pl.pallas_call(
        _flash_kernel,
        grid=(bsz, nq, nq),
        in_specs=[q_spec, k_spec, v_spec],
        out_specs=pl.BlockSpec((tq, MLA_HEADS * MLA_V), lambda b, i, j: (b * nq + i, 0)),
        out_shape=jax.ShapeDtypeStruct((t, MLA_HEADS * MLA_V), F32),
        scratch_shapes=[pltpu.VMEM((MLA_HEADS, tq, 1), F32), pltpu.VMEM((MLA_HEADS, tq, 1), F32),
                        pltpu.VMEM((MLA_HEADS, tq, MLA_V), F32)],
        compiler_params=_cparams("parallel", "parallel", "arbitrary"),
        name="mla_flash",
    )(q, k, v)


def _rot_half_cols(w, width):
    shp = w.shape
    w = w.reshape(shp[0], -1, 2, width // 2)
    return jnp.stack([-w[:, :, 1], w[:, :, 0]], axis=2).reshape(shp)


def mla_params(q_norm, kv_norm, w_uq, w_ukv):
    dq = MLA_NOPE + MLA_ROPE
    wq = jnp.zeros((MLA_HEADS, 256, MLA_QK_PAD), F32)
    wqr = jnp.zeros((MLA_HEADS, 256, MLA_QK_PAD), F32)
    w_uq_h = w_uq.reshape(MLA_Q_LORA, MLA_HEADS, dq).transpose(1, 0, 2)
    wq = wq.at[:, :MLA_Q_LORA, :dq].set(w_uq_h)
    pe = w_uq_h[:, :, MLA_NOPE:]
    pe_rot = jnp.concatenate([-pe[..., MLA_ROPE // 2:], pe[..., :MLA_ROPE // 2]], axis=-1)
    wqr = wqr.at[:, :MLA_Q_LORA, MLA_NOPE:dq].set(pe_rot)
    w_kv_h = w_ukv.reshape(MLA_KV_LORA, MLA_HEADS, MLA_NOPE + MLA_V).transpose(1, 0, 2)
    wk = jnp.zeros((MLA_HEADS, MLA_KV_LORA, MLA_QK_PAD), F32).at[:, :, :MLA_NOPE].set(w_kv_h[:, :, :MLA_NOPE])
    wv = w_kv_h[:, :, MLA_NOPE:]
    gq = jnp.zeros((1, 256), F32).at[0, :MLA_Q_LORA].set(q_norm)
    return {"wq": wq.astype(BF16), "wqr": wqr.astype(BF16), "wk": wk.astype(BF16), "wv": wv.astype(BF16),
            "gq": gq, "gkv": kv_norm.reshape(1, MLA_KV_LORA)}


def _head_mask(shape, hd, axis):
    return (lax.broadcasted_iota(jnp.int32, shape, axis) // HEAD_DIM) == hd


def _hgrn_kernel(h_ref, lb_ref, ng_ref, alev_ref, acum_ref, blk_ref, o_ref, st_ref):
    tt = h_ref.shape[0]
    width = HEADS * HEAD_DIM
    n_lev = alev_ref.shape[0]

    @pl.when(pl.program_id(1) == 0)
    def _():
        st_ref[...] = jnp.zeros_like(st_ref)

    q = h_ref[:, 0:width]
    f = h_ref[:, width:2 * width]
    v = h_ref[:, 2 * width:3 * width]
    g = h_ref[:, 3 * width:4 * width]
    lb = lb_ref[...]
    log_sig = jnp.minimum(f, 0.0) - jnp.log1p(jnp.exp(-jnp.abs(f)))
    a = jnp.log(lb)
    b = jnp.log1p(-lb) + log_sig
    log_f = jnp.maximum(a, b) + jnp.log1p(jnp.exp(-jnp.abs(a - b)))
    k = (1.0 - lb) * _sigmoid(-f)
    qf = q * _sigmoid(q)
    lf3 = _split3(log_f)

    def decay_sum(a_mat):
        return _dot(a_mat, lf3[0]) + _dot(a_mat, lf3[1]) + _dot(a_mat, lf3[2])

    t_row = lax.broadcasted_iota(jnp.int32, (tt, width), 0)
    t_i = lax.broadcasted_iota(jnp.int32, (tt, tt), 0)
    s_i = lax.broadcasted_iota(jnp.int32, (tt, tt), 1)
    hmask = [_head_mask((tt, width), hd, 1) for hd in range(HEADS)]
    attn = [jnp.zeros((tt, tt), F32) for _ in range(HEADS)]
    for lev in range(n_lev):
        x = jnp.exp(decay_sum(alev_ref[lev]))
        right = ((t_row >> lev) & 1) == 1
        qt = jnp.where(right, qf * x, 0.0)
        kt = jnp.where(right, 0.0, k * x).astype(BF16)
        same = (t_i >> (lev + 1)) == (s_i >> (lev + 1))
        for hd in range(HEADS):
            s = _dot_nt(jnp.where(hmask[hd], qt, 0.0).astype(BF16), kt)
            attn[hd] = attn[hd] + jnp.where(same, s, 0.0)
    o = jnp.zeros((tt, width), F32)
    vb = v.astype(BF16)
    for hd in range(HEADS):
        o = o + _dot(attn[hd].astype(BF16), jnp.where(hmask[hd], vb, jnp.zeros_like(vb)))
    ones_blk = blk_ref[...]
    o = o + _dot_exact_rhs(qf * k, ones_blk) * v
    bc = decay_sum(acum_ref[...])
    st = st_ref[...]
    st_m = jnp.where(ones_blk > 0, st, 0.0).astype(BF16)
    o = o + _dot_nt((qf * jnp.exp(bc)).astype(BF16), st_m)
    b_last = bc[tt - 1:tt, :]
    k_out = (k * jnp.exp(b_last - bc)).astype(BF16)
    st_ref[...] = st * jnp.exp(b_last) + _dot_tn(vb, k_out)
    ms = _dot_exact_rhs(o * o, ones_blk) * (1.0 / HEAD_DIM)
    o = o * lax.rsqrt(ms + EPS) * ng_ref[...]
    o_ref[...] = o * (g * _sigmoid(g))


def _hgrn_level_matrices(tt):
    n_lev = int(math.log2(tt))
    t = np.arange(tt)[:, None]
    j = np.arange(tt)[None, :]
    mats = []
    for lev in range(n_lev):
        half = 1 << lev
        mid = (t >> (lev + 1) << (lev + 1)) + half
        right = ((t >> lev) & 1) == 1
        m = np.where(right, (j >= mid) & (j <= t), (j > t) & (j < mid))
        mats.append(m.astype(np.float32))
    acum = (j <= t).astype(np.float32)
    return jnp.asarray(np.stack(mats), BF16), jnp.asarray(acum, BF16)


def _head_block_ones():
    i = np.arange(HEADS * HEAD_DIM)
    return jnp.asarray((i[:, None] // HEAD_DIM == i[None, :] // HEAD_DIM).astype(np.float32), BF16)


def hgrn2_mixer(h_hg, lb, norm_g, bsz, seq):
    t = bsz * seq
    tt = TT_HG
    width = HEADS * HEAD_DIM
    alev, acum = _hgrn_level_matrices(tt)
    nt = seq // tt
    return pl.pallas_call(
        _hgrn_kernel,
        grid=(bsz, nt),
        in_specs=[pl.BlockSpec((tt, 4 * width), lambda b, i: (b * nt + i, 0)),
                  _const_spec((1, width)), _const_spec((1, width)),
                  _const_spec(alev.shape), _const_spec(acum.shape), _const_spec((width, width))],
        out_specs=pl.BlockSpec((tt, width), lambda b, i: (b * nt + i, 0)),
        out_shape=jax.ShapeDtypeStruct((t, width), F32),
        scratch_shapes=[pltpu.VMEM((width, width), F32)],
        compiler_params=_cparams("parallel", "arbitrary"),
        name="hgrn2_mixer",
    )(h_hg, lb.reshape(1, width), norm_g.reshape(1, width), alev, acum, _head_block_ones())


def _ret_kernel(h_ref, cos_ref, sin_ref, dmat_ref, qdec_ref, kdec_ref, cdec_ref, blk_ref, g_ref, b_ref,
                o_ref, st_ref):
    tt = h_ref.shape[0]
    width = HEADS * HEAD_DIM

    @pl.when(pl.program_id(1) == 0)
    def _():
        st_ref[...] = jnp.zeros_like(st_ref)

    cos = cos_ref[...]
    sin = sin_ref[...]
    q = (h_ref[:, 0:width] * cos + h_ref[:, 4 * width:5 * width] * sin) * (HEAD_DIM ** -0.5)
    k = h_ref[:, width:2 * width] * cos + h_ref[:, 5 * width:6 * width] * sin
    v = h_ref[:, 2 * width:3 * width]
    g = h_ref[:, 3 * width:4 * width]
    kb = k.astype(BF16)
    vb = v.astype(BF16)
    ones_blk = blk_ref[...]
    o = jnp.zeros((tt, width), F32)
    for hd in range(HEADS):
        hm = _head_mask((tt, width), hd, 1)
        s = _dot_nt(jnp.where(hm, q, 0.0).astype(BF16), kb) * dmat_ref[hd]
        o = o + _dot(s.astype(BF16), jnp.where(hm, vb, jnp.zeros_like(vb)))
    st = st_ref[...]
    st_m = jnp.where(ones_blk > 0, st, 0.0).astype(BF16)
    o = o + _dot((q * qdec_ref[...]).astype(BF16), st_m)
    st_ref[...] = st * cdec_ref[...] + _dot_tn((k * kdec_ref[...]).astype(BF16), vb)
    mu = _dot_exact_rhs(o, ones_blk) * (1.0 / HEAD_DIM)
    oc = o - mu
    var = _dot_exact_rhs(oc * oc, ones_blk) * (1.0 / HEAD_DIM)
    o = oc * lax.rsqrt(var + EPS) * g_ref[...] + b_ref[...]
    o_ref[...] = (g * _sigmoid(g)) * o


def _retention_tables(tt):
    log_gamma = jnp.log(1.0 - 2.0 ** (-5.0 - jnp.arange(HEADS, dtype=F32)))
    t_idx = jnp.arange(tt, dtype=F32)
    rel = t_idx[:, None] - t_idx[None, :]
    dmat = jnp.where(rel >= 0, jnp.exp(jnp.maximum(rel, 0.0)[None] * log_gamma[:, None, None]), 0.0)
    lanes = lambda a: jnp.repeat(a, HEAD_DIM, axis=-1)
    qdec = lanes(jnp.exp((t_idx + 1.0)[:, None] * log_gamma[None, :]))
    kdec = lanes(jnp.exp((tt - 1 - t_idx)[:, None] * log_gamma[None, :]))
    cdec = lanes(jnp.exp(tt * log_gamma)[None, :])
    return dmat, qdec, kdec, cdec


def retention_mixer(h_ret, cos_r, sin_r, gn_g, gn_b, bsz, seq):
    t = bsz * seq
    tt = TT_RET
    width = HEADS * HEAD_DIM
    nt = seq // tt
    dmat, qdec, kdec, cdec = _retention_tables(tt)
    row = lambda w: pl.BlockSpec((tt, w), lambda b, i: (b * nt + i, 0))
    return pl.pallas_call(
        _ret_kernel,
        grid=(bsz, nt),
        in_specs=[row(W_RET_IN), row(width), row(width), _const_spec((HEADS, tt, tt)), _const_spec((tt, width)),
                  _const_spec((tt, width)), _const_spec((1, width)), _const_spec((width, width)),
                  _const_spec((1, width)), _const_spec((1, width))],
        out_specs=row(width),
        out_shape=jax.ShapeDtypeStruct((t, width), F32),
        scratch_shapes=[pltpu.VMEM((width, width), F32)],
        compiler_params=_cparams("parallel", "arbitrary"),
        name="retention_mixer",
    )(h_ret, cos_r, sin_r, dmat, qdec, kdec, cdec, _head_block_ones(), gn_g.reshape(1, width),
      gn_b.reshape(1, width))


def _store_token_tiles(ref, val):
    tm = val.shape[0]
    for j in range(D_MODEL // LANES):
        ref[pl.ds(j, tm, stride=SUBLANES), :] = val[:, j * LANES:(j + 1) * LANES]


def _load_token_tiles(ref, tm):
    return jnp.concatenate([ref[pl.ds(j, tm, stride=SUBLANES), :] for j in range(D_MODEL // LANES)], axis=-1)


def _mix_kernel(x_ref, ya_ref, yb_ref, yc_ref, yd_ref, wg_ref, wb_ref, wo_ref, g_ref, b_ref, *out_refs):
    x = x_ref[...]
    xb = x.astype(BF16)
    acc = jnp.zeros(x.shape, F32)
    for n, y_ref in enumerate((ya_ref, yb_ref, yc_ref, yd_ref)):
        gate = _sigmoid(_dot(xb, wg_ref[:, n * D_MODEL:(n + 1) * D_MODEL]))
        acc = acc + gate * _dot(y_ref[...].astype(BF16), wb_ref[n])
    z = ALPHA * x + _dot(acc.astype(BF16), wo_ref[...])
    x1 = _layer_norm_rows(z, g_ref[...], b_ref[...])
    out_refs[0][...] = x1
    if len(out_refs) > 1:
        _store_token_tiles(out_refs[1], x1)


def mix_branches(x, ys, w_gate, w_branch, w_o, ln_g, ln_b, with_token_tiles):
    t = x.shape[0]
    tm = TM_PROJ
    row = lambda w: pl.BlockSpec((tm, w), lambda i: (i, 0))
    out_specs = [row(D_MODEL)]
    out_shape = [jax.ShapeDtypeStruct((t, D_MODEL), F32)]
    if with_token_tiles:
        out_specs.append(pl.BlockSpec((tm * SUBLANES, LANES), lambda i: (i, 0)))
        out_shape.append(jax.ShapeDtypeStruct((t * SUBLANES, LANES), F32))
    return pl.pallas_call(
        _mix_kernel,
        grid=(t // tm,),
        in_specs=[row(D_MODEL)] + [row(MIX_WIDTH)] * 4 +
                 [_const_spec((D_MODEL, N_BRANCH * D_MODEL)), _const_spec((N_BRANCH, MIX_WIDTH, D_MODEL)),
                  _const_spec((D_MODEL, D_MODEL)), _const_spec((1, D_MODEL)), _const_spec((1, D_MODEL))],
        out_specs=out_specs,
        out_shape=out_shape,
        compiler_params=_cparams("parallel"),
        name="mix_branches",
    )(x, *ys, w_gate, w_branch, w_o, ln_g.reshape(1, D_MODEL), ln_b.reshape(1, D_MODEL))


def _ffn_kernel(x_ref, wg_ref, wu_ref, wd_ref, o_ref, xb_ref, acc_ref):
    c = pl.program_id(1)

    @pl.when(c == 0)
    def _():
        xb_ref[...] = x_ref[...].astype(BF16)
        acc_ref[...] = jnp.zeros_like(acc_ref)

    xb = xb_ref[...]
    gate = _dot(xb, wg_ref[...])
    act = gate * _sigmoid(gate) * _dot(xb, wu_ref[...])
    acc_ref[...] += _dot(act.astype(BF16), wd_ref[...])

    @pl.when(c == pl.num_programs(1) - 1)
    def _():
        o_ref[...] = acc_ref[...]


def dense_ffn(x, w_gate, w_up, w_down):
    t = x.shape[0]
    tm, tf = TM_FFN, TF_FFN
    return pl.pallas_call(
        _ffn_kernel,
        grid=(t // tm, D_FF // tf),
        in_specs=[pl.BlockSpec((tm, D_MODEL), lambda i, c: (i, 0)),
                  pl.BlockSpec((D_MODEL, tf), lambda i, c: (0, c)),
                  pl.BlockSpec((D_MODEL, tf), lambda i, c: (0, c)),
                  pl.BlockSpec((tf, D_MODEL), lambda i, c: (c, 0))],
        out_specs=pl.BlockSpec((tm, D_MODEL), lambda i, c: (i, 0)),
        out_shape=jax.ShapeDtypeStruct((t, D_MODEL), F32),
        scratch_shapes=[pltpu.VMEM((tm, D_MODEL), BF16), pltpu.VMEM((tm, D_MODEL), F32)],
        compiler_params=_cparams("parallel", "arbitrary"),
        name="dense_ffn",
    )(x, w_gate, w_up, w_down)


def _router_kernel(x_ref, wh_ref, wl_ref, upper_ref, meta_ref, cnt_ref, carry_ref):
    tm = x_ref.shape[0]

    @pl.when(pl.program_id(0) == 0)
    def _():
        carry_ref[...] = jnp.zeros_like(carry_ref)

    xh, xl = _split2(x_ref[...])
    wh = wh_ref[...]
    logits = _dot_nt(wh, xh) + _dot_nt(wh, xl) + _dot_nt(wl_ref[...], xh)
    e_idx = lax.broadcasted_iota(jnp.int32, logits.shape, 0)
    m1 = jnp.max(logits, axis=0, keepdims=True)
    i1 = jnp.min(jnp.where(logits == m1, e_idx, N_EXPERTS), axis=0, keepdims=True)
    rest = jnp.where(e_idx == i1, -jnp.inf, logits)
    m2 = jnp.max(rest, axis=0, keepdims=True)
    i2 = jnp.min(jnp.where(rest == m2, e_idx, N_EXPERTS), axis=0, keepdims=True)
    ex = jnp.exp(m2 - m1)
    w1 = 1.0 / (1.0 + ex)
    w2 = ex / (1.0 + ex)
    sel1 = e_idx == i1
    sel2 = e_idx == i2
    onehot = jnp.where(sel1 | sel2, 1.0, 0.0)
    carry = carry_ref[...]
    ranks = _dot(onehot.astype(BF16), upper_ref[...]) + carry[:, 0:1]
    r1 = jnp.sum(jnp.where(sel1, ranks, 0.0), axis=0, keepdims=True)
    r2 = jnp.sum(jnp.where(sel2, ranks, 0.0), axis=0, keepdims=True)
    row = lax.broadcasted_iota(jnp.int32, (SUBLANES, tm), 0)
    vals = (i1.astype(F32), i2.astype(F32), w1, w2, r1, r2)
    meta = jnp.zeros((SUBLANES, tm), F32)
    for n, val in enumerate(vals):
        meta = jnp.where(row == n, val, meta)
    meta_ref[...] = meta
    carry = carry + jnp.sum(onehot, axis=1, keepdims=True)
    carry_ref[...] = carry
    cnt_ref[...] = carry


def moe_router(x, w_router):
    t = x.shape[0]
    tm = TM_ROUTE
    wt = w_router.T
    wh = wt.astype(BF16)
    wl = (wt - wh.astype(F32)).astype(BF16)
    upper = jnp.asarray(np.triu(np.ones((tm, tm), np.float32), 1), BF16)
    return pl.pallas_call(
        _router_kernel,
        grid=(t // tm,),
        in_specs=[pl.BlockSpec((tm, D_MODEL), lambda i: (i, 0)), _const_spec((N_EXPERTS, D_MODEL)),
                  _const_spec((N_EXPERTS, D_MODEL)), _const_spec((tm, tm))],
        out_specs=[pl.BlockSpec((SUBLANES, tm), lambda i: (0, i)), pl.BlockSpec((N_EXPERTS, LANES), lambda i: (0, 0))],
        out_shape=[jax.ShapeDtypeStruct((SUBLANES, t), F32), jax.ShapeDtypeStruct((N_EXPERTS, LANES), F32)],
        scratch_shapes=[pltpu.VMEM((N_EXPERTS, LANES), F32)],
        compiler_params=_cparams("arbitrary"),
        name="moe_router",
    )(x, wh, wl, upper)


def _row_copy(src_ref, dst_ref, src_tok, dst_tok, sem):
    return pltpu.make_async_copy(src_ref.at[pl.ds(pl.multiple_of(src_tok * SUBLANES, SUBLANES), SUBLANES), :],
                                 dst_ref.at[pl.ds(pl.multiple_of(dst_tok * SUBLANES, SUBLANES), SUBLANES), :], sem)


def _dispatch_kernel(dest_ref, x_ref, init_ref, xs_ref, sem):
    del init_ref
    td = dest_ref.shape[2] // 2
    base = pl.program_id(0) * td

    def copies(r):
        return (_row_copy(x_ref, xs_ref, base + r, dest_ref[0, 0, 2 * r], sem),
                _row_copy(x_ref, xs_ref, base + r, dest_ref[0, 0, 2 * r + 1], sem))

    def start(r, carry):
        for cp in copies(r):
            cp.start()
        return carry

    def wait(r, carry):
        for cp in copies(r):
            cp.wait()
        return carry

    lax.fori_loop(0, td, start, 0)
    lax.fori_loop(0, td, wait, 0)


def moe_dispatch(x_tiles, dest, p_len):
    t = dest.shape[0]
    td = TD_DISPATCH
    dest3 = dest.reshape(t // td, 1, 2 * td)
    init = jnp.zeros((p_len * SUBLANES, LANES), F32)
    return pl.pallas_call(
        _dispatch_kernel,
        grid=(t // td,),
        in_specs=[pl.BlockSpec((1, 1, 2 * td), lambda i: (i, 0, 0), memory_space=pltpu.SMEM),
                  pl.BlockSpec(memory_space=pl.ANY), pl.BlockSpec(memory_space=pl.ANY)],
        out_specs=pl.BlockSpec(memory_space=pl.ANY),
        out_shape=jax.ShapeDtypeStruct((p_len * SUBLANES, LANES), F32),
        scratch_shapes=[pltpu.SemaphoreType.DMA(())],
        input_output_aliases={2: 0},
        compiler_params=pltpu.CompilerParams(dimension_semantics=("arbitrary",), has_side_effects=True),
        name="moe_dispatch",
    )(dest3, x_tiles, init)


def _combine_kernel(dest_ref, ys_ref, y2_ref, sem):
    td = dest_ref.shape[2] // 2
    base = pl.program_id(0) * td

    def copies(r):
        return (_row_copy(ys_ref, y2_ref.at[0], dest_ref[0, 0, 2 * r], base + r, sem),
                _row_copy(ys_ref, y2_ref.at[1], dest_ref[0, 0, 2 * r + 1], base + r, sem))

    def start(r, carry):
        for cp in copies(r):
            cp.start()
        return carry

    def wait(r, carry):
        for cp in copies(r):
            cp.wait()
        return carry

    lax.fori_loop(0, td, start, 0)
    lax.fori_loop(0, td, wait, 0)


def moe_combine(ys, dest):
    t = dest.shape[0]
    td = TD_DISPATCH
    dest3 = dest.reshape(t // td, 1, 2 * td)
    return pl.pallas_call(
        _combine_kernel,
        grid=(t // td,),
        in_specs=[pl.BlockSpec((1, 1, 2 * td), lambda i: (i, 0, 0), memory_space=pltpu.SMEM),
                  pl.BlockSpec(memory_space=pl.ANY)],
        out_specs=pl.BlockSpec(memory_space=pl.ANY),
        out_shape=jax.ShapeDtypeStruct((2, t * SUBLANES, LANES), F32),
        scratch_shapes=[pltpu.SemaphoreType.DMA(())],
        compiler_params=pltpu.CompilerParams(dimension_semantics=("arbitrary",), has_side_effects=True),
        name="moe_combine",
    )(dest3, ys)


def _expert_kernel(te_ref, nv_ref, xs_ref, wg_ref, wu_ref, wd_ref, ys_ref, xb_ref, acc_ref):
    i = pl.program_id(0)
    c = pl.program_id(1)
    tm = xb_ref.shape[0]
    valid = i < nv_ref[0]

    @pl.when(valid & (c == 0))
    def _():
        xb_ref[...] = _load_token_tiles(xs_ref, tm).astype(BF16)
        acc_ref[...] = jnp.zeros_like(acc_ref)

    @pl.when(valid)
    def _():
        xb = xb_ref[...]
        gate = _dot(xb, wg_ref[0].astype(BF16))
        act = gate * _sigmoid(gate) * _dot(xb, wu_ref[0].astype(BF16))
        acc_ref[...] += _dot(act.astype(BF16), wd_ref[0].astype(BF16))

    @pl.when(c == pl.num_programs(1) - 1)
    def _():
        _store_token_tiles(ys_ref, jnp.where(valid, acc_ref[...], 0.0))


def moe_experts(xs, tile_expert, n_valid, w_gate, w_up, w_down):
    tm, tf = TM_MOE, TF_FFN
    n_tiles = xs.shape[0] // (tm * SUBLANES)
    nc = D_FF // tf

    def chunk(i, c, te, nv):
        return jnp.where(i < nv[0], c, nc - 1)

    grid_spec = pltpu.PrefetchScalarGridSpec(
        num_scalar_prefetch=2,
        grid=(n_tiles, nc),
        in_specs=[pl.BlockSpec((tm * SUBLANES, LANES), lambda i, c, te, nv: (jnp.minimum(i, nv[0] - 1), 0)),
                  pl.BlockSpec((1, D_MODEL, tf), lambda i, c, te, nv: (te[i], 0, chunk(i, c, te, nv))),
                  pl.BlockSpec((1, D_MODEL, tf), lambda i, c, te, nv: (te[i], 0, chunk(i, c, te, nv))),
                  pl.BlockSpec((1, tf, D_MODEL), lambda i, c, te, nv: (te[i], chunk(i, c, te, nv), 0))],
        out_specs=pl.BlockSpec((tm * SUBLANES, LANES), lambda i, c, te, nv: (i, 0)),
        scratch_shapes=[pltpu.VMEM((tm, D_MODEL), BF16), pltpu.VMEM((tm, D_MODEL), F32)],
    )
    return pl.pallas_call(
        _expert_kernel,
        grid_spec=grid_spec,
        out_shape=jax.ShapeDtypeStruct(xs.shape, F32),
        compiler_params=_cparams("arbitrary", "arbitrary"),
        name="moe_experts",
    )(tile_expert, n_valid, xs, w_gate, w_up, w_down)


def moe_ffn(x, x_tiles, w_router, w_gate, w_up, w_down):
    t = x.shape[0]
    tm = TM_MOE
    meta, cnt = moe_router(x, w_router)
    counts = cnt[:, 0].astype(jnp.int32)
    padded = (counts + tm - 1) // tm * tm
    pad_end = jnp.cumsum(padded)
    pad_start = pad_end - padded
    experts = meta[0:2].astype(jnp.int32)
    dest = (pad_start[experts] + meta[4:6].astype(jnp.int32)).T
    n_tiles = (2 * t) // tm + N_EXPERTS
    n_valid = (pad_end[-1] // tm).astype(jnp.int32).reshape(1)
    tile_start = jnp.arange(n_tiles, dtype=jnp.int32) * tm
    tile_expert = jnp.minimum(jnp.sum(pad_end[None, :] <= tile_start[:, None], axis=1), N_EXPERTS - 1)
    last_expert = jnp.take(tile_expert, n_valid[0] - 1)
    tile_expert = jnp.where(jnp.arange(n_tiles) < n_valid[0], tile_expert, last_expert).astype(jnp.int32)
    xs = moe_dispatch(x_tiles, dest, n_tiles * tm)
    ys = moe_experts(xs, tile_expert, n_valid, w_gate, w_up, w_down)
    y2 = moe_combine(ys, dest)
    return y2, meta[2:4].T


def _post_kernel(moe, x_ref, p_ref, wpg_ref, wpp_ref, g_ref, b_ref, *refs):
    o_ref = refs[-1]
    x = x_ref[...]
    tm = x.shape[0]
    if moe:
        y2_ref, w_ref = refs[0], refs[1]
        w = w_ref[...]
        f = w[:, 0:1] * _load_token_tiles(y2_ref.at[0], tm) + w[:, 1:2] * _load_token_tiles(y2_ref.at[1], tm)
    else:
        f = refs[0][...]
    ple = _sigmoid(_dot(x.astype(BF16), wpg_ref[...])) * _dot(p_ref[...].astype(BF16), wpp_ref[...])
    o_ref[...] = _layer_norm_rows(ALPHA * x + f + ple, g_ref[...], b_ref[...])


def post_layer(x, p, w_ple_gate, w_ple_proj, ln_g, ln_b, f=None, y2=None, w2=None):
    t = x.shape[0]
    tm = TM_PROJ
    row = lambda w: pl.BlockSpec((tm, w), lambda i: (i, 0))
    in_specs = [row(D_MODEL), row(PLE_DIM), _const_spec((D_MODEL, D_MODEL)), _const_spec((PLE_DIM, D_MODEL)),
                _const_spec((1, D_MODEL)), _const_spec((1, D_MODEL))]
    args = [x, p, w_ple_gate, w_ple_proj, ln_g.reshape(1, D_MODEL), ln_b.reshape(1, D_MODEL)]
    moe = f is None
    if moe:
        in_specs += [pl.BlockSpec((2, tm * SUBLANES, LANES), lambda i: (0, i, 0)), row(2)]
        args += [y2, w2]
    else:
        in_specs += [row(D_MODEL)]
        args += [f]
    return pl.pallas_call(
        functools.partial(_post_kernel, moe),
        grid=(t // tm,),
        in_specs=in_specs,
        out_specs=row(D_MODEL),
        out_shape=jax.ShapeDtypeStruct((t, D_MODEL), F32),
        compiler_params=_cparams("parallel"),
        name="post_layer",
    )(*args)


def split_in_weights(w):
    sizes = (MIX_WIDTH, MLA_Q_LORA, MLA_KV_LORA, MLA_ROPE) + (HEADS * HEAD_DIM,) * 8 + (N_BRANCH * D_MODEL,)
    parts, start = [], 0
    for n in sizes:
        parts.append(w[:, start:start + n])
        start += n
    u_s5, c_q, c_kv, k_rope, hq, hf, hi, hg, rq, rk, rv, rg, gate = parts
    k_rot = jnp.concatenate([-k_rope[:, MLA_ROPE // 2:], k_rope[:, :MLA_ROPE // 2]], axis=1)
    w_mla = jnp.zeros((D_MODEL, W_MLA_IN), F32)
    w_mla = w_mla.at[:, 0:MLA_Q_LORA].set(c_q).at[:, 256:384].set(c_kv)
    w_mla = w_mla.at[:, 384 + MLA_NOPE:384 + MLA_NOPE + MLA_ROPE].set(k_rope)
    w_mla = w_mla.at[:, 512 + MLA_NOPE:512 + MLA_NOPE + MLA_ROPE].set(k_rot)
    w_hg = jnp.concatenate([hq, hf, hi, hg], axis=1)
    w_ret = jnp.concatenate([rq, rk, rv, rg, _rot_half_cols(rq, HEAD_DIM), _rot_half_cols(rk, HEAD_DIM)], axis=1)
    return [a.astype(BF16) for a in (w_mla, u_s5, w_hg, w_ret)], gate.astype(BF16)


def kernel(x, p, positions, w_in, s5_a_re, s5_a_im, s5_log_dt, s5_b_re, s5_b_im, s5_c_re, s5_c_im, s5_d, s5_w_glu, mla_q_norm, mla_kv_norm, mla_w_uq, mla_w_ukv, hg_lb_raw, hg_norm, ret_gn_g, ret_gn_b, w_branch, w_o, ln1_g, ln1_b, ff_w_gate, ff_w_up, ff_w_down, moe_router, moe_w_gate, moe_w_up, moe_w_down, ple_w_gate, ple_w_proj, ln2_g, ln2_b):
    bsz, seq, _ = x.shape
    t = bsz * seq
    depth = w_in.shape[0]
    x = x.reshape(t, D_MODEL)
    p = p.reshape(depth, t, PLE_DIM)
    cos_m, sin_m, cos_r, sin_r = rope_tables(positions)
    lb_all = jnp.cumsum(jax.nn.softmax(hg_lb_raw.astype(F32), axis=0), axis=0)
    lb_all = lb_all - lb_all[0]
    for i in range(depth):
        in_ws, w_gate = split_in_weights(w_in[i])
        h_mla, h_s5, h_hg, h_ret = in_projection(x, in_ws)
        y_a = s5_mixer(h_s5, bsz, seq, s5_params(s5_a_re[i], s5_a_im[i], s5_log_dt[i], s5_b_re[i], s5_b_im[i],
                                                s5_c_re[i], s5_c_im[i], s5_d[i], s5_w_glu[i]))
        y_b = mla_mixer(h_mla, cos_m, sin_m, bsz, seq,
                        mla_params(mla_q_norm[i], mla_kv_norm[i], mla_w_uq[i], mla_w_ukv[i]))
        y_c = hgrn2_mixer(h_hg, lb_all[i], hg_norm[i], bsz, seq)
        y_d = retention_mixer(h_ret, cos_r, sin_r, ret_gn_g[i], ret_gn_b[i], bsz, seq)
        is_moe = i % 2 == 1
        outs = mix_branches(x, (y_a, y_b, y_c, y_d), w_gate, w_branch[i].astype(BF16), w_o[i].astype(BF16),
                            ln1_g[i], ln1_b[i], with_token_tiles=is_moe)
        x = outs[0]
        ple_args = (p[i], ple_w_gate[i].astype(BF16), ple_w_proj[i].astype(BF16), ln2_g[i], ln2_b[i])
        if is_moe:
            y2, w2 = moe_ffn(x, outs[1], moe_router[i // 2], moe_w_gate[i // 2], moe_w_up[i // 2],
                             moe_w_down[i // 2])
            x = post_layer(x, *ple_args, y2=y2, w2=w2)
        else:
            f = dense_ffn(x, ff_w_gate[i // 2].astype(BF16), ff_w_up[i // 2].astype(BF16),
                          ff_w_down[i // 2].astype(BF16))
            x = post_layer(x, *ple_args, f=f)
    return x.reshape(bsz, seq, D_MODEL)
```

```python
import functools
import math

import numpy as np
import jax
import jax.numpy as jnp
from jax import lax
from jax.experimental import pallas as pl
from jax.experimental.pallas import tpu as pltpu

F32 = jnp.float32
BF16 = jnp.bfloat16

D_MODEL = 1024
DEPTH = 4
PLE_DIM = 256
MIX_WIDTH = 256
N_BRANCH = 4
S5_GROUP = 16
S5_GROUPS = 16
S5_STATE = 64
MLA_HEADS = 4
MLA_NOPE = 64
MLA_ROPE = 32
MLA_V = 64
MLA_Q_LORA = 192
MLA_KV_LORA = 128
HEADS = 4
HEAD_DIM = 64
D_FF = 3584
N_EXPERTS = 8
ROPE_BASE = 10000.0
EPS = 1e-5
NEG_INF = -1e30
ALPHA = (2 * DEPTH) ** 0.25

LANES = 128
SUBLANES = 8
VMEM_LIMIT = 56 * 1024 * 1024

TM_PROJ = 512
TS_S5 = 64
TQ_ATT = 512
TT_HG = 128
TT_RET = 256
TM_FFN = 1024
TF_FFN = 512
TM_MOE = 1024
TM_ROUTE = 512
TD_DISPATCH = 512


def _cparams(*sem):
    return pltpu.CompilerParams(dimension_semantics=sem, vmem_limit_bytes=VMEM_LIMIT)


def _const_spec(shape):
    nd = len(shape)
    return pl.BlockSpec(shape, lambda *_: (0,) * nd, pipeline_mode=pl.Buffered(1))


def _dot(a, b):
    return jnp.dot(a, b, preferred_element_type=F32)


def _dot_nt(a, b):
    return lax.dot_general(a, b, (((1,), (1,)), ((), ())), preferred_element_type=F32)


def _dot_tn(a, b):
    return lax.dot_general(a, b, (((0,), (0,)), ((), ())), preferred_element_type=F32)


def _split2(x):
    hi = x.astype(BF16)
    lo = (x - hi.astype(F32)).astype(BF16)
    return hi, lo


def _split3(x):
    h1 = x.astype(BF16)
    r1 = x - h1.astype(F32)
    h2 = r1.astype(BF16)
    h3 = (r1 - h2.astype(F32)).astype(BF16)
    return h1, h2, h3


def _dot_exact_rhs(x, m):
    hi, lo = _split2(x)
    return _dot(hi, m) + _dot(lo, m)


def _sigmoid(x):
    return 1.0 / (1.0 + jnp.exp(-x))


def _layer_norm_rows(z, g, b):
    mu = jnp.mean(z, axis=-1, keepdims=True)
    zc = z - mu
    var = jnp.mean(zc * zc, axis=-1, keepdims=True)
    return zc * lax.rsqrt(var + EPS) * g + b


def _rope_kernel(pos_ref, fm_ref, fr_ref, cm_ref, sm_ref, cr_ref, sr_ref):
    pos = pos_ref[...]
    ang_m = pos * fm_ref[...]
    lane = lax.broadcasted_iota(jnp.int32, ang_m.shape, 1)
    in_pe = (lane >= MLA_NOPE) & (lane < MLA_NOPE + MLA_ROPE)
    cm_ref[...] = jnp.where(lane < MLA_NOPE, 1.0, jnp.where(in_pe, jnp.cos(ang_m), 0.0))
    sm_ref[...] = jnp.where(in_pe, jnp.sin(ang_m), 0.0)
    ang_r = pos * fr_ref[...]
    cr_ref[...] = jnp.cos(ang_r)
    sr_ref[...] = jnp.sin(ang_r)


def rope_tables(positions):
    t = positions.size
    tm = TM_PROJ
    pos = positions.reshape(t, 1).astype(F32)
    inv_m = ROPE_BASE ** (-jnp.arange(MLA_ROPE // 2, dtype=F32) / (MLA_ROPE // 2))
    inv_r = ROPE_BASE ** (-jnp.arange(HEAD_DIM // 2, dtype=F32) / (HEAD_DIM // 2))
    fm = jnp.zeros((1, LANES), F32).at[0, MLA_NOPE:MLA_NOPE + MLA_ROPE].set(jnp.concatenate([inv_m, inv_m]))
    fr = jnp.tile(jnp.concatenate([inv_r, inv_r]), HEADS).reshape(1, HEADS * HEAD_DIM)
    row = lambda w: pl.BlockSpec((tm, w), lambda i: (i, 0))
    return pl.pallas_call(
        _rope_kernel,
        grid=(t // tm,),
        in_specs=[row(1), _const_spec((1, LANES)), _const_spec((1, 256))],
        out_specs=[row(LANES), row(LANES), row(256), row(256)],
        out_shape=[jax.ShapeDtypeStruct((t, LANES), F32), jax.ShapeDtypeStruct((t, LANES), F32),
                   jax.ShapeDtypeStruct((t, 256), F32), jax.ShapeDtypeStruct((t, 256), F32)],
        compiler_params=_cparams("parallel"),
        name="rope_tables",
    )(pos, fm, fr)


W_MLA_IN = 640
W_RET_IN = 1536
_N_CHUNK = 256


def _inproj_kernel(x_ref, *refs):
    n = len(refs) // 2
    xb = x_ref[...].astype(BF16)
    for w_ref, o_ref in zip(refs[:n], refs[n:]):
        width = w_ref.shape[1]
        for c0 in range(0, width, _N_CHUNK):
            c1 = min(c0 + _N_CHUNK, width)
            o_ref[:, c0:c1] = _dot(xb, w_ref[:, c0:c1])


def in_projection(x, weights):
    t = x.shape[0]
    tm = TM_PROJ
    return pl.pallas_call(
        _inproj_kernel,
        grid=(t // tm,),
        in_specs=[pl.BlockSpec((tm, D_MODEL), lambda i: (i, 0))] + [_const_spec(w.shape) for w in weights],
        out_specs=[pl.BlockSpec((tm, w.shape[1]), lambda i: (i, 0)) for w in weights],
        out_shape=[jax.ShapeDtypeStruct((t, w.shape[1]), F32) for w in weights],
        compiler_params=_cparams("parallel"),
        name="in_projection",
    )(x, *weights)


S5_HALF = S5_GROUPS * S5_STATE


def _s5_kernel(u_ref, bmat_ref, ar_ref, ai_ref, cmat_ref, d_ref, wglu_ref, y_ref, bu_ref, hs_ref, h_ref):
    nb, ts, width = u_ref.shape
    n_half = S5_HALF // LANES

    @pl.when(pl.program_id(0) == 0)
    def _():
        h_ref[...] = jnp.zeros_like(h_ref)

    u = u_ref[...].reshape(nb * ts, width)
    ub = u.astype(BF16)
    for c in range(0, 2 * n_half, 2):
        res = _dot(ub, bmat_ref[:, c * LANES:(c + 2) * LANES])
        bu_ref[c] = res[:, :LANES]
        bu_ref[c + 1] = res[:, LANES:]
    ar = [jnp.broadcast_to(ar_ref[:, c * LANES:(c + 1) * LANES], (nb, LANES)) for c in range(n_half)]
    ai = [jnp.broadcast_to(ai_ref[:, c * LANES:(c + 1) * LANES], (nb, LANES)) for c in range(n_half)]
    h_re = [h_ref[c] for c in range(n_half)]
    h_im = [h_ref[n_half + c] for c in range(n_half)]
    for n in range(ts):
        rows = pl.ds(n, nb, stride=ts)
        for c in range(n_half):
            new_re = ar[c] * h_re[c] - ai[c] * h_im[c] + bu_ref[c, rows, :]
            new_im = ar[c] * h_im[c] + ai[c] * h_re[c] + bu_ref[n_half + c, rows, :]
            h_re[c], h_im[c] = new_re, new_im
            hs_ref[c, rows, :] = new_re
            hs_ref[n_half + c, rows, :] = new_im
    for c in range(n_half):
        h_ref[c] = h_re[c]
        h_ref[n_half + c] = h_im[c]
    hs = jnp.concatenate([hs_ref[c].astype(BF16) for c in range(2 * n_half)], axis=-1)
    y = _dot(hs, cmat_ref[...]) + d_ref[...] * u
    y = jax.nn.gelu(y)
    y = y * _sigmoid(_dot(y.astype(BF16), wglu_ref[...]))
    y_ref[...] = y.reshape(nb, ts, width)


def s5_mixer(u, bsz, seq, prm):
    ts = TS_S5
    u3 = u.reshape(bsz, seq, MIX_WIDTH)
    blk = pl.BlockSpec((bsz, ts, MIX_WIDTH), lambda i: (0, i, 0))
    y = pl.pallas_call(
        _s5_kernel,
        grid=(seq // ts,),
        in_specs=[blk, _const_spec((MIX_WIDTH, 2 * S5_HALF)), _const_spec((1, S5_HALF)), _const_spec((1, S5_HALF)),
                  _const_spec((2 * S5_HALF, MIX_WIDTH)), _const_spec((1, MIX_WIDTH)),
                  _const_spec((MIX_WIDTH, MIX_WIDTH))],
        out_specs=blk,
        out_shape=jax.ShapeDtypeStruct((bsz, seq, MIX_WIDTH), F32),
        scratch_shapes=[pltpu.VMEM((2 * S5_HALF // LANES, bsz * ts, LANES), F32),
                        pltpu.VMEM((2 * S5_HALF // LANES, bsz * ts, LANES), F32),
                        pltpu.VMEM((2 * S5_HALF // LANES, bsz, LANES), F32)],
        compiler_params=_cparams("arbitrary"),
        name="s5_mixer",
    )(u3, prm["bmat"], prm["a_re"], prm["a_im"], prm["cmat"], prm["d"], prm["w_glu"])
    return y.reshape(bsz * seq, MIX_WIDTH)


def s5_params(a_re, a_im, log_dt, b_re, b_im, c_re, c_im, d_skip, w_glu):
    dt = jnp.exp(log_dt)[:, None]
    mag = jnp.exp(a_re * dt)
    abar_re, abar_im = mag * jnp.cos(a_im * dt), mag * jnp.sin(a_im * dt)
    den = a_re * a_re + a_im * a_im
    num_re, num_im = abar_re - 1.0, abar_im
    coef_re = (num_re * a_re + num_im * a_im) / den
    coef_im = (num_im * a_re - num_re * a_im) / den
    bbar_re = coef_re[..., None] * b_re - coef_im[..., None] * b_im
    bbar_im = coef_re[..., None] * b_im + coef_im[..., None] * b_re
    eye = jnp.eye(S5_GROUPS, dtype=F32)
    b_bd_re = jnp.einsum("gpc,gh->gchp", bbar_re, eye).reshape(MIX_WIDTH, S5_HALF)
    b_bd_im = jnp.einsum("gpc,gh->gchp", bbar_im, eye).reshape(MIX_WIDTH, S5_HALF)
    c_bd_re = jnp.einsum("gcp,gh->gphc", c_re, eye).reshape(S5_HALF, MIX_WIDTH)
    c_bd_im = jnp.einsum("gcp,gh->gphc", c_im, eye).reshape(S5_HALF, MIX_WIDTH)
    return {
        "bmat": jnp.concatenate([b_bd_re, b_bd_im], axis=1).astype(BF16),
        "cmat": jnp.concatenate([c_bd_re, -c_bd_im], axis=0).astype(BF16),
        "a_re": abar_re.reshape(1, S5_HALF), "a_im": abar_im.reshape(1, S5_HALF),
        "d": d_skip.reshape(1, MIX_WIDTH), "w_glu": w_glu.astype(BF16),
    }


MLA_QK_PAD = 128


def _mla_prep_kernel(h_ref, cos_ref, sin_ref, gq_ref, gkv_ref, wq_ref, wqr_ref, wk_ref, wv_ref,
                     q_ref, k_ref, v_ref):
    h = h_ref[...]
    cos = cos_ref[...]
    sin = sin_ref[...]
    c_q = h[:, 0:256]
    c_kv = h[:, 256:384]
    k_pe = h[:, 384:512] * cos + h[:, 512:640] * sin
    qn = c_q * lax.rsqrt(jnp.sum(c_q * c_q, axis=-1, keepdims=True) / MLA_Q_LORA + EPS) * gq_ref[...]
    kvn = c_kv * lax.rsqrt(jnp.mean(c_kv * c_kv, axis=-1, keepdims=True) + EPS) * gkv_ref[...]
    qn = qn.astype(BF16)
    kvn = kvn.astype(BF16)
    scale = (MLA_NOPE + MLA_ROPE) ** -0.5
    for hd in range(MLA_HEADS):
        q = _dot(qn, wq_ref[hd]) * cos + _dot(qn, wqr_ref[hd]) * sin
        q_ref[hd] = (q * scale).astype(BF16)
        k_ref[hd] = (_dot(kvn, wk_ref[hd]) + k_pe).astype(BF16)
        v_ref[hd] = _dot(kvn, wv_ref[hd]).astype(BF16)


def _flash_kernel(q_ref, k_ref, v_ref, o_ref, m_ref, l_ref, acc_ref):
    qi = pl.program_id(1)
    ki = pl.program_id(2)
    tq = q_ref.shape[1]
    tk = k_ref.shape[1]

    @pl.when(ki == 0)
    def _():
        m_ref[...] = jnp.full_like(m_ref, NEG_INF)
        l_ref[...] = jnp.zeros_like(l_ref)
        acc_ref[...] = jnp.zeros_like(acc_ref)

    def update(masked):
        if masked:
            row = lax.broadcasted_iota(jnp.int32, (tq, tk), 0)
            col = lax.broadcasted_iota(jnp.int32, (tq, tk), 1)
            keep = col <= row
        for hd in range(MLA_HEADS):
            s = _dot_nt(q_ref[hd], k_ref[hd])
            if masked:
                s = jnp.where(keep, s, NEG_INF)
            m_old = m_ref[hd]
            m_new = jnp.maximum(m_old, jnp.max(s, axis=-1, keepdims=True))
            p = jnp.exp(s - m_new)
            corr = jnp.exp(m_old - m_new)
            l_ref[hd] = corr * l_ref[hd] + jnp.sum(p, axis=-1, keepdims=True)
            acc_ref[hd] = corr * acc_ref[hd] + _dot(p.astype(BF16), v_ref[hd])
            m_ref[hd] = m_new

    @pl.when(ki < qi)
    def _():
        update(False)

    @pl.when(ki == qi)
    def _():
        update(True)
        o_ref[...] = jnp.concatenate([acc_ref[hd] / l_ref[hd] for hd in range(MLA_HEADS)], axis=-1)


def mla_mixer(h_mla, cos_m, sin_m, bsz, seq, prm):
    t = bsz * seq
    tm = TM_PROJ
    row = lambda w: pl.BlockSpec((tm, w), lambda i: (i, 0))
    hrow = lambda w: pl.BlockSpec((MLA_HEADS, tm, w), lambda i: (0, i, 0))
    q, k, v = pl.pallas_call(
        _mla_prep_kernel,
        grid=(t // tm,),
        in_specs=[row(W_MLA_IN), row(LANES), row(LANES), _const_spec((1, 256)), _const_spec((1, MLA_KV_LORA)),
                  _const_spec((MLA_HEADS, 256, MLA_QK_PAD)), _const_spec((MLA_HEADS, 256, MLA_QK_PAD)),
                  _const_spec((MLA_HEADS, MLA_KV_LORA, MLA_QK_PAD)), _const_spec((MLA_HEADS, MLA_KV_LORA, MLA_V))],
        out_specs=[hrow(MLA_QK_PAD), hrow(MLA_QK_PAD), hrow(MLA_V)],
        out_shape=[jax.ShapeDtypeStruct((MLA_HEADS, t, MLA_QK_PAD), BF16),
                   jax.ShapeDtypeStruct((MLA_HEADS, t, MLA_QK_PAD), BF16),
                   jax.ShapeDtypeStruct((MLA_HEADS, t, MLA_V), BF16)],
        compiler_params=_cparams("parallel"),
        name="mla_prep",
    )(h_mla, cos_m, sin_m, prm["gq"], prm["gkv"], prm["wq"], prm["wqr"], prm["wk"], prm["wv"])

    tq = TQ_ATT
    nq = seq // tq
    q_spec = pl.BlockSpec((MLA_HEADS, tq, MLA_QK_PAD), lambda b, i, j: (0, b * nq + i, 0))
    k_spec = pl.BlockSpec((MLA_HEADS, tq, MLA_QK_PAD), lambda b, i, j: (0, b * nq + jnp.minimum(i, j), 0))
    v_spec = pl.BlockSpec((MLA_HEADS, tq, MLA_V), lambda b, i, j: (0, b * nq + jnp.minimum(i, j), 0))
    return pl.pallas_call(
        _flash_kernel,
        grid=(bsz, nq, nq),
        in_specs=[q_spec, k_spec, v_spec],
        out_specs=pl.BlockSpec((tq, MLA_HEADS * MLA_V), lambda b, i, j: (b * nq + i, 0)),
        out_shape=jax.ShapeDtypeStruct((t, MLA_HEADS * MLA_V), F32),
        scratch_shapes=[pltpu.VMEM((MLA_HEADS, tq, 1), F32), pltpu.VMEM((MLA_HEADS, tq, 1), F32),
                        pltpu.VMEM((MLA_HEADS, tq, MLA_V), F32)],
        compiler_params=_cparams("parallel", "parallel", "arbitrary"),
        name="mla_flash",
    )(q, k, v)


def _rot_half_cols(w, width):
    shp = w.shape
    w = w.reshape(shp[0], -1, 2, width // 2)
    return jnp.stack([-w[:, :, 1], w[:, :, 0]], axis=2).reshape(shp)


def mla_params(q_norm, kv_norm, w_uq, w_ukv):
    dq = MLA_NOPE + MLA_ROPE
    wq = jnp.zeros((MLA_HEADS, 256, MLA_QK_PAD), F32)
    wqr = jnp.zeros((MLA_HEADS, 256, MLA_QK_PAD), F32)
    w_uq_h = w_uq.reshape(MLA_Q_LORA, MLA_HEADS, dq).transpose(1, 0, 2)
    wq = wq.at[:, :MLA_Q_LORA, :dq].set(w_uq_h)
    pe = w_uq_h[:, :, MLA_NOPE:]
    pe_rot = jnp.concatenate([-pe[..., MLA_ROPE // 2:], pe[..., :MLA_ROPE // 2]], axis=-1)
    wqr = wqr.at[:, :MLA_Q_LORA, MLA_NOPE:dq].set(pe_rot)
    w_kv_h = w_ukv.reshape(MLA_KV_LORA, MLA_HEADS, MLA_NOPE + MLA_V).transpose(1, 0, 2)
    wk = jnp.zeros((MLA_HEADS, MLA_KV_LORA, MLA_QK_PAD), F32).at[:, :, :MLA_NOPE].set(w_kv_h[:, :, :MLA_NOPE])
    wv = w_kv_h[:, :, MLA_NOPE:]
    gq = jnp.zeros((1, 256), F32).at[0, :MLA_Q_LORA].set(q_norm)
    return {"wq": wq.astype(BF16), "wqr": wqr.astype(BF16), "wk": wk.astype(BF16), "wv": wv.astype(BF16),
            "gq": gq, "gkv": kv_norm.reshape(1, MLA_KV_LORA)}


def _head_mask(shape, hd, axis):
    return (lax.broadcasted_iota(jnp.int32, shape, axis) // HEAD_DIM) == hd


def _hgrn_kernel(h_ref, lb_ref, ng_ref, alev_ref, acum_ref, blk_ref, o_ref, st_ref):
    tt = h_ref.shape[0]
    width = HEADS * HEAD_DIM
    n_lev = alev_ref.shape[0]

    @pl.when(pl.program_id(1) == 0)
    def _():
        st_ref[...] = jnp.zeros_like(st_ref)

    q = h_ref[:, 0:width]
    f = h_ref[:, width:2 * width]
    v = h_ref[:, 2 * width:3 * width]
    g = h_ref[:, 3 * width:4 * width]
    lb = lb_ref[...]
    log_sig = jnp.minimum(f, 0.0) - jnp.log1p(jnp.exp(-jnp.abs(f)))
    a = jnp.log(lb)
    b = jnp.log1p(-lb) + log_sig
    log_f = jnp.maximum(a, b) + jnp.log1p(jnp.exp(-jnp.abs(a - b)))
    k = (1.0 - lb) * _sigmoid(-f)
    qf = q * _sigmoid(q)
    lf3 = _split3(log_f)

    def decay_sum(a_mat):
        return _dot(a_mat, lf3[0]) + _dot(a_mat, lf3[1]) + _dot(a_mat, lf3[2])

    t_row = lax.broadcasted_iota(jnp.int32, (tt, width), 0)
    t_i = lax.broadcasted_iota(jnp.int32, (tt, tt), 0)
    s_i = lax.broadcasted_iota(jnp.int32, (tt, tt), 1)
    hmask = [_head_mask((tt, width), hd, 1) for hd in range(HEADS)]
    attn = [jnp.zeros((tt, tt), F32) for _ in range(HEADS)]
    for lev in range(n_lev):
        x = jnp.exp(decay_sum(alev_ref[lev]))
        right = ((t_row >> lev) & 1) == 1
        qt = jnp.where(right, qf * x, 0.0)
        kt = jnp.where(right, 0.0, k * x).astype(BF16)
        same = (t_i >> (lev + 1)) == (s_i >> (lev + 1))
        for hd in range(HEADS):
            s = _dot_nt(jnp.where(hmask[hd], qt, 0.0).astype(BF16), kt)
            attn[hd] = attn[hd] + jnp.where(same, s, 0.0)
    o = jnp.zeros((tt, width), F32)
    vb = v.astype(BF16)
    for hd in range(HEADS):
        o = o + _dot(attn[hd].astype(BF16), jnp.where(hmask[hd], vb, jnp.zeros_like(vb)))
    ones_blk = blk_ref[...]
    o = o + _dot_exact_rhs(qf * k, ones_blk) * v
    bc = decay_sum(acum_ref[...])
    st = st_ref[...]
    st_m = jnp.where(ones_blk > 0, st, 0.0).astype(BF16)
    o = o + _dot_nt((qf * jnp.exp(bc)).astype(BF16), st_m)
    b_last = bc[tt - 1:tt, :]
    k_out = (k * jnp.exp(b_last - bc)).astype(BF16)
    st_ref[...] = st * jnp.exp(b_last) + _dot_tn(vb, k_out)
    ms = _dot_exact_rhs(o * o, ones_blk) * (1.0 / HEAD_DIM)
    o = o * lax.rsqrt(ms + EPS) * ng_ref[...]
    o_ref[...] = o * (g * _sigmoid(g))


def _hgrn_level_matrices(tt):
    n_lev = int(math.log2(tt))
    t = np.arange(tt)[:, None]
    j = np.arange(tt)[None, :]
    mats = []
    for lev in range(n_lev):
        half = 1 << lev
        mid = (t >> (lev + 1) << (lev + 1)) + half
        right = ((t >> lev) & 1) == 1
        m = np.where(right, (j >= mid) & (j <= t), (j > t) & (j < mid))
        mats.append(m.astype(np.float32))
    acum = (j <= t).astype(np.float32)
    return jnp.asarray(np.stack(mats), BF16), jnp.asarray(acum, BF16)


def _head_block_ones():
    i = np.arange(HEADS * HEAD_DIM)
    return jnp.asarray((i[:, None] // HEAD_DIM == i[None, :] // HEAD_DIM).astype(np.float32), BF16)


def hgrn2_mixer(h_hg, lb, norm_g, bsz, seq):
    t = bsz * seq
    tt = TT_HG
    width = HEADS * HEAD_DIM
    alev, acum = _hgrn_level_matrices(tt)
    nt = seq // tt
    return pl.pallas_call(
        _hgrn_kernel,
        grid=(bsz, nt),
        in_specs=[pl.BlockSpec((tt, 4 * width), lambda b, i: (b * nt + i, 0)),
                  _const_spec((1, width)), _const_spec((1, width)),
                  _const_spec(alev.shape), _const_spec(acum.shape), _const_spec((width, width))],
        out_specs=pl.BlockSpec((tt, width), lambda b, i: (b * nt + i, 0)),
        out_shape=jax.ShapeDtypeStruct((t, width), F32),
        scratch_shapes=[pltpu.VMEM((width, width), F32)],
        compiler_params=_cparams("parallel", "arbitrary"),
        name="hgrn2_mixer",
    )(h_hg, lb.reshape(1, width), norm_g.reshape(1, width), alev, acum, _head_block_ones())


def _ret_kernel(h_ref, cos_ref, sin_ref, dmat_ref, qdec_ref, kdec_ref, cdec_ref, blk_ref, g_ref, b_ref,
                o_ref, st_ref):
    tt = h_ref.shape[0]
    width = HEADS * HEAD_DIM

    @pl.when(pl.program_id(1) == 0)
    def _():
        st_ref[...] = jnp.zeros_like(st_ref)

    cos = cos_ref[...]
    sin = sin_ref[...]
    q = (h_ref[:, 0:width] * cos + h_ref[:, 4 * width:5 * width] * sin) * (HEAD_DIM ** -0.5)
    k = h_ref[:, width:2 * width] * cos + h_ref[:, 5 * width:6 * width] * sin
    v = h_ref[:, 2 * width:3 * width]
    g = h_ref[:, 3 * width:4 * width]
    kb = k.astype(BF16)
    vb = v.astype(BF16)
    ones_blk = blk_ref[...]
    o = jnp.zeros((tt, width), F32)
    for hd in range(HEADS):
        hm = _head_mask((tt, width), hd, 1)
        s = _dot_nt(jnp.where(hm, q, 0.0).astype(BF16), kb) * dmat_ref[hd]
        o = o + _dot(s.astype(BF16), jnp.where(hm, vb, jnp.zeros_like(vb)))
    st = st_ref[...]
    st_m = jnp.where(ones_blk > 0, st, 0.0).astype(BF16)
    o = o + _dot((q * qdec_ref[...]).astype(BF16), st_m)
    st_ref[...] = st * cdec_ref[...] + _dot_tn((k * kdec_ref[...]).astype(BF16), vb)
    mu = _dot_exact_rhs(o, ones_blk) * (1.0 / HEAD_DIM)
    oc = o - mu
    var = _dot_exact_rhs(oc * oc, ones_blk) * (1.0 / HEAD_DIM)
    o = oc * lax.rsqrt(var + EPS) * g_ref[...] + b_ref[...]
    o_ref[...] = (g * _sigmoid(g)) * o


def _retention_tables(tt):
    log_gamma = jnp.log(1.0 - 2.0 ** (-5.0 - jnp.arange(HEADS, dtype=F32)))
    t_idx = jnp.arange(tt, dtype=F32)
    rel = t_idx[:, None] - t_idx[None, :]
    dmat = jnp.where(rel >= 0, jnp.exp(jnp.maximum(rel, 0.0)[None] * log_gamma[:, None, None]), 0.0)
    lanes = lambda a: jnp.repeat(a, HEAD_DIM, axis=-1)
    qdec = lanes(jnp.exp((t_idx + 1.0)[:, None] * log_gamma[None, :]))
    kdec = lanes(jnp.exp((tt - 1 - t_idx)[:, None] * log_gamma[None, :]))
    cdec = lanes(jnp.exp(tt * log_gamma)[None, :])
    return dmat, qdec, kdec, cdec


def retention_mixer(h_ret, cos_r, sin_r, gn_g, gn_b, bsz, seq):
    t = bsz * seq
    tt = TT_RET
    width = HEADS * HEAD_DIM
    nt = seq // tt
    dmat, qdec, kdec, cdec = _retention_tables(tt)
    row = lambda w: pl.BlockSpec((tt, w), lambda b, i: (b * nt + i, 0))
    return pl.pallas_call(
        _ret_kernel,
        grid=(bsz, nt),
        in_specs=[row(W_RET_IN), row(width), row(width), _const_spec((HEADS, tt, tt)), _const_spec((tt, width)),
                  _const_spec((tt, width)), _const_spec((1, width)), _const_spec((width, width)),
                  _const_spec((1, width)), _const_spec((1, width))],
        out_specs=row(width),
        out_shape=jax.ShapeDtypeStruct((t, width), F32),
        scratch_shapes=[pltpu.VMEM((width, width), F32)],
        compiler_params=_cparams("parallel", "arbitrary"),
        name="retention_mixer",
    )(h_ret, cos_r, sin_r, dmat, qdec, kdec, cdec, _head_block_ones(), gn_g.reshape(1, width),
      gn_b.reshape(1, width))


def _store_token_tiles(ref, val):
    tm = val.shape[0]
    for j in range(D_MODEL // LANES):
        ref[pl.ds(j, tm, stride=SUBLANES), :] = val[:, j * LANES:(j + 1) * LANES]


def _load_token_tiles(ref, tm):
    return jnp.concatenate([ref[pl.ds(j, tm, stride=SUBLANES), :] for j in range(D_MODEL // LANES)], axis=-1)


def _mix_kernel(x_ref, ya_ref, yb_ref, yc_ref, yd_ref, wg_ref, wb_ref, wo_ref, g_ref, b_ref, o_ref):
    x = x_ref[...]
    xb = x.astype(BF16)
    acc = jnp.zeros(x.shape, F32)
    for n, y_ref in enumerate((ya_ref, yb_ref, yc_ref, yd_ref)):
        gate = _sigmoid(_dot(xb, wg_ref[:, n * D_MODEL:(n + 1) * D_MODEL]))
        acc = acc + gate * _dot(y_ref[...].astype(BF16), wb_ref[n])
    z = ALPHA * x + _dot(acc.astype(BF16), wo_ref[...])
    o_ref[...] = _layer_norm_rows(z, g_ref[...], b_ref[...])


def mix_branches(x, ys, w_gate, w_branch, w_o, ln_g, ln_b):
    t = x.shape[0]
    tm = TM_PROJ
    row = lambda w: pl.BlockSpec((tm, w), lambda i: (i, 0))
    return pl.pallas_call(
        _mix_kernel,
        grid=(t // tm,),
        in_specs=[row(D_MODEL)] + [row(MIX_WIDTH)] * 4 +
                 [_const_spec((D_MODEL, N_BRANCH * D_MODEL)), _const_spec((N_BRANCH, MIX_WIDTH, D_MODEL)),
                  _const_spec((D_MODEL, D_MODEL)), _const_spec((1, D_MODEL)), _const_spec((1, D_MODEL))],
        out_specs=row(D_MODEL),
        out_shape=jax.ShapeDtypeStruct((t, D_MODEL), F32),
        compiler_params=_cparams("parallel"),
        name="mix_branches",
    )(x, *ys, w_gate, w_branch, w_o, ln_g.reshape(1, D_MODEL), ln_b.reshape(1, D_MODEL))


def _ffn_kernel(x_ref, wg_ref, wu_ref, wd_ref, o_ref, xb_ref, acc_ref):
    c = pl.program_id(1)

    @pl.when(c == 0)
    def _():
        xb_ref[...] = x_ref[...].astype(BF16)
        acc_ref[...] = jnp.zeros_like(acc_ref)

    xb = xb_ref[...]
    gate = _dot(xb, wg_ref[...])
    act = gate * _sigmoid(gate) * _dot(xb, wu_ref[...])
    acc_ref[...] += _dot(act.astype(BF16), wd_ref[...])

    @pl.when(c == pl.num_programs(1) - 1)
    def _():
        o_ref[...] = acc_ref[...]


def dense_ffn(x, w_gate, w_up, w_down):
    t = x.shape[0]
    tm, tf = TM_FFN, TF_FFN
    return pl.pallas_call(
        _ffn_kernel,
        grid=(t // tm, D_FF // tf),
        in_specs=[pl.BlockSpec((tm, D_MODEL), lambda i, c: (i, 0)),
                  pl.BlockSpec((D_MODEL, tf), lambda i, c: (0, c)),
                  pl.BlockSpec((D_MODEL, tf), lambda i, c: (0, c)),
                  pl.BlockSpec((tf, D_MODEL), lambda i, c: (c, 0))],
        out_specs=pl.BlockSpec((tm, D_MODEL), lambda i, c: (i, 0)),
        out_shape=jax.ShapeDtypeStruct((t, D_MODEL), F32),
        scratch_shapes=[pltpu.VMEM((tm, D_MODEL), BF16), pltpu.VMEM((tm, D_MODEL), F32)],
        compiler_params=_cparams("parallel", "arbitrary"),
        name="dense_ffn",
    )(x, w_gate, w_up, w_down)


def _router_kernel(x_ref, wh_ref, wl_ref, upper_ref, meta_ref, cnt_ref, carry_ref):
    tm = x_ref.shape[0]

    @pl.when(pl.program_id(0) == 0)
    def _():
        carry_ref[...] = jnp.zeros_like(carry_ref)

    xh, xl = _split2(x_ref[...])
    wh = wh_ref[...]
    logits = _dot_nt(wh, xh) + _dot_nt(wh, xl) + _dot_nt(wl_ref[...], xh)
    e_idx = lax.broadcasted_iota(jnp.int32, logits.shape, 0)
    m1 = jnp.max(logits, axis=0, keepdims=True)
    i1 = jnp.min(jnp.where(logits == m1, e_idx, N_EXPERTS), axis=0, keepdims=True)
    rest = jnp.where(e_idx == i1, -jnp.inf, logits)
    m2 = jnp.max(rest, axis=0, keepdims=True)
    i2 = jnp.min(jnp.where(rest == m2, e_idx, N_EXPERTS), axis=0, keepdims=True)
    ex = jnp.exp(m2 - m1)
    w1 = 1.0 / (1.0 + ex)
    w2 = ex / (1.0 + ex)
    sel1 = e_idx == i1
    sel2 = e_idx == i2
    onehot = jnp.where(sel1 | sel2, 1.0, 0.0)
    carry = carry_ref[...]
    ranks = _dot(onehot.astype(BF16), upper_ref[...]) + carry[:, 0:1]
    r1 = jnp.sum(jnp.where(sel1, ranks, 0.0), axis=0, keepdims=True)
    r2 = jnp.sum(jnp.where(sel2, ranks, 0.0), axis=0, keepdims=True)
    row = lax.broadcasted_iota(jnp.int32, (SUBLANES, tm), 0)
    vals = (i1.astype(F32), i2.astype(F32), w1, w2, r1, r2)
    meta = jnp.zeros((SUBLANES, tm), F32)
    for n, val in enumerate(vals):
        meta = jnp.where(row == n, val, meta)
    meta_ref[...] = meta
    carry = carry + jnp.sum(onehot, axis=1, keepdims=True)
    carry_ref[...] = carry
    cnt_ref[...] = carry


def moe_router(x, w_router):
    t = x.shape[0]
    tm = TM_ROUTE
    wt = w_router.T
    wh = wt.astype(BF16)
    wl = (wt - wh.astype(F32)).astype(BF16)
    upper = jnp.asarray(np.triu(np.ones((tm, tm), np.float32), 1), BF16)
    return pl.pallas_call(
        _router_kernel,
        grid=(t // tm,),
        in_specs=[pl.BlockSpec((tm, D_MODEL), lambda i: (i, 0)), _const_spec((N_EXPERTS, D_MODEL)),
                  _const_spec((N_EXPERTS, D_MODEL)), _const_spec((tm, tm))],
        out_specs=[pl.BlockSpec((SUBLANES, tm), lambda i: (0, i)), pl.BlockSpec((N_EXPERTS, LANES), lambda i: (0, 0))],
        out_shape=[jax.ShapeDtypeStruct((SUBLANES, t), F32), jax.ShapeDtypeStruct((N_EXPERTS, LANES), F32)],
        scratch_shapes=[pltpu.VMEM((N_EXPERTS, LANES), F32)],
        compiler_params=_cparams("arbitrary"),
        name="moe_router",
    )(x, wh, wl, upper)


def _token_rows(ref, tok):
    return ref.at[pl.ds(pl.multiple_of(tok * SUBLANES, SUBLANES), SUBLANES), :]


def _start_then_wait(n, copies):
    def start(r, carry):
        for cp in copies(r):
            cp.start()
        return carry

    def wait(r, carry):
        for cp in copies(r):
            cp.wait()
        return carry

    lax.fori_loop(0, n, start, 0)
    return lambda: lax.fori_loop(0, n, wait, 0)


def _dispatch_kernel(dest_ref, x_ref, init_ref, xs_ref, buf_ref, sem):
    del init_ref
    td = x_ref.shape[0]
    _store_token_tiles(buf_ref, x_ref[...])

    def copies(r):
        src = _token_rows(buf_ref, r)
        return (pltpu.make_async_copy(src, _token_rows(xs_ref, dest_ref[0, 0, 2 * r]), sem),
                pltpu.make_async_copy(src, _token_rows(xs_ref, dest_ref[0, 0, 2 * r + 1]), sem))

    _start_then_wait(td, copies)()


def moe_dispatch(x, dest3, p_len):
    t = x.shape[0]
    td = dest3.shape[2] // 2
    init = jnp.zeros((p_len * SUBLANES, LANES), F32)
    return pl.pallas_call(
        _dispatch_kernel,
        grid=(t // td,),
        in_specs=[pl.BlockSpec((1, 1, 2 * td), lambda i: (i, 0, 0), memory_space=pltpu.SMEM),
                  pl.BlockSpec((td, D_MODEL), lambda i: (i, 0)), pl.BlockSpec(memory_space=pl.ANY)],
        out_specs=pl.BlockSpec(memory_space=pl.ANY),
        out_shape=jax.ShapeDtypeStruct((p_len * SUBLANES, LANES), F32),
        scratch_shapes=[pltpu.VMEM((td * SUBLANES, LANES), F32), pltpu.SemaphoreType.DMA(())],
        input_output_aliases={2: 0},
        compiler_params=pltpu.CompilerParams(dimension_semantics=("arbitrary",), has_side_effects=True),
        name="moe_dispatch",
    )(dest3, x, init)


def _expert_kernel(te_ref, nv_ref, xs_ref, wg_ref, wu_ref, wd_ref, ys_ref, xb_ref, acc_ref):
    i = pl.program_id(0)
    c = pl.program_id(1)
    tm = xb_ref.shape[0]
    valid = i < nv_ref[0]

    @pl.when(valid & (c == 0))
    def _():
        xb_ref[...] = _load_token_tiles(xs_ref, tm).astype(BF16)
        acc_ref[...] = jnp.zeros_like(acc_ref)

    @pl.when(valid)
    def _():
        xb = xb_ref[...]
        gate = _dot(xb, wg_ref[...].astype(BF16))
        act = gate * _sigmoid(gate) * _dot(xb, wu_ref[...].astype(BF16))
        acc_ref[...] += _dot(act.astype(BF16), wd_ref[...].astype(BF16))

    @pl.when(c == pl.num_programs(1) - 1)
    def _():
        _store_token_tiles(ys_ref, jnp.where(valid, acc_ref[...], 0.0))


def moe_experts(xs, tile_expert, n_valid, layer, w_gate, w_up, w_down):
    tm, tf = TM_MOE, TF_FFN
    n_tiles = xs.shape[0] // (tm * SUBLANES)
    nc = D_FF // tf

    def chunk(i, c, nv):
        return jnp.where(i < nv[0], c, nc - 1)

    grid_spec = pltpu.PrefetchScalarGridSpec(
        num_scalar_prefetch=2,
        grid=(n_tiles, nc),
        in_specs=[pl.BlockSpec((tm * SUBLANES, LANES), lambda i, c, te, nv: (jnp.minimum(i, nv[0] - 1), 0)),
                  pl.BlockSpec((None, None, D_MODEL, tf), lambda i, c, te, nv: (layer, te[i], 0, chunk(i, c, nv))),
                  pl.BlockSpec((None, None, D_MODEL, tf), lambda i, c, te, nv: (layer, te[i], 0, chunk(i, c, nv))),
                  pl.BlockSpec((None, None, tf, D_MODEL), lambda i, c, te, nv: (layer, te[i], chunk(i, c, nv), 0))],
        out_specs=pl.BlockSpec((tm * SUBLANES, LANES), lambda i, c, te, nv: (i, 0)),
        scratch_shapes=[pltpu.VMEM((tm, D_MODEL), BF16), pltpu.VMEM((tm, D_MODEL), F32)],
    )
    return pl.pallas_call(
        _expert_kernel,
        grid_spec=grid_spec,
        out_shape=jax.ShapeDtypeStruct(xs.shape, F32),
        compiler_params=_cparams("arbitrary", "arbitrary"),
        name="moe_experts",
    )(tile_expert, n_valid, xs, w_gate, w_up, w_down)


def moe_ffn(x, w_router, layer, w_gate, w_up, w_down):
    t = x.shape[0]
    tm = TM_MOE
    meta, cnt = moe_router(x, w_router)
    counts = cnt[:, 0].astype(jnp.int32)
    padded = (counts + tm - 1) // tm * tm
    pad_end = jnp.cumsum(padded)
    pad_start = pad_end - padded
    experts = meta[0:2].astype(jnp.int32)
    start_of = jnp.sum(jnp.where(experts[..., None] == jnp.arange(N_EXPERTS), pad_start, 0), axis=-1)
    dest = (start_of + meta[4:6].astype(jnp.int32)).T
    dest3 = dest.reshape(t // TD_DISPATCH, 1, 2 * TD_DISPATCH)
    n_tiles = (2 * t) // tm + N_EXPERTS
    n_valid = (pad_end[-1] // tm).astype(jnp.int32).reshape(1)
    tile_start = jnp.arange(n_tiles, dtype=jnp.int32) * tm
    tile_expert = jnp.minimum(jnp.sum(pad_end[None, :] <= tile_start[:, None], axis=1), N_EXPERTS - 1)
    last_expert = jnp.sum(jnp.where(jnp.arange(n_tiles) == n_valid[0] - 1, tile_expert, 0))
    tile_expert = jnp.where(jnp.arange(n_tiles) < n_valid[0], tile_expert, last_expert).astype(jnp.int32)
    xs = moe_dispatch(x, dest3, n_tiles * tm)
    ys = moe_experts(xs, tile_expert, n_valid, layer, w_gate, w_up, w_down)
    return ys, dest3, meta[2:4].T


def _post_kernel(x_ref, p_ref, wpg_ref, wpp_ref, g_ref, b_ref, f_ref, o_ref):
    x = x_ref[...]
    ple = _sigmoid(_dot(x.astype(BF16), wpg_ref[...])) * _dot(p_ref[...].astype(BF16), wpp_ref[...])
    o_ref[...] = _layer_norm_rows(ALPHA * x + f_ref[...] + ple, g_ref[...], b_ref[...])


def _post_moe_kernel(dest_ref, x_ref, p_ref, wpg_ref, wpp_ref, g_ref, b_ref, w_ref, ys_ref, o_ref,
                     buf0_ref, buf1_ref, sem):
    x = x_ref[...]
    tm = x.shape[0]

    def copies(r):
        return (pltpu.make_async_copy(_token_rows(ys_ref, dest_ref[0, 0, 2 * r]), _token_rows(buf0_ref, r), sem),
                pltpu.make_async_copy(_token_rows(ys_ref, dest_ref[0, 0, 2 * r + 1]), _token_rows(buf1_ref, r), sem))

    wait_all = _start_then_wait(tm, copies)
    ple = _sigmoid(_dot(x.astype(BF16), wpg_ref[...])) * _dot(p_ref[...].astype(BF16), wpp_ref[...])
    wait_all()
    w = w_ref[...]
    f = w[:, 0:1] * _load_token_tiles(buf0_ref, tm) + w[:, 1:2] * _load_token_tiles(buf1_ref, tm)
    o_ref[...] = _layer_norm_rows(ALPHA * x + f + ple, g_ref[...], b_ref[...])


def post_layer(x, p, layer, w_ple_gate, w_ple_proj, ln_g, ln_b, f=None, moe=None):
    t = x.shape[0]
    tm = TM_PROJ if moe is None else TD_DISPATCH
    row = lambda w: pl.BlockSpec((tm, w), lambda i: (i, 0))
    in_specs = [row(D_MODEL), pl.BlockSpec((None, tm, PLE_DIM), lambda i: (layer, i, 0)),
                _const_spec((D_MODEL, D_MODEL)), _const_spec((PLE_DIM, D_MODEL)),
                _const_spec((1, D_MODEL)), _const_spec((1, D_MODEL))]
    args = [x, p, w_ple_gate, w_ple_proj, ln_g.reshape(1, D_MODEL), ln_b.reshape(1, D_MODEL)]
    scratch = []
    if moe is None:
        body = _post_kernel
        in_specs += [row(D_MODEL)]
        args += [f]
    else:
        body = _post_moe_kernel
        ys, dest3, w2 = moe
        in_specs = [pl.BlockSpec((1, 1, 2 * tm), lambda i: (i, 0, 0), memory_space=pltpu.SMEM)] + in_specs
        in_specs += [row(2), pl.BlockSpec(memory_space=pl.ANY)]
        args = [dest3] + args + [w2, ys]
        scratch = [pltpu.VMEM((tm * SUBLANES, LANES), F32), pltpu.VMEM((tm * SUBLANES, LANES), F32),
                   pltpu.SemaphoreType.DMA(())]
    return pl.pallas_call(
        body,
        grid=(t // tm,),
        in_specs=in_specs,
        out_specs=row(D_MODEL),
        out_shape=jax.ShapeDtypeStruct((t, D_MODEL), F32),
        scratch_shapes=scratch,
        compiler_params=_cparams("parallel" if moe is None else "arbitrary"),
        name="post_layer",
    )(*args)


def split_in_weights(w):
    sizes = (MIX_WIDTH, MLA_Q_LORA, MLA_KV_LORA, MLA_ROPE) + (HEADS * HEAD_DIM,) * 8 + (N_BRANCH * D_MODEL,)
    parts, start = [], 0
    for n in sizes:
        parts.append(w[:, start:start + n])
        start += n
    u_s5, c_q, c_kv, k_rope, hq, hf, hi, hg, rq, rk, rv, rg, gate = parts
    k_rot = jnp.concatenate([-k_rope[:, MLA_ROPE // 2:], k_rope[:, :MLA_ROPE // 2]], axis=1)
    w_mla = jnp.zeros((D_MODEL, W_MLA_IN), F32)
    w_mla = w_mla.at[:, 0:MLA_Q_LORA].set(c_q).at[:, 256:384].set(c_kv)
    w_mla = w_mla.at[:, 384 + MLA_NOPE:384 + MLA_NOPE + MLA_ROPE].set(k_rope)
    w_mla = w_mla.at[:, 512 + MLA_NOPE:512 + MLA_NOPE + MLA_ROPE].set(k_rot)
    w_hg = jnp.concatenate([hq, hf, hi, hg], axis=1)
    w_ret = jnp.concatenate([rq, rk, rv, rg, _rot_half_cols(rq, HEAD_DIM), _rot_half_cols(rk, HEAD_DIM)], axis=1)
    return [a.astype(BF16) for a in (w_mla, u_s5, w_hg, w_ret)], gate.astype(BF16)


def kernel(x, p, positions, w_in, s5_a_re, s5_a_im, s5_log_dt, s5_b_re, s5_b_im, s5_c_re, s5_c_im, s5_d, s5_w_glu, mla_q_norm, mla_kv_norm, mla_w_uq, mla_w_ukv, hg_lb_raw, hg_norm, ret_gn_g, ret_gn_b, w_branch, w_o, ln1_g, ln1_b, ff_w_gate, ff_w_up, ff_w_down, moe_router, moe_w_gate, moe_w_up, moe_w_down, ple_w_gate, ple_w_proj, ln2_g, ln2_b):
    bsz, seq, _ = x.shape
    t = bsz * seq
    depth = w_in.shape[0]
    x = x.reshape(t, D_MODEL)
    p = p.reshape(depth, t, PLE_DIM)
    cos_m, sin_m, cos_r, sin_r = rope_tables(positions)
    lb_all = jnp.cumsum(jax.nn.softmax(hg_lb_raw.astype(F32), axis=0), axis=0)
    lb_all = lb_all - lb_all[0]
    for i in range(depth):
        in_ws, w_gate = split_in_weights(w_in[i])
        h_mla, h_s5, h_hg, h_ret = in_projection(x, in_ws)
        y_a = s5_mixer(h_s5, bsz, seq, s5_params(s5_a_re[i], s5_a_im[i], s5_log_dt[i], s5_b_re[i], s5_b_im[i],
                                                s5_c_re[i], s5_c_im[i], s5_d[i], s5_w_glu[i]))
        y_b = mla_mixer(h_mla, cos_m, sin_m, bsz, seq,
                        mla_params(mla_q_norm[i], mla_kv_norm[i], mla_w_uq[i], mla_w_ukv[i]))
        y_c = hgrn2_mixer(h_hg, lb_all[i], hg_norm[i], bsz, seq)
        y_d = retention_mixer(h_ret, cos_r, sin_r, ret_gn_g[i], ret_gn_b[i], bsz, seq)
        x = mix_branches(x, (y_a, y_b, y_c, y_d), w_gate, w_branch[i].astype(BF16), w_o[i].astype(BF16),
                         ln1_g[i], ln1_b[i])
        ple_args = (p, i, ple_w_gate[i].astype(BF16), ple_w_proj[i].astype(BF16), ln2_g[i], ln2_b[i])
        if i % 2 == 1:
            x = post_layer(x, *ple_args, moe=moe_ffn(x, moe_router[i // 2], i // 2, moe_w_gate, moe_w_up,
                                                     moe_w_down))
        else:
            f = dense_ffn(x, ff_w_gate[i // 2].astype(BF16), ff_w_up[i // 2].astype(BF16),
                          ff_w_down[i // 2].astype(BF16))
            x = post_layer(x, *ple_args, f=f)
    return x.reshape(bsz, seq, D_MODEL)
```

```python
import functools
import math

import numpy as np
import jax
import jax.numpy as jnp
from jax import lax
from jax.experimental import pallas as pl
from jax.experimental.pallas import tpu as pltpu

F32 = jnp.float32
BF16 = jnp.bfloat16

D_MODEL = 1024
DEPTH = 4
PLE_DIM = 256
MIX_WIDTH = 256
N_BRANCH = 4
S5_GROUP = 16
S5_GROUPS = 16
S5_STATE = 64
MLA_HEADS = 4
MLA_NOPE = 64
MLA_ROPE = 32
MLA_V = 64
MLA_Q_LORA = 192
MLA_KV_LORA = 128
HEADS = 4
HEAD_DIM = 64
D_FF = 3584
N_EXPERTS = 8
ROPE_BASE = 10000.0
EPS = 1e-5
NEG_INF = -1e30
ALPHA = (2 * DEPTH) ** 0.25

LANES = 128
SUBLANES = 8
VMEM_LIMIT = 56 * 1024 * 1024

TM_PROJ = 512
TS_S5 = 64
TQ_ATT = 256
TT_HG = 128
TT_RET = 256
TM_FFN = 1024
TF_FFN = 512
TM_MOE = 1024
TM_ROUTE = 512
TD_DISPATCH = 512


def _cparams(*sem):
    return pltpu.CompilerParams(dimension_semantics=sem, vmem_limit_bytes=VMEM_LIMIT)


def _const_spec(shape):
    nd = len(shape)
    return pl.BlockSpec(shape, lambda *_: (0,) * nd, pipeline_mode=pl.Buffered(1))


def _dot(a, b):
    return jnp.dot(a, b, preferred_element_type=F32)


def _dot_nt(a, b):
    return lax.dot_general(a, b, (((1,), (1,)), ((), ())), preferred_element_type=F32)


def _dot_tn(a, b):
    return lax.dot_general(a, b, (((0,), (0,)), ((), ())), preferred_element_type=F32)


def _split2(x):
    hi = x.astype(BF16)
    lo = (x - hi.astype(F32)).astype(BF16)
    return hi, lo


def _split3(x):
    h1 = x.astype(BF16)
    r1 = x - h1.astype(F32)
    h2 = r1.astype(BF16)
    h3 = (r1 - h2.astype(F32)).astype(BF16)
    return h1, h2, h3


def _dot_exact_rhs(x, m):
    hi, lo = _split2(x)
    return _dot(hi, m) + _dot(lo, m)


def _dot_exact_lhs(m, x):
    hi, lo = _split2(x)
    return _dot(m, hi) + _dot(m, lo)


def _sigmoid(x):
    return 1.0 / (1.0 + jnp.exp(-x))


def _layer_norm_rows(z, g, b):
    mu = jnp.mean(z, axis=-1, keepdims=True)
    zc = z - mu
    var = jnp.mean(zc * zc, axis=-1, keepdims=True)
    return zc * lax.rsqrt(var + EPS) * g + b


def _rope_kernel(pos_ref, fm_ref, fr_ref, cm_ref, sm_ref, cr_ref, sr_ref):
    pos = pos_ref[...]
    ang_m = pos * fm_ref[...]
    lane = lax.broadcasted_iota(jnp.int32, ang_m.shape, 1)
    in_pe = (lane >= MLA_NOPE) & (lane < MLA_NOPE + MLA_ROPE)
    cm_ref[...] = jnp.where(lane < MLA_NOPE, 1.0, jnp.where(in_pe, jnp.cos(ang_m), 0.0))
    sm_ref[...] = jnp.where(in_pe, jnp.sin(ang_m), 0.0)
    ang_r = pos * fr_ref[...]
    cr_ref[...] = jnp.cos(ang_r)
    sr_ref[...] = jnp.sin(ang_r)


def rope_tables(positions):
    t = positions.size
    tm = TM_PROJ
    pos = positions.reshape(t, 1).astype(F32)
    inv_m = ROPE_BASE ** (-jnp.arange(MLA_ROPE // 2, dtype=F32) / (MLA_ROPE // 2))
    inv_r = ROPE_BASE ** (-jnp.arange(HEAD_DIM // 2, dtype=F32) / (HEAD_DIM // 2))
    fm = jnp.zeros((1, LANES), F32).at[0, MLA_NOPE:MLA_NOPE + MLA_ROPE].set(jnp.concatenate([inv_m, inv_m]))
    fr = jnp.tile(jnp.concatenate([inv_r, inv_r]), HEADS).reshape(1, HEADS * HEAD_DIM)
    row = lambda w: pl.BlockSpec((tm, w), lambda i: (i, 0))
    return pl.pallas_call(
        _rope_kernel,
        grid=(t // tm,),
        in_specs=[row(1), _const_spec((1, LANES)), _const_spec((1, 256))],
        out_specs=[row(LANES), row(LANES), row(256), row(256)],
        out_shape=[jax.ShapeDtypeStruct((t, LANES), F32), jax.ShapeDtypeStruct((t, LANES), F32),
                   jax.ShapeDtypeStruct((t, 256), F32), jax.ShapeDtypeStruct((t, 256), F32)],
        compiler_params=_cparams("parallel"),
        name="rope_tables",
    )(pos, fm, fr)


W_MLA_IN = 640
W_RET_IN = 1536
_N_CHUNK = 256


def _inproj_kernel(x_ref, *refs):
    n = len(refs) // 2
    xb = x_ref[...].astype(BF16)
    for w_ref, o_ref in zip(refs[:n], refs[n:]):
        width = w_ref.shape[1]
        for c0 in range(0, width, _N_CHUNK):
            c1 = min(c0 + _N_CHUNK, width)
            o_ref[:, c0:c1] = _dot(xb, w_ref[:, c0:c1])


def in_projection(x, weights):
    t = x.shape[0]
    tm = TM_PROJ
    return pl.pallas_call(
        _inproj_kernel,
        grid=(t // tm,),
        in_specs=[pl.BlockSpec((tm, D_MODEL), lambda i: (i, 0))] + [_const_spec(w.shape) for w in weights],
        out_specs=[pl.BlockSpec((tm, w.shape[1]), lambda i: (i, 0)) for w in weights],
        out_shape=[jax.ShapeDtypeStruct((t, w.shape[1]), F32) for w in weights],
        compiler_params=_cparams("parallel"),
        name="in_projection",
    )(x, *weights)


S5_HALF = S5_GROUPS * S5_STATE


def _s5_kernel(u_ref, perm_ref, permt_ref, bmat_ref, ar_ref, ai_ref, cmat_ref, d_ref, wglu_ref, y_ref,
               bu_ref, hs_ref, h_ref):
    nb, ts, width = u_ref.shape
    n_half = S5_HALF // LANES

    @pl.when(pl.program_id(0) == 0)
    def _():
        h_ref[...] = jnp.zeros_like(h_ref)

    u = u_ref[...].reshape(nb * ts, width)
    u_tm = _dot(perm_ref[...], u.astype(BF16)).astype(BF16)
    bu_ref[...] = _dot(u_tm, bmat_ref[...])
    ar = [jnp.broadcast_to(ar_ref[:, c * LANES:(c + 1) * LANES], (nb, LANES)) for c in range(n_half)]
    ai = [jnp.broadcast_to(ai_ref[:, c * LANES:(c + 1) * LANES], (nb, LANES)) for c in range(n_half)]
    h_re = [h_ref[:, c * LANES:(c + 1) * LANES] for c in range(n_half)]
    h_im = [h_ref[:, S5_HALF + c * LANES:S5_HALF + (c + 1) * LANES] for c in range(n_half)]
    for n in range(ts):
        rows = slice(n * nb, (n + 1) * nb)
        for c in range(n_half):
            re_cols = slice(c * LANES, (c + 1) * LANES)
            im_cols = slice(S5_HALF + c * LANES, S5_HALF + (c + 1) * LANES)
            new_re = ar[c] * h_re[c] - ai[c] * h_im[c] + bu_ref[rows, re_cols]
            new_im = ar[c] * h_im[c] + ai[c] * h_re[c] + bu_ref[rows, im_cols]
            h_re[c], h_im[c] = new_re, new_im
            hs_ref[rows, re_cols] = new_re
            hs_ref[rows, im_cols] = new_im
    for c in range(n_half):
        h_ref[:, c * LANES:(c + 1) * LANES] = h_re[c]
        h_ref[:, S5_HALF + c * LANES:S5_HALF + (c + 1) * LANES] = h_im[c]
    y_tm = _dot(hs_ref[...].astype(BF16), cmat_ref[...])
    y = _dot_exact_lhs(permt_ref[...], y_tm) + d_ref[...] * u
    y = jax.nn.gelu(y)
    y = y * _sigmoid(_dot(y.astype(BF16), wglu_ref[...]))
    y_ref[...] = y.reshape(nb, ts, width)


def _step_major_permutation(nb, ts):
    p = np.zeros((nb * ts, nb * ts), np.float32)
    b, n = np.meshgrid(np.arange(nb), np.arange(ts), indexing="ij")
    p[(n * nb + b).ravel(), (b * ts + n).ravel()] = 1.0
    return jnp.asarray(p, BF16), jnp.asarray(p.T, BF16)


def s5_mixer(u, bsz, seq, prm):
    ts = TS_S5
    u3 = u.reshape(bsz, seq, MIX_WIDTH)
    blk = pl.BlockSpec((bsz, ts, MIX_WIDTH), lambda i: (0, i, 0))
    perm, perm_t = _step_major_permutation(bsz, ts)
    y = pl.pallas_call(
        _s5_kernel,
        grid=(seq // ts,),
        in_specs=[blk, _const_spec(perm.shape), _const_spec(perm.shape),
                  _const_spec((MIX_WIDTH, 2 * S5_HALF)), _const_spec((1, S5_HALF)), _const_spec((1, S5_HALF)),
                  _const_spec((2 * S5_HALF, MIX_WIDTH)), _const_spec((1, MIX_WIDTH)),
                  _const_spec((MIX_WIDTH, MIX_WIDTH))],
        out_specs=blk,
        out_shape=jax.ShapeDtypeStruct((bsz, seq, MIX_WIDTH), F32),
        scratch_shapes=[pltpu.VMEM((bsz * ts, 2 * S5_HALF), F32), pltpu.VMEM((bsz * ts, 2 * S5_HALF), F32),
                        pltpu.VMEM((bsz, 2 * S5_HALF), F32)],
        compiler_params=_cparams("arbitrary"),
        name="s5_mixer",
    )(u3, perm, perm_t, prm["bmat"], prm["a_re"], prm["a_im"], prm["cmat"], prm["d"], prm["w_glu"])
    return y.reshape(bsz * seq, MIX_WIDTH)


def s5_params(a_re, a_im, log_dt, b_re, b_im, c_re, c_im, d_skip, w_glu):
    dt = jnp.exp(log_dt)[:, None]
    mag = jnp.exp(a_re * dt)
    abar_re, abar_im = mag * jnp.cos(a_im * dt), mag * jnp.sin(a_im * dt)
    den = a_re * a_re + a_im * a_im
    num_re, num_im = abar_re - 1.0, abar_im
    coef_re = (num_re * a_re + num_im * a_im) / den
    coef_im = (num_im * a_re - num_re * a_im) / den
    bbar_re = coef_re[..., None] * b_re - coef_im[..., None] * b_im
    bbar_im = coef_re[..., None] * b_im + coef_im[..., None] * b_re
    eye = jnp.eye(S5_GROUPS, dtype=F32)
    b_bd_re = jnp.einsum("gpc,gh->gchp", bbar_re, eye).reshape(MIX_WIDTH, S5_HALF)
    b_bd_im = jnp.einsum("gpc,gh->gchp", bbar_im, eye).reshape(MIX_WIDTH, S5_HALF)
    c_bd_re = jnp.einsum("gcp,gh->gphc", c_re, eye).reshape(S5_HALF, MIX_WIDTH)
    c_bd_im = jnp.einsum("gcp,gh->gphc", c_im, eye).reshape(S5_HALF, MIX_WIDTH)
    return {
        "bmat": jnp.concatenate([b_bd_re, b_bd_im], axis=1).astype(BF16),
        "cmat": jnp.concatenate([c_bd_re, -c_bd_im], axis=0).astype(BF16),
        "a_re": abar_re.reshape(1, S5_HALF), "a_im": abar_im.reshape(1, S5_HALF),
        "d": d_skip.reshape(1, MIX_WIDTH), "w_glu": w_glu.astype(BF16),
    }


MLA_QK_PAD = 128


def _mla_prep_kernel(h_ref, cos_ref, sin_ref, gq_ref, gkv_ref, wq_ref, wqr_ref, wk_ref, wv_ref,
                     q_ref, k_ref, v_ref):
    h = h_ref[...]
    cos = cos_ref[...]
    sin = sin_ref[...]
    c_q = h[:, 0:256]
    c_kv = h[:, 256:384]
    k_pe = h[:, 384:512] * cos + h[:, 512:640] * sin
    qn = c_q * lax.rsqrt(jnp.sum(c_q * c_q, axis=-1, keepdims=True) / MLA_Q_LORA + EPS) * gq_ref[...]
    kvn = c_kv * lax.rsqrt(jnp.mean(c_kv * c_kv, axis=-1, keepdims=True) + EPS) * gkv_ref[...]
    qn = qn.astype(BF16)
    kvn = kvn.astype(BF16)
    scale = (MLA_NOPE + MLA_ROPE) ** -0.5 * math.log2(math.e)
    for hd in range(MLA_HEADS):
        q = _dot(qn, wq_ref[hd]) * cos + _dot(qn, wqr_ref[hd]) * sin
        q_ref[hd] = (q * scale).astype(BF16)
        k_ref[hd] = (_dot(kvn, wk_ref[hd]) + k_pe).astype(BF16)
        v_ref[hd] = _dot(kvn, wv_ref[hd]).astype(BF16)


def _flash_kernel(q_ref, k_ref, v_ref, o_ref, m_ref, l_ref, acc_ref):
    qi = pl.program_id(1)
    tq = q_ref.shape[1]
    n_part = tq // LANES
    row = lax.broadcasted_iota(jnp.int32, (tq, tq), 0)
    col = lax.broadcasted_iota(jnp.int32, (tq, tq), 1)

    def scores(hd, rows, masked):
        s = _dot_nt(q_ref[hd], k_ref[hd, rows, :])
        if masked:
            s = jnp.where(col <= row, s, NEG_INF)
        return [s[:, c * LANES:(c + 1) * LANES] for c in range(n_part)]

    def chunk(j, masked):
        rows = pl.ds(pl.multiple_of(j * tq, tq), tq)
        parts = [scores(hd, rows, masked) for hd in range(MLA_HEADS)]
        probs, corrs = [], []
        for hd in range(MLA_HEADS):
            m_old = m_ref[hd]
            row_max = jnp.max(functools.reduce(jnp.maximum, parts[hd]), axis=-1, keepdims=True)
            m_new = jnp.maximum(m_old, row_max)
            p = [jnp.exp2(part - m_new) for part in parts[hd]]
            corr = jnp.exp2(m_old - m_new)
            l_ref[hd] = corr * l_ref[hd] + functools.reduce(jnp.add, p)
            m_ref[hd] = m_new
            probs.append(jnp.concatenate(p, axis=-1).astype(BF16))
            corrs.append(corr[:, :MLA_V])
        for hd in range(MLA_HEADS):
            acc_ref[hd] = corrs[hd] * acc_ref[hd] + _dot(probs[hd], v_ref[hd, rows, :])

    def body(j, carry):
        chunk(j, False)
        return carry

    m_ref[...] = jnp.full_like(m_ref, NEG_INF)
    l_ref[...] = jnp.zeros_like(l_ref)
    acc_ref[...] = jnp.zeros_like(acc_ref)
    lax.fori_loop(0, qi, body, 0)
    chunk(qi, True)
    o_ref[...] = jnp.concatenate(
        [acc_ref[hd] / jnp.sum(l_ref[hd], axis=-1, keepdims=True) for hd in range(MLA_HEADS)], axis=-1)


def mla_mixer(h_mla, cos_m, sin_m, bsz, seq, prm):
    t = bsz * seq
    tm = TM_PROJ
    row = lambda w: pl.BlockSpec((tm, w), lambda i: (i, 0))
    hrow = lambda w: pl.BlockSpec((MLA_HEADS, tm, w), lambda i: (0, i, 0))
    q, k, v = pl.pallas_call(
        _mla_prep_kernel,
        grid=(t // tm,),
        in_specs=[row(W_MLA_IN), row(LANES), row(LANES), _const_spec((1, 256)), _const_spec((1, MLA_KV_LORA)),
                  _const_spec((MLA_HEADS, 256, MLA_QK_PAD)), _const_spec((MLA_HEADS, 256, MLA_QK_PAD)),
                  _const_spec((MLA_HEADS, MLA_KV_LORA, MLA_QK_PAD)), _const_spec((MLA_HEADS, MLA_KV_LORA, MLA_V))],
        out_specs=[hrow(MLA_QK_PAD), hrow(MLA_QK_PAD), hrow(MLA_V)],
        out_shape=[jax.ShapeDtypeStruct((MLA_HEADS, t, MLA_QK_PAD), BF16),
                   jax.ShapeDtypeStruct((MLA_HEADS, t, MLA_QK_PAD), BF16),
                   jax.ShapeDtypeStruct((MLA_HEADS, t, MLA_V), BF16)],
        compiler_params=_cparams("parallel"),
        name="mla_prep",
    )(h_mla, cos_m, sin_m, prm["gq"], prm["gkv"], prm["wq"], prm["wqr"], prm["wk"], prm["wv"])

    tq = TQ_ATT
    nq = seq // tq
    q_spec = pl.BlockSpec((MLA_HEADS, tq, MLA_QK_PAD), lambda b, i: (0, b * nq + i, 0))
    k_spec = pl.BlockSpec((MLA_HEADS, seq, MLA_QK_PAD), lambda b, i: (0, b, 0))
    v_spec = pl.BlockSpec((MLA_HEADS, seq, MLA_V), lambda b, i: (0, b, 0))
    return pl.pallas_call(
        _flash_kernel,
        grid=(bsz, nq),
        in_specs=[q_spec, k_spec, v_spec],
        out_specs=pl.BlockSpec((tq, MLA_HEADS * MLA_V), lambda b, i: (b * nq + i, 0)),
        out_shape=jax.ShapeDtypeStruct((t, MLA_HEADS * MLA_V), F32),
        scratch_shapes=[pltpu.VMEM((MLA_HEADS, tq, LANES), F32), pltpu.VMEM((MLA_HEADS, tq, LANES), F32),
                        pltpu.VMEM((MLA_HEADS, tq, MLA_V), F32)],
        compiler_params=_cparams("parallel", "arbitrary"),
        name="mla_flash",
    )(q, k, v)


def _rot_half_cols(w, width):
    shp = w.shape
    w = w.reshape(shp[0], -1, 2, width // 2)
    return jnp.stack([-w[:, :, 1], w[:, :, 0]], axis=2).reshape(shp)


def mla_params(q_norm, kv_norm, w_uq, w_ukv):
    dq = MLA_NOPE + MLA_ROPE
    wq = jnp.zeros((MLA_HEADS, 256, MLA_QK_PAD), F32)
    wqr = jnp.zeros((MLA_HEADS, 256, MLA_QK_PAD), F32)
    w_uq_h = w_uq.reshape(MLA_Q_LORA, MLA_HEADS, dq).transpose(1, 0, 2)
    wq = wq.at[:, :MLA_Q_LORA, :dq].set(w_uq_h)
    pe = w_uq_h[:, :, MLA_NOPE:]
    pe_rot = jnp.concatenate([-pe[..., MLA_ROPE // 2:], pe[..., :MLA_ROPE // 2]], axis=-1)
    wqr = wqr.at[:, :MLA_Q_LORA, MLA_NOPE:dq].set(pe_rot)
    w_kv_h = w_ukv.reshape(MLA_KV_LORA, MLA_HEADS, MLA_NOPE + MLA_V).transpose(1, 0, 2)
    wk = jnp.zeros((MLA_HEADS, MLA_KV_LORA, MLA_QK_PAD), F32).at[:, :, :MLA_NOPE].set(w_kv_h[:, :, :MLA_NOPE])
    wv = w_kv_h[:, :, MLA_NOPE:]
    gq = jnp.zeros((1, 256), F32).at[0, :MLA_Q_LORA].set(q_norm)
    return {"wq": wq.astype(BF16), "wqr": wqr.astype(BF16), "wk": wk.astype(BF16), "wv": wv.astype(BF16),
            "gq": gq, "gkv": kv_norm.reshape(1, MLA_KV_LORA)}


def _head_mask(shape, hd, axis):
    return (lax.broadcasted_iota(jnp.int32, shape, axis) // HEAD_DIM) == hd


def _hgrn_kernel(h_ref, lb_ref, ng_ref, aall_ref, lvl_ref, hm_ref, blk_ref, o_ref, st_ref, e_ref):
    tt = h_ref.shape[0]
    width = HEADS * HEAD_DIM
    n_lev = aall_ref.shape[0] // tt - 1

    @pl.when(pl.program_id(1) == 0)
    def _():
        st_ref[...] = jnp.zeros_like(st_ref)

    q = h_ref[:, 0:width]
    f = h_ref[:, width:2 * width]
    v = h_ref[:, 2 * width:3 * width]
    g = h_ref[:, 3 * width:4 * width]
    lb = lb_ref[...]
    log_sig = jnp.minimum(f, 0.0) - jnp.log1p(jnp.exp(-jnp.abs(f)))
    a = jnp.log(lb)
    b = jnp.log1p(-lb) + log_sig
    log_f = jnp.maximum(a, b) + jnp.log1p(jnp.exp(-jnp.abs(a - b)))
    k = (1.0 - lb) * _sigmoid(-f)
    qf = q * _sigmoid(q)
    e2 = _dot(aall_ref[...], jnp.concatenate(_split2(log_f), axis=-1))
    e_ref[...] = e2[:, 0:width] + e2[:, width:2 * width]

    t_row = lax.broadcasted_iota(jnp.int32, (tt, width), 0)
    lvl = lvl_ref[...]
    attn = [jnp.zeros((tt, tt), F32) for _ in range(HEADS)]
    for lev in range(n_lev):
        x = jnp.exp(e_ref[lev * tt:(lev + 1) * tt, :])
        right = ((t_row >> lev) & 1) == 1
        qt = jnp.where(right, qf * x, 0.0).astype(BF16)
        kt = jnp.where(right, 0.0, k * x).astype(BF16)
        q_heads = jnp.concatenate([qt * hm_ref[hd] for hd in range(HEADS)], axis=0)
        s = _dot_nt(q_heads, kt)
        here = lvl == lev
        for hd in range(HEADS):
            attn[hd] = jnp.where(here, s[hd * tt:(hd + 1) * tt, :], attn[hd])
    vb = v.astype(BF16)
    o = jnp.zeros((tt, width), F32)
    for hd in range(HEADS):
        o = o + _dot(attn[hd].astype(BF16), vb * hm_ref[hd])
    ones_blk = blk_ref[...]
    o = o + _dot_exact_rhs(qf * k, ones_blk) * v
    bc = e_ref[n_lev * tt:(n_lev + 1) * tt, :]
    st = st_ref[...]
    st_m = jnp.where(ones_blk > 0, st, 0.0).astype(BF16)
    o = o + _dot_nt((qf * jnp.exp(bc)).astype(BF16), st_m)
    b_last = bc[tt - 1:tt, :]
    k_out = (k * jnp.exp(b_last - bc)).astype(BF16)
    st_ref[...] = st * jnp.exp(b_last) + _dot_tn(vb, k_out)
    ms = _dot_exact_rhs(o * o, ones_blk) * (1.0 / HEAD_DIM)
    o = o * lax.rsqrt(ms + EPS) * ng_ref[...]
    o_ref[...] = o * (g * _sigmoid(g))


def _hgrn_level_matrices(tt):
    n_lev = int(math.log2(tt))
    t = np.arange(tt)[:, None]
    j = np.arange(tt)[None, :]
    mats = []
    for lev in range(n_lev):
        half = 1 << lev
        mid = (t >> (lev + 1) << (lev + 1)) + half
        right = ((t >> lev) & 1) == 1
        mats.append(np.where(right, (j >= mid) & (j <= t), (j > t) & (j < mid)).astype(np.float32))
    mats.append((j <= t).astype(np.float32))
    level = np.full((tt, tt), -1, np.int32)
    lower = np.broadcast_to(j < t, (tt, tt))
    level[lower] = np.floor(np.log2((t ^ j)[lower])).astype(np.int32)
    return jnp.asarray(np.concatenate(mats, axis=0), BF16), jnp.asarray(level)


def _head_block_ones():
    i = np.arange(HEADS * HEAD_DIM)
    return jnp.asarray((i[:, None] // HEAD_DIM == i[None, :] // HEAD_DIM).astype(np.float32), BF16)


def _head_lane_masks(rows):
    i = np.arange(HEADS * HEAD_DIM)
    m = (i[None, :] // HEAD_DIM == np.arange(HEADS)[:, None]).astype(np.float32)
    return jnp.asarray(np.broadcast_to(m[:, None, :], (HEADS, rows, HEADS * HEAD_DIM)), BF16)


def hgrn2_mixer(h_hg, lb, norm_g, bsz, seq):
    t = bsz * seq
    tt = TT_HG
    width = HEADS * HEAD_DIM
    aall, level = _hgrn_level_matrices(tt)
    nt = seq // tt
    return pl.pallas_call(
        _hgrn_kernel,
        grid=(bsz, nt),
        in_specs=[pl.BlockSpec((tt, 4 * width), lambda b, i: (b * nt + i, 0)),
                  _const_spec((1, width)), _const_spec((1, width)),
                  _const_spec(aall.shape), _const_spec(level.shape), _const_spec((HEADS, tt, width)),
                  _const_spec((width, width))],
        out_specs=pl.BlockSpec((tt, width), lambda b, i: (b * nt + i, 0)),
        out_shape=jax.ShapeDtypeStruct((t, width), F32),
        scratch_shapes=[pltpu.VMEM((width, width), F32), pltpu.VMEM((aall.shape[0], width), F32)],
        compiler_params=_cparams("parallel", "arbitrary"),
        name="hgrn2_mixer",
    )(h_hg, lb.reshape(1, width), norm_g.reshape(1, width), aall, level, _head_lane_masks(tt), _head_block_ones())


def _ret_kernel(h_ref, cos_ref, sin_ref, dmat_ref, qdec_ref, kdec_ref, cdec_ref, blk_ref, g_ref, b_ref,
                o_ref, st_ref):
    tt = h_ref.shape[0]
    width = HEADS * HEAD_DIM

    @pl.when(pl.program_id(1) == 0)
    def _():
        st_ref[...] = jnp.zeros_like(st_ref)

    cos = cos_ref[...]
    sin = sin_ref[...]
    q = (h_ref[:, 0:width] * cos + h_ref[:, 4 * width:5 * width] * sin) * (HEAD_DIM ** -0.5)
    k = h_ref[:, width:2 * width] * cos + h_ref[:, 5 * width:6 * width] * sin
    v = h_ref[:, 2 * width:3 * width]
    g = h_ref[:, 3 * width:4 * width]
    kb = k.astype(BF16)
    vb = v.astype(BF16)
    ones_blk = blk_ref[...]
    o = jnp.zeros((tt, width), F32)
    for hd in range(HEADS):
        hm = _head_mask((tt, width), hd, 1)
        s = _dot_nt(jnp.where(hm, q, 0.0).astype(BF16), kb) * dmat_ref[hd]
        o = o + _dot(s.astype(BF16), jnp.where(hm, vb, jnp.zeros_like(vb)))
    st = st_ref[...]
    st_m = jnp.where(ones_blk > 0, st, 0.0).astype(BF16)
    o = o + _dot((q * qdec_ref[...]).astype(BF16), st_m)
    st_ref[...] = st * cdec_ref[...] + _dot_tn((k * kdec_ref[...]).astype(BF16), vb)
    mu = _dot_exact_rhs(o, ones_blk) * (1.0 / HEAD_DIM)
    oc = o - mu
    var = _dot_exact_rhs(oc * oc, ones_blk) * (1.0 / HEAD_DIM)
    o = oc * lax.rsqrt(var + EPS) * g_ref[...] + b_ref[...]
    o_ref[...] = (g * _sigmoid(g)) * o


def _retention_tables(tt):
    log_gamma = jnp.log(1.0 - 2.0 ** (-5.0 - jnp.arange(HEADS, dtype=F32)))
    t_idx = jnp.arange(tt, dtype=F32)
    rel = t_idx[:, None] - t_idx[None, :]
    dmat = jnp.where(rel >= 0, jnp.exp(jnp.maximum(rel, 0.0)[None] * log_gamma[:, None, None]), 0.0)
    lanes = lambda a: jnp.repeat(a, HEAD_DIM, axis=-1)
    qdec = lanes(jnp.exp((t_idx + 1.0)[:, None] * log_gamma[None, :]))
    kdec = lanes(jnp.exp((tt - 1 - t_idx)[:, None] * log_gamma[None, :]))
    cdec = lanes(jnp.exp(tt * log_gamma)[None, :])
    return dmat, qdec, kdec, cdec


def retention_mixer(h_ret, cos_r, sin_r, gn_g, gn_b, bsz, seq):
    t = bsz * seq
    tt = TT_RET
    width = HEADS * HEAD_DIM
    nt = seq // tt
    dmat, qdec, kdec, cdec = _retention_tables(tt)
    row = lambda w: pl.BlockSpec((tt, w), lambda b, i: (b * nt + i, 0))
    return pl.pallas_call(
        _ret_kernel,
        grid=(bsz, nt),
        in_specs=[row(W_RET_IN), row(width), row(width), _const_spec((HEADS, tt, tt)), _const_spec((tt, width)),
                  _const_spec((tt, width)), _const_spec((1, width)), _const_spec((width, width)),
                  _const_spec((1, width)), _const_spec((1, width))],
        out_specs=row(width),
        out_shape=jax.ShapeDtypeStruct((t, width), F32),
        scratch_shapes=[pltpu.VMEM((width, width), F32)],
        compiler_params=_cparams("parallel", "arbitrary"),
        name="retention_mixer",
    )(h_ret, cos_r, sin_r, dmat, qdec, kdec, cdec, _head_block_ones(), gn_g.reshape(1, width),
      gn_b.reshape(1, width))


def _store_token_tiles(ref, val):
    tm = val.shape[0]
    for j in range(D_MODEL // LANES):
        ref[pl.ds(j, tm, stride=SUBLANES), :] = val[:, j * LANES:(j + 1) * LANES]


def _load_token_tiles(ref, tm):
    return jnp.concatenate([ref[pl.ds(j, tm, stride=SUBLANES), :] for j in range(D_MODEL // LANES)], axis=-1)


def _mix_kernel(x_ref, ya_ref, yb_ref, yc_ref, yd_ref, wg_ref, wb_ref, wo_ref, g_ref, b_ref, o_ref):
    x = x_ref[...]
    xb = x.astype(BF16)
    acc = jnp.zeros(x.shape, F32)
    for n, y_ref in enumerate((ya_ref, yb_ref, yc_ref, yd_ref)):
        gate = _sigmoid(_dot(xb, wg_ref[:, n * D_MODEL:(n + 1) * D_MODEL]))
        acc = acc + gate * _dot(y_ref[...].astype(BF16), wb_ref[n])
    z = ALPHA * x + _dot(acc.astype(BF16), wo_ref[...])
    o_ref[...] = _layer_norm_rows(z, g_ref[...], b_ref[...])


def mix_branches(x, ys, w_gate, w_branch, w_o, ln_g, ln_b):
    t = x.shape[0]
    tm = TM_PROJ
    row = lambda w: pl.BlockSpec((tm, w), lambda i: (i, 0))
    return pl.pallas_call(
        _mix_kernel,
        grid=(t // tm,),
        in_specs=[row(D_MODEL)] + [row(MIX_WIDTH)] * 4 +
                 [_const_spec((D_MODEL, N_BRANCH * D_MODEL)), _const_spec((N_BRANCH, MIX_WIDTH, D_MODEL)),
                  _const_spec((D_MODEL, D_MODEL)), _const_spec((1, D_MODEL)), _const_spec((1, D_MODEL))],
        out_specs=row(D_MODEL),
        out_shape=jax.ShapeDtypeStruct((t, D_MODEL), F32),
        compiler_params=_cparams("parallel"),
        name="mix_branches",
    )(x, *ys, w_gate, w_branch, w_o, ln_g.reshape(1, D_MODEL), ln_b.reshape(1, D_MODEL))


def _ffn_kernel(x_ref, wg_ref, wu_ref, wd_ref, o_ref, xb_ref, acc_ref):
    c = pl.program_id(1)

    @pl.when(c == 0)
    def _():
        xb_ref[...] = x_ref[...].astype(BF16)
        acc_ref[...] = jnp.zeros_like(acc_ref)

    xb = xb_ref[...]
    gate = _dot(xb, wg_ref[...])
    act = gate * _sigmoid(gate) * _dot(xb, wu_ref[...])
    acc_ref[...] += _dot(act.astype(BF16), wd_ref[...])

    @pl.when(c == pl.num_programs(1) - 1)
    def _():
        o_ref[...] = acc_ref[...]


def dense_ffn(x, w_gate, w_up, w_down):
    t = x.shape[0]
    tm, tf = TM_FFN, TF_FFN
    return pl.pallas_call(
        _ffn_kernel,
        grid=(t // tm, D_FF // tf),
        in_specs=[pl.BlockSpec((tm, D_MODEL), lambda i, c: (i, 0)),
                  pl.BlockSpec((D_MODEL, tf), lambda i, c: (0, c)),
                  pl.BlockSpec((D_MODEL, tf), lambda i, c: (0, c)),
                  pl.BlockSpec((tf, D_MODEL), lambda i, c: (c, 0))],
        out_specs=pl.BlockSpec((tm, D_MODEL), lambda i, c: (i, 0)),
        out_shape=jax.ShapeDtypeStruct((t, D_MODEL), F32),
        scratch_shapes=[pltpu.VMEM((tm, D_MODEL), BF16), pltpu.VMEM((tm, D_MODEL), F32)],
        compiler_params=_cparams("parallel", "arbitrary"),
        name="dense_ffn",
    )(x, w_gate, w_up, w_down)


def _router_kernel(x_ref, wh_ref, wl_ref, upper_ref, meta_ref, cnt_ref, carry_ref):
    tm = x_ref.shape[0]

    @pl.when(pl.program_id(0) == 0)
    def _():
        carry_ref[...] = jnp.zeros_like(carry_ref)

    xh, xl = _split2(x_ref[...])
    wh = wh_ref[...]
    logits = _dot_nt(wh, xh) + _dot_nt(wh, xl) + _dot_nt(wl_ref[...], xh)
    e_idx = lax.broadcasted_iota(jnp.int32, logits.shape, 0)
    m1 = jnp.max(logits, axis=0, keepdims=True)
    i1 = jnp.min(jnp.where(logits == m1, e_idx, N_EXPERTS), axis=0, keepdims=True)
    rest = jnp.where(e_idx == i1, -jnp.inf, logits)
    m2 = jnp.max(rest, axis=0, keepdims=True)
    i2 = jnp.min(jnp.where(rest == m2, e_idx, N_EXPERTS), axis=0, keepdims=True)
    ex = jnp.exp(m2 - m1)
    w1 = 1.0 / (1.0 + ex)
    w2 = ex / (1.0 + ex)
    sel1 = e_idx == i1
    sel2 = e_idx == i2
    onehot = jnp.where(sel1 | sel2, 1.0, 0.0)
    carry = carry_ref[...]
    ranks = _dot(onehot.astype(BF16), upper_ref[...]) + carry[:, 0:1]
    r1 = jnp.sum(jnp.where(sel1, ranks, 0.0), axis=0, keepdims=True)
    r2 = jnp.sum(jnp.where(sel2, ranks, 0.0), axis=0, keepdims=True)
    row = lax.broadcasted_iota(jnp.int32, (SUBLANES, tm), 0)
    vals = (i1.astype(F32), i2.astype(F32), w1, w2, r1, r2)
    meta = jnp.zeros((SUBLANES, tm), F32)
    for n, val in enumerate(vals):
        meta = jnp.where(row == n, val, meta)
    meta_ref[...] = meta
    carry = carry + jnp.sum(onehot, axis=1, keepdims=True)
    carry_ref[...] = carry
    cnt_ref[...] = carry


def moe_router(x, w_router):
    t = x.shape[0]
    tm = TM_ROUTE
    wt = w_router.T
    wh = wt.astype(BF16)
    wl = (wt - wh.astype(F32)).astype(BF16)
    upper = jnp.asarray(np.triu(np.ones((tm, tm), np.float32), 1), BF16)
    return pl.pallas_call(
        _router_kernel,
        grid=(t // tm,),
        in_specs=[pl.BlockSpec((tm, D_MODEL), lambda i: (i, 0)), _const_spec((N_EXPERTS, D_MODEL)),
                  _const_spec((N_EXPERTS, D_MODEL)), _const_spec((tm, tm))],
        out_specs=[pl.BlockSpec((SUBLANES, tm), lambda i: (0, i)), pl.BlockSpec((N_EXPERTS, LANES), lambda i: (0, 0))],
        out_shape=[jax.ShapeDtypeStruct((SUBLANES, t), F32), jax.ShapeDtypeStruct((N_EXPERTS, LANES), F32)],
        scratch_shapes=[pltpu.VMEM((N_EXPERTS, LANES), F32)],
        compiler_params=_cparams("arbitrary"),
        name="moe_router",
    )(x, wh, wl, upper)


def _token_rows(ref, tok):
    return ref.at[pl.ds(pl.multiple_of(tok * SUBLANES, SUBLANES), SUBLANES), :]


def _start_then_wait(n, copies):
    def start(r, carry):
        for cp in copies(r):
            cp.start()
        return carry

    def wait(r, carry):
        for cp in copies(r):
            cp.wait()
        return carry

    lax.fori_loop(0, n, start, 0)
    return lambda: lax.fori_loop(0, n, wait, 0)


def _dispatch_kernel(dest_ref, x_ref, init_ref, xs_ref, buf_ref, sem):
    del init_ref
    td = x_ref.shape[0]
    _store_token_tiles(buf_ref, x_ref[...])

    def copies(r):
        src = _token_rows(buf_ref, r)
        return (pltpu.make_async_copy(src, _token_rows(xs_ref, dest_ref[0, 0, 2 * r]), sem),
                pltpu.make_async_copy(src, _token_rows(xs_ref, dest_ref[0, 0, 2 * r + 1]), sem))

    _start_then_wait(td, copies)()


def moe_dispatch(x, dest3, p_len):
    t = x.shape[0]
    td = dest3.shape[2] // 2
    init = jnp.zeros((p_len * SUBLANES, LANES), F32)
    return pl.pallas_call(
        _dispatch_kernel,
        grid=(t // td,),
        in_specs=[pl.BlockSpec((1, 1, 2 * td), lambda i: (i, 0, 0), memory_space=pltpu.SMEM),
                  pl.BlockSpec((td, D_MODEL), lambda i: (i, 0)), pl.BlockSpec(memory_space=pl.ANY)],
        out_specs=pl.BlockSpec(memory_space=pl.ANY),
        out_shape=jax.ShapeDtypeStruct((p_len * SUBLANES, LANES), F32),
        scratch_shapes=[pltpu.VMEM((td * SUBLANES, LANES), F32), pltpu.SemaphoreType.DMA(())],
        input_output_aliases={2: 0},
        compiler_params=pltpu.CompilerParams(dimension_semantics=("arbitrary",), has_side_effects=True),
        name="moe_dispatch",
    )(dest3, x, init)


def _expert_kernel(te_ref, nv_ref, xs_ref, wg_ref, wu_ref, wd_ref, ys_ref, xb_ref, acc_ref):
    i = pl.program_id(0)
    c = pl.program_id(1)
    tm = xb_ref.shape[0]
    valid = i < nv_ref[0]

    @pl.when(valid & (c == 0))
    def _():
        xb_ref[...] = _load_token_tiles(xs_ref, tm).astype(BF16)
        acc_ref[...] = jnp.zeros_like(acc_ref)

    @pl.when(valid)
    def _():
        xb = xb_ref[...]
        gate = _dot(xb, wg_ref[...].astype(BF16))
        act = gate * _sigmoid(gate) * _dot(xb, wu_ref[...].astype(BF16))
        acc_ref[...] += _dot(act.astype(BF16), wd_ref[...].astype(BF16))

    @pl.when(c == pl.num_programs(1) - 1)
    def _():
        _store_token_tiles(ys_ref, jnp.where(valid, acc_ref[...], 0.0))


def moe_experts(xs, tile_expert, n_valid, layer, w_gate, w_up, w_down):
    tm, tf = TM_MOE, TF_FFN
    n_tiles = xs.shape[0] // (tm * SUBLANES)
    nc = D_FF // tf

    def chunk(i, c, nv):
        return jnp.where(i < nv[0], c, nc - 1)

    grid_spec = pltpu.PrefetchScalarGridSpec(
        num_scalar_prefetch=2,
        grid=(n_tiles, nc),
        in_specs=[pl.BlockSpec((tm * SUBLANES, LANES), lambda i, c, te, nv: (jnp.minimum(i, nv[0] - 1), 0)),
                  pl.BlockSpec((None, None, D_MODEL, tf), lambda i, c, te, nv: (layer, te[i], 0, chunk(i, c, nv))),
                  pl.BlockSpec((None, None, D_MODEL, tf), lambda i, c, te, nv: (layer, te[i], 0, chunk(i, c, nv))),
                  pl.BlockSpec((None, None, tf, D_MODEL), lambda i, c, te, nv: (layer, te[i], chunk(i, c, nv), 0))],
        out_specs=pl.BlockSpec((tm * SUBLANES, LANES), lambda i, c, te, nv: (i, 0)),
        scratch_shapes=[pltpu.VMEM((tm, D_MODEL), BF16), pltpu.VMEM((tm, D_MODEL), F32)],
    )
    return pl.pallas_call(
        _expert_kernel,
        grid_spec=grid_spec,
        out_shape=jax.ShapeDtypeStruct(xs.shape, F32),
        compiler_params=_cparams("arbitrary", "arbitrary"),
        name="moe_experts",
    )(tile_expert, n_valid, xs, w_gate, w_up, w_down)


def moe_ffn(x, w_router, layer, w_gate, w_up, w_down):
    t = x.shape[0]
    tm = TM_MOE
    meta, cnt = moe_router(x, w_router)
    counts = cnt[:, 0].astype(jnp.int32)
    padded = (counts + tm - 1) // tm * tm
    pad_end = jnp.cumsum(padded)
    pad_start = pad_end - padded
    experts = meta[0:2].astype(jnp.int32)
    start_of = jnp.sum(jnp.where(experts[..., None] == jnp.arange(N_EXPERTS), pad_start, 0), axis=-1)
    dest = (start_of + meta[4:6].astype(jnp.int32)).T
    dest3 = dest.reshape(t // TD_DISPATCH, 1, 2 * TD_DISPATCH)
    n_tiles = (2 * t) // tm + N_EXPERTS
    n_valid = (pad_end[-1] // tm).astype(jnp.int32).reshape(1)
    tile_start = jnp.arange(n_tiles, dtype=jnp.int32) * tm
    tile_expert = jnp.minimum(jnp.sum(pad_end[None, :] <= tile_start[:, None], axis=1), N_EXPERTS - 1)
    last_expert = jnp.sum(jnp.where(jnp.arange(n_tiles) == n_valid[0] - 1, tile_expert, 0))
    tile_expert = jnp.where(jnp.arange(n_tiles) < n_valid[0], tile_expert, last_expert).astype(jnp.int32)
    xs = moe_dispatch(x, dest3, n_tiles * tm)
    ys = moe_experts(xs, tile_expert, n_valid, layer, w_gate, w_up, w_down)
    return ys, dest3, meta[2:4].T


def _post_kernel(x_ref, p_ref, wpg_ref, wpp_ref, g_ref, b_ref, f_ref, o_ref):
    x = x_ref[...]
    ple = _sigmoid(_dot(x.astype(BF16), wpg_ref[...])) * _dot(p_ref[...].astype(BF16), wpp_ref[...])
    o_ref[...] = _layer_norm_rows(ALPHA * x + f_ref[...] + ple, g_ref[...], b_ref[...])


def _post_moe_kernel(dest_ref, x_ref, p_ref, wpg_ref, wpp_ref, g_ref, b_ref, w_ref, ys_ref, o_ref,
                     buf0_ref, buf1_ref, sem):
    x = x_ref[...]
    tm = x.shape[0]

    def copies(r):
        return (pltpu.make_async_copy(_token_rows(ys_ref, dest_ref[0, 0, 2 * r]), _token_rows(buf0_ref, r), sem),
                pltpu.make_async_copy(_token_rows(ys_ref, dest_ref[0, 0, 2 * r + 1]), _token_rows(buf1_ref, r), sem))

    wait_all = _start_then_wait(tm, copies)
    ple = _sigmoid(_dot(x.astype(BF16), wpg_ref[...])) * _dot(p_ref[...].astype(BF16), wpp_ref[...])
    wait_all()
    w = w_ref[...]
    f = w[:, 0:1] * _load_token_tiles(buf0_ref, tm) + w[:, 1:2] * _load_token_tiles(buf1_ref, tm)
    o_ref[...] = _layer_norm_rows(ALPHA * x + f + ple, g_ref[...], b_ref[...])


def post_layer(x, p, layer, w_ple_gate, w_ple_proj, ln_g, ln_b, f=None, moe=None):
    t = x.shape[0]
    tm = TM_PROJ if moe is None else TD_DISPATCH
    row = lambda w: pl.BlockSpec((tm, w), lambda i: (i, 0))
    in_specs = [row(D_MODEL), pl.BlockSpec((None, tm, PLE_DIM), lambda i: (layer, i, 0)),
                _const_spec((D_MODEL, D_MODEL)), _const_spec((PLE_DIM, D_MODEL)),
                _const_spec((1, D_MODEL)), _const_spec((1, D_MODEL))]
    args = [x, p, w_ple_gate, w_ple_proj, ln_g.reshape(1, D_MODEL), ln_b.reshape(1, D_MODEL)]
    scratch = []
    if moe is None:
        body = _post_kernel
        in_specs += [row(D_MODEL)]
        args += [f]
    else:
        body = _post_moe_kernel
        ys, dest3, w2 = moe
        in_specs = [pl.BlockSpec((1, 1, 2 * tm), lambda i: (i, 0, 0), memory_space=pltpu.SMEM)] + in_specs
        in_specs += [row(2), pl.BlockSpec(memory_space=pl.ANY)]
        args = [dest3] + args + [w2, ys]
        scratch = [pltpu.VMEM((tm * SUBLANES, LANES), F32), pltpu.VMEM((tm * SUBLANES, LANES), F32),
                   pltpu.SemaphoreType.DMA(())]
    return pl.pallas_call(
        body,
        grid=(t // tm,),
        in_specs=in_specs,
        out_specs=row(D_MODEL),
        out_shape=jax.ShapeDtypeStruct((t, D_MODEL), F32),
        scratch_shapes=scratch,
        compiler_params=_cparams("parallel" if moe is None else "arbitrary"),
        name="post_layer",
    )(*args)


def split_in_weights(w):
    sizes = (MIX_WIDTH, MLA_Q_LORA, MLA_KV_LORA, MLA_ROPE) + (HEADS * HEAD_DIM,) * 8 + (N_BRANCH * D_MODEL,)
    parts, start = [], 0
    for n in sizes:
        parts.append(w[:, start:start + n])
        start += n
    u_s5, c_q, c_kv, k_rope, hq, hf, hi, hg, rq, rk, rv, rg, gate = parts
    k_rot = jnp.concatenate([-k_rope[:, MLA_ROPE // 2:], k_rope[:, :MLA_ROPE // 2]], axis=1)
    w_mla = jnp.zeros((D_MODEL, W_MLA_IN), F32)
    w_mla = w_mla.at[:, 0:MLA_Q_LORA].set(c_q).at[:, 256:384].set(c_kv)
    w_mla = w_mla.at[:, 384 + MLA_NOPE:384 + MLA_NOPE + MLA_ROPE].set(k_rope)
    w_mla = w_mla.at[:, 512 + MLA_NOPE:512 + MLA_NOPE + MLA_ROPE].set(k_rot)
    w_hg = jnp.concatenate([hq, hf, hi, hg], axis=1)
    w_ret = jnp.concatenate([rq, rk, rv, rg, _rot_half_cols(rq, HEAD_DIM), _rot_half_cols(rk, HEAD_DIM)], axis=1)
    return [a.astype(BF16) for a in (w_mla, u_s5, w_hg, w_ret)], gate.astype(BF16)


def kernel(x, p, positions, w_in, s5_a_re, s5_a_im, s5_log_dt, s5_b_re, s5_b_im, s5_c_re, s5_c_im, s5_d, s5_w_glu, mla_q_norm, mla_kv_norm, mla_w_uq, mla_w_ukv, hg_lb_raw, hg_norm, ret_gn_g, ret_gn_b, w_branch, w_o, ln1_g, ln1_b, ff_w_gate, ff_w_up, ff_w_down, moe_router, moe_w_gate, moe_w_up, moe_w_down, ple_w_gate, ple_w_proj, ln2_g, ln2_b):
    bsz, seq, _ = x.shape
    t = bsz * seq
    depth = w_in.shape[0]
    x = x.reshape(t, D_MODEL)
    p = p.reshape(depth, t, PLE_DIM)
    cos_m, sin_m, cos_r, sin_r = rope_tables(positions)
    lb_all = jnp.cumsum(jax.nn.softmax(hg_lb_raw.astype(F32), axis=0), axis=0)
    lb_all = lb_all - lb_all[0]
    for i in range(depth):
        in_ws, w_gate = split_in_weights(w_in[i])
        h_mla, h_s5, h_hg, h_ret = in_projection(x, in_ws)
        y_a = s5_mixer(h_s5, bsz, seq, s5_params(s5_a_re[i], s5_a_im[i], s5_log_dt[i], s5_b_re[i], s5_b_im[i],
                                                s5_c_re[i], s5_c_im[i], s5_d[i], s5_w_glu[i]))
        y_b = mla_mixer(h_mla, cos_m, sin_m, bsz, seq,
                        mla_params(mla_q_norm[i], mla_kv_norm[i], mla_w_uq[i], mla_w_ukv[i]))
        y_c = hgrn2_mixer(h_hg, lb_all[i], hg_norm[i], bsz, seq)
        y_d = retention_mixer(h_ret, cos_r, sin_r, ret_gn_g[i], ret_gn_b[i], bsz, seq)
        x = mix_branches(x, (y_a, y_b, y_c, y_d), w_gate, w_branch[i].astype(BF16), w_o[i].astype(BF16),
                         ln1_g[i], ln1_b[i])
        ple_args = (p, i, ple_w_gate[i].astype(BF16), ple_w_proj[i].astype(BF16), ln2_g[i], ln2_b[i])
        if i % 2 == 1:
            x = post_layer(x, *ple_args, moe=moe_ffn(x, moe_router[i // 2], i // 2, moe_w_gate, moe_w_up,
                                                     moe_w_down))
        else:
            f = dense_ffn(x, ff_w_gate[i // 2].astype(BF16), ff_w_up[i // 2].astype(BF16),
                          ff_w_down[i // 2].astype(BF16))
            x = post_layer(x, *ple_args, f=f)
    return x.reshape(bsz, seq, D_MODEL)
```

```python
import functools
import math

import numpy as np
import jax
import jax.numpy as jnp
from jax import lax
from jax.experimental import pallas as pl
from jax.experimental.pallas import tpu as pltpu

F32 = jnp.float32
BF16 = jnp.bfloat16

D_MODEL = 1024
DEPTH = 4
PLE_DIM = 256
MIX_WIDTH = 256
N_BRANCH = 4
S5_GROUP = 16
S5_GROUPS = 16
S5_STATE = 64
MLA_HEADS = 4
MLA_NOPE = 64
MLA_ROPE = 32
MLA_V = 64
MLA_Q_LORA = 192
MLA_KV_LORA = 128
HEADS = 4
HEAD_DIM = 64
D_FF = 3584
N_EXPERTS = 8
ROPE_BASE = 10000.0
EPS = 1e-5
NEG_INF = -1e30
ALPHA = (2 * DEPTH) ** 0.25

LANES = 128
SUBLANES = 8
VMEM_LIMIT = 56 * 1024 * 1024

TM_PROJ = 512
TS_S5 = 64
TQ_ATT = 256
NB_ATT = 2
TT_HG = 128
NB_HG = 4
TT_RET = 256
NB_RET = 4
TM_FFN = 1024
TF_FFN = 512
TM_MOE = 1024
TM_ROUTE = 512
TD_DISPATCH = 512


def _cparams(*sem):
    return pltpu.CompilerParams(dimension_semantics=sem, vmem_limit_bytes=VMEM_LIMIT)


def _const_spec(shape):
    nd = len(shape)
    return pl.BlockSpec(shape, lambda *_: (0,) * nd, pipeline_mode=pl.Buffered(1))


def _dot(a, b):
    return jnp.dot(a, b, preferred_element_type=F32)


def _dot_nt(a, b):
    return lax.dot_general(a, b, (((1,), (1,)), ((), ())), preferred_element_type=F32)


def _dot_tn(a, b):
    return lax.dot_general(a, b, (((0,), (0,)), ((), ())), preferred_element_type=F32)


def _split2(x):
    hi = x.astype(BF16)
    lo = (x - hi.astype(F32)).astype(BF16)
    return hi, lo


def _split3(x):
    h1 = x.astype(BF16)
    r1 = x - h1.astype(F32)
    h2 = r1.astype(BF16)
    h3 = (r1 - h2.astype(F32)).astype(BF16)
    return h1, h2, h3


def _dot_exact_rhs(x, m):
    hi, lo = _split2(x)
    return _dot(hi, m) + _dot(lo, m)


def _dot_exact_lhs(m, x):
    hi, lo = _split2(x)
    return _dot(m, hi) + _dot(m, lo)


def _sigmoid(x):
    return 1.0 / (1.0 + jnp.exp(-x))


def _layer_norm_rows(z, g, b):
    mu = jnp.mean(z, axis=-1, keepdims=True)
    zc = z - mu
    var = jnp.mean(zc * zc, axis=-1, keepdims=True)
    return zc * lax.rsqrt(var + EPS) * g + b


def _rope_kernel(pos_ref, fm_ref, fr_ref, cm_ref, sm_ref, cr_ref, sr_ref):
    pos = pos_ref[...]
    ang_m = pos * fm_ref[...]
    lane = lax.broadcasted_iota(jnp.int32, ang_m.shape, 1)
    in_pe = (lane >= MLA_NOPE) & (lane < MLA_NOPE + MLA_ROPE)
    cm_ref[...] = jnp.where(lane < MLA_NOPE, 1.0, jnp.where(in_pe, jnp.cos(ang_m), 0.0))
    sm_ref[...] = jnp.where(in_pe, jnp.sin(ang_m), 0.0)
    ang_r = pos * fr_ref[...]
    cr_ref[...] = jnp.cos(ang_r)
    sr_ref[...] = jnp.sin(ang_r)


def rope_tables(positions):
    t = positions.size
    tm = TM_PROJ
    pos = positions.reshape(t, 1).astype(F32)
    inv_m = ROPE_BASE ** (-jnp.arange(MLA_ROPE // 2, dtype=F32) / (MLA_ROPE // 2))
    inv_r = ROPE_BASE ** (-jnp.arange(HEAD_DIM // 2, dtype=F32) / (HEAD_DIM // 2))
    fm = jnp.zeros((1, LANES), F32).at[0, MLA_NOPE:MLA_NOPE + MLA_ROPE].set(jnp.concatenate([inv_m, inv_m]))
    fr = jnp.tile(jnp.concatenate([inv_r, inv_r]), HEADS).reshape(1, HEADS * HEAD_DIM)
    row = lambda w: pl.BlockSpec((tm, w), lambda i: (i, 0))
    return pl.pallas_call(
        _rope_kernel,
        grid=(t // tm,),
        in_specs=[row(1), _const_spec((1, LANES)), _const_spec((1, 256))],
        out_specs=[row(LANES), row(LANES), row(256), row(256)],
        out_shape=[jax.ShapeDtypeStruct((t, LANES), F32), jax.ShapeDtypeStruct((t, LANES), F32),
                   jax.ShapeDtypeStruct((t, 256), F32), jax.ShapeDtypeStruct((t, 256), F32)],
        compiler_params=_cparams("parallel"),
        name="rope_tables",
    )(pos, fm, fr)


W_MLA_IN = 640
W_RET_IN = 1536
_N_CHUNK = 256


def _inproj_kernel(x_ref, *refs):
    n = len(refs) // 2
    xb = x_ref[...].astype(BF16)
    for w_ref, o_ref in zip(refs[:n], refs[n:]):
        width = w_ref.shape[1]
        for c0 in range(0, width, _N_CHUNK):
            c1 = min(c0 + _N_CHUNK, width)
            o_ref[:, c0:c1] = _dot(xb, w_ref[:, c0:c1])


def in_projection(x, weights):
    t = x.shape[0]
    tm = TM_PROJ
    return pl.pallas_call(
        _inproj_kernel,
        grid=(t // tm,),
        in_specs=[pl.BlockSpec((tm, D_MODEL), lambda i: (i, 0))] + [_const_spec(w.shape) for w in weights],
        out_specs=[pl.BlockSpec((tm, w.shape[1]), lambda i: (i, 0)) for w in weights],
        out_shape=[jax.ShapeDtypeStruct((t, w.shape[1]), F32) for w in weights],
        compiler_params=_cparams("parallel"),
        name="in_projection",
    )(x, *weights)


S5_HALF = S5_GROUPS * S5_STATE


def _s5_kernel(u_ref, perm_ref, permt_ref, bmat_ref, ar_ref, ai_ref, cmat_ref, d_ref, wglu_ref, y_ref,
               bu_ref, hs_ref, h_ref):
    nb, ts, width = u_ref.shape
    n_half = S5_HALF // LANES

    @pl.when(pl.program_id(0) == 0)
    def _():
        h_ref[...] = jnp.zeros_like(h_ref)

    u = u_ref[...].reshape(nb * ts, width)
    u_tm = _dot(perm_ref[...], u.astype(BF16)).astype(BF16)
    bu_ref[...] = _dot(u_tm, bmat_ref[...])
    ar = [jnp.broadcast_to(ar_ref[:, c * LANES:(c + 1) * LANES], (nb, LANES)) for c in range(n_half)]
    ai = [jnp.broadcast_to(ai_ref[:, c * LANES:(c + 1) * LANES], (nb, LANES)) for c in range(n_half)]
    h_re = [h_ref[:, c * LANES:(c + 1) * LANES] for c in range(n_half)]
    h_im = [h_ref[:, S5_HALF + c * LANES:S5_HALF + (c + 1) * LANES] for c in range(n_half)]
    for n in range(ts):
        rows = slice(n * nb, (n + 1) * nb)
        for c in range(n_half):
            re_cols = slice(c * LANES, (c + 1) * LANES)
            im_cols = slice(S5_HALF + c * LANES, S5_HALF + (c + 1) * LANES)
            new_re = ar[c] * h_re[c] - ai[c] * h_im[c] + bu_ref[rows, re_cols]
            new_im = ar[c] * h_im[c] + ai[c] * h_re[c] + bu_ref[rows, im_cols]
            h_re[c], h_im[c] = new_re, new_im
            hs_ref[rows, re_cols] = new_re
            hs_ref[rows, im_cols] = new_im
    for c in range(n_half):
        h_ref[:, c * LANES:(c + 1) * LANES] = h_re[c]
        h_ref[:, S5_HALF + c * LANES:S5_HALF + (c + 1) * LANES] = h_im[c]
    y_tm = _dot(hs_ref[...].astype(BF16), cmat_ref[...])
    y = _dot_exact_lhs(permt_ref[...], y_tm) + d_ref[...] * u
    y = jax.nn.gelu(y)
    y = y * _sigmoid(_dot(y.astype(BF16), wglu_ref[...]))
    y_ref[...] = y.reshape(nb, ts, width)


def _step_major_permutation(nb, ts):
    p = np.zeros((nb * ts, nb * ts), np.float32)
    b, n = np.meshgrid(np.arange(nb), np.arange(ts), indexing="ij")
    p[(n * nb + b).ravel(), (b * ts + n).ravel()] = 1.0
    return jnp.asarray(p, BF16), jnp.asarray(p.T, BF16)


def s5_mixer(u, bsz, seq, prm):
    ts = TS_S5
    u3 = u.reshape(bsz, seq, MIX_WIDTH)
    blk = pl.BlockSpec((bsz, ts, MIX_WIDTH), lambda i: (0, i, 0))
    perm, perm_t = _step_major_permutation(bsz, ts)
    y = pl.pallas_call(
        _s5_kernel,
        grid=(seq // ts,),
        in_specs=[blk, _const_spec(perm.shape), _const_spec(perm.shape),
                  _const_spec((MIX_WIDTH, 2 * S5_HALF)), _const_spec((1, S5_HALF)), _const_spec((1, S5_HALF)),
                  _const_spec((2 * S5_HALF, MIX_WIDTH)), _const_spec((1, MIX_WIDTH)),
                  _const_spec((MIX_WIDTH, MIX_WIDTH))],
        out_specs=blk,
        out_shape=jax.ShapeDtypeStruct((bsz, seq, MIX_WIDTH), F32),
        scratch_shapes=[pltpu.VMEM((bsz * ts, 2 * S5_HALF), F32), pltpu.VMEM((bsz * ts, 2 * S5_HALF), F32),
                        pltpu.VMEM((bsz, 2 * S5_HALF), F32)],
        compiler_params=_cparams("arbitrary"),
        name="s5_mixer",
    )(u3, perm, perm_t, prm["bmat"], prm["a_re"], prm["a_im"], prm["cmat"], prm["d"], prm["w_glu"])
    return y.reshape(bsz * seq, MIX_WIDTH)


def s5_params(a_re, a_im, log_dt, b_re, b_im, c_re, c_im, d_skip, w_glu):
    dt = jnp.exp(log_dt)[:, None]
    mag = jnp.exp(a_re * dt)
    abar_re, abar_im = mag * jnp.cos(a_im * dt), mag * jnp.sin(a_im * dt)
    den = a_re * a_re + a_im * a_im
    num_re, num_im = abar_re - 1.0, abar_im
    coef_re = (num_re * a_re + num_im * a_im) / den
    coef_im = (num_im * a_re - num_re * a_im) / den
    bbar_re = coef_re[..., None] * b_re - coef_im[..., None] * b_im
    bbar_im = coef_re[..., None] * b_im + coef_im[..., None] * b_re
    eye = jnp.eye(S5_GROUPS, dtype=F32)
    b_bd_re = jnp.einsum("gpc,gh->gchp", bbar_re, eye).reshape(MIX_WIDTH, S5_HALF)
    b_bd_im = jnp.einsum("gpc,gh->gchp", bbar_im, eye).reshape(MIX_WIDTH, S5_HALF)
    c_bd_re = jnp.einsum("gcp,gh->gphc", c_re, eye).reshape(S5_HALF, MIX_WIDTH)
    c_bd_im = jnp.einsum("gcp,gh->gphc", c_im, eye).reshape(S5_HALF, MIX_WIDTH)
    return {
        "bmat": jnp.concatenate([b_bd_re, b_bd_im], axis=1).astype(BF16),
        "cmat": jnp.concatenate([c_bd_re, -c_bd_im], axis=0).astype(BF16),
        "a_re": abar_re.reshape(1, S5_HALF), "a_im": abar_im.reshape(1, S5_HALF),
        "d": d_skip.reshape(1, MIX_WIDTH), "w_glu": w_glu.astype(BF16),
    }


MLA_QK_PAD = 128


def _mla_prep_kernel(h_ref, cos_ref, sin_ref, gq_ref, gkv_ref, wq_ref, wqr_ref, wk_ref, wv_ref,
                     q_ref, k_ref, v_ref):
    h = h_ref[...]
    cos = cos_ref[...]
    sin = sin_ref[...]
    c_q = h[:, 0:256]
    c_kv = h[:, 256:384]
    k_pe = h[:, 384:512] * cos + h[:, 512:640] * sin
    qn = c_q * lax.rsqrt(jnp.sum(c_q * c_q, axis=-1, keepdims=True) / MLA_Q_LORA + EPS) * gq_ref[...]
    kvn = c_kv * lax.rsqrt(jnp.mean(c_kv * c_kv, axis=-1, keepdims=True) + EPS) * gkv_ref[...]
    qn = qn.astype(BF16)
    kvn = kvn.astype(BF16)
    scale = (MLA_NOPE + MLA_ROPE) ** -0.5 * math.log2(math.e)
    for hd in range(MLA_HEADS):
        q = _dot(qn, wq_ref[hd]) * cos + _dot(qn, wqr_ref[hd]) * sin
        q_ref[hd] = (q * scale).astype(BF16)
        k_ref[hd] = (_dot(kvn, wk_ref[hd]) + k_pe).astype(BF16)
        v_ref[hd] = _dot(kvn, wv_ref[hd]).astype(BF16)


def _flash_kernel(q_ref, k_ref, v_ref, o_ref, m_ref, l_ref, acc_ref):
    qi = pl.program_id(1)
    _, nb, tq, _ = q_ref.shape
    n_part = tq // LANES
    row = lax.broadcasted_iota(jnp.int32, (tq, tq), 0)
    col = lax.broadcasted_iota(jnp.int32, (tq, tq), 1)
    units = [(hd, r) for r in range(nb) for hd in range(MLA_HEADS)]

    def scores(hd, r, rows, masked):
        s = _dot_nt(q_ref[hd, r], k_ref[hd, r, rows, :])
        if masked:
            s = jnp.where(col <= row, s, NEG_INF)
        return [s[:, c * LANES:(c + 1) * LANES] for c in range(n_part)]

    def chunk(j, masked):
        rows = pl.ds(pl.multiple_of(j * tq, tq), tq)
        nxt = scores(*units[0], rows, masked)
        for u, (hd, r) in enumerate(units):
            parts = nxt
            if u + 1 < len(units):
                nxt = scores(*units[u + 1], rows, masked)
            m_old = m_ref[u]
            row_max = jnp.max(functools.reduce(jnp.maximum, parts), axis=-1, keepdims=True)
            m_new = jnp.maximum(m_old, row_max)
            p = [jnp.exp2(part - m_new) for part in parts]
            corr = jnp.exp2(m_old - m_new)
            l_ref[u] = corr * l_ref[u] + functools.reduce(jnp.add, p)
            m_ref[u] = m_new
            prob = jnp.concatenate(p, axis=-1).astype(BF16)
            acc_ref[u] = corr[:, :MLA_V] * acc_ref[u] + _dot(prob, v_ref[hd, r, rows, :])

    def body(j, carry):
        chunk(j, False)
        return carry

    m_ref[...] = jnp.full_like(m_ref, NEG_INF)
    l_ref[...] = jnp.zeros_like(l_ref)
    acc_ref[...] = jnp.zeros_like(acc_ref)
    lax.fori_loop(0, qi, body, 0)
    chunk(qi, True)
    for r in range(nb):
        o_ref[r] = jnp.concatenate(
            [acc_ref[r * MLA_HEADS + hd] / jnp.sum(l_ref[r * MLA_HEADS + hd], axis=-1, keepdims=True)
             for hd in range(MLA_HEADS)], axis=-1)


def mla_mixer(h_mla, cos_m, sin_m, bsz, seq, prm):
    t = bsz * seq
    tm = TM_PROJ
    row = lambda w: pl.BlockSpec((tm, w), lambda i: (i, 0))
    hrow = lambda w: pl.BlockSpec((MLA_HEADS, tm, w), lambda i: (0, i, 0))
    q, k, v = pl.pallas_call(
        _mla_prep_kernel,
        grid=(t // tm,),
        in_specs=[row(W_MLA_IN), row(LANES), row(LANES), _const_spec((1, 256)), _const_spec((1, MLA_KV_LORA)),
                  _const_spec((MLA_HEADS, 256, MLA_QK_PAD)), _const_spec((MLA_HEADS, 256, MLA_QK_PAD)),
                  _const_spec((MLA_HEADS, MLA_KV_LORA, MLA_QK_PAD)), _const_spec((MLA_HEADS, MLA_KV_LORA, MLA_V))],
        out_specs=[hrow(MLA_QK_PAD), hrow(MLA_QK_PAD), hrow(MLA_V)],
        out_shape=[jax.ShapeDtypeStruct((MLA_HEADS, t, MLA_QK_PAD), BF16),
                   jax.ShapeDtypeStruct((MLA_HEADS, t, MLA_QK_PAD), BF16),
                   jax.ShapeDtypeStruct((MLA_HEADS, t, MLA_V), BF16)],
        compiler_params=_cparams("parallel"),
        name="mla_prep",
    )(h_mla, cos_m, sin_m, prm["gq"], prm["gkv"], prm["wq"], prm["wqr"], prm["wk"], prm["wv"])

    tq = TQ_ATT
    nb = NB_ATT
    n_unit = nb * MLA_HEADS
    by_row = lambda a: a.reshape(MLA_HEADS, bsz, seq, a.shape[-1])
    q_spec = pl.BlockSpec((MLA_HEADS, nb, tq, MLA_QK_PAD), lambda b, i: (0, b, i, 0))
    k_spec = pl.BlockSpec((MLA_HEADS, nb, seq, MLA_QK_PAD), lambda b, i: (0, b, 0, 0))
    v_spec = pl.BlockSpec((MLA_HEADS, nb, seq, MLA_V), lambda b, i: (0, b, 0, 0))
    y = pl.pallas_call(
        _flash_kernel,
        grid=(bsz // nb, seq // tq),
        in_specs=[q_spec, k_spec, v_spec],
        out_specs=pl.BlockSpec((nb, tq, MLA_HEADS * MLA_V), lambda b, i: (b, i, 0)),
        out_shape=jax.ShapeDtypeStruct((bsz, seq, MLA_HEADS * MLA_V), F32),
        scratch_shapes=[pltpu.VMEM((n_unit, tq, LANES), F32), pltpu.VMEM((n_unit, tq, LANES), F32),
                        pltpu.VMEM((n_unit, tq, MLA_V), F32)],
        compiler_params=_cparams("parallel", "arbitrary"),
        name="mla_flash",
    )(by_row(q), by_row(k), by_row(v))
    return y.reshape(t, MLA_HEADS * MLA_V)


def _rot_half_cols(w, width):
    shp = w.shape
    w = w.reshape(shp[0], -1, 2, width // 2)
    return jnp.stack([-w[:, :, 1], w[:, :, 0]], axis=2).reshape(shp)


def mla_params(q_norm, kv_norm, w_uq, w_ukv):
    dq = MLA_NOPE + MLA_ROPE
    wq = jnp.zeros((MLA_HEADS, 256, MLA_QK_PAD), F32)
    wqr = jnp.zeros((MLA_HEADS, 256, MLA_QK_PAD), F32)
    w_uq_h = w_uq.reshape(MLA_Q_LORA, MLA_HEADS, dq).transpose(1, 0, 2)
    wq = wq.at[:, :MLA_Q_LORA, :dq].set(w_uq_h)
    pe = w_uq_h[:, :, MLA_NOPE:]
    pe_rot = jnp.concatenate([-pe[..., MLA_ROPE // 2:], pe[..., :MLA_ROPE // 2]], axis=-1)
    wqr = wqr.at[:, :MLA_Q_LORA, MLA_NOPE:dq].set(pe_rot)
    w_kv_h = w_ukv.reshape(MLA_KV_LORA, MLA_HEADS, MLA_NOPE + MLA_V).transpose(1, 0, 2)
    wk = jnp.zeros((MLA_HEADS, MLA_KV_LORA, MLA_QK_PAD), F32).at[:, :, :MLA_NOPE].set(w_kv_h[:, :, :MLA_NOPE])
    wv = w_kv_h[:, :, MLA_NOPE:]
    gq = jnp.zeros((1, 256), F32).at[0, :MLA_Q_LORA].set(q_norm)
    return {"wq": wq.astype(BF16), "wqr": wqr.astype(BF16), "wk": wk.astype(BF16), "wv": wv.astype(BF16),
            "gq": gq, "gkv": kv_norm.reshape(1, MLA_KV_LORA)}


def _head_mask(shape, hd, axis):
    return (lax.broadcasted_iota(jnp.int32, shape, axis) // HEAD_DIM) == hd


def _hgrn_kernel(h_ref, lb_ref, ng_ref, aall_ref, lvl_ref, hm_ref, blk_ref, o_ref, st_ref, e_ref):
    @pl.when(pl.program_id(1) == 0)
    def _():
        st_ref[...] = jnp.zeros_like(st_ref)

    for row in range(h_ref.shape[0]):
        _hgrn_tile(h_ref.at[row], lb_ref, ng_ref, aall_ref, lvl_ref, hm_ref, blk_ref, o_ref.at[row],
                   st_ref.at[row], e_ref.at[row])


def _hgrn_tile(h_ref, lb_ref, ng_ref, aall_ref, lvl_ref, hm_ref, blk_ref, o_ref, st_ref, e_ref):
    tt = h_ref.shape[0]
    width = HEADS * HEAD_DIM
    n_lev = aall_ref.shape[0] // tt - 1

    q = h_ref[:, 0:width]
    f = h_ref[:, width:2 * width]
    v = h_ref[:, 2 * width:3 * width]
    g = h_ref[:, 3 * width:4 * width]
    lb = lb_ref[...]
    log_sig = jnp.minimum(f, 0.0) - jnp.log1p(jnp.exp(-jnp.abs(f)))
    a = jnp.log(lb)
    b = jnp.log1p(-lb) + log_sig
    log_f = jnp.maximum(a, b) + jnp.log1p(jnp.exp(-jnp.abs(a - b)))
    k = (1.0 - lb) * _sigmoid(-f)
    qf = q * _sigmoid(q)
    e2 = _dot(aall_ref[...], jnp.concatenate(_split2(log_f), axis=-1))
    e_ref[...] = e2[:, 0:width] + e2[:, width:2 * width]

    t_row = lax.broadcasted_iota(jnp.int32, (tt, width), 0)
    lvl = lvl_ref[...]
    attn = [jnp.zeros((tt, tt), F32) for _ in range(HEADS)]
    for lev in range(n_lev):
        x = jnp.exp(e_ref[lev * tt:(lev + 1) * tt, :])
        right = ((t_row >> lev) & 1) == 1
        qt = jnp.where(right, qf * x, 0.0).astype(BF16)
        kt = jnp.where(right, 0.0, k * x).astype(BF16)
        q_heads = jnp.concatenate([qt * hm_ref[hd] for hd in range(HEADS)], axis=0)
        s = _dot_nt(q_heads, kt)
        here = lvl == lev
        for hd in range(HEADS):
            attn[hd] = jnp.where(here, s[hd * tt:(hd + 1) * tt, :], attn[hd])
    vb = v.astype(BF16)
    o = jnp.zeros((tt, width), F32)
    for hd in range(HEADS):
        o = o + _dot(attn[hd].astype(BF16), vb * hm_ref[hd])
    ones_blk = blk_ref[...]
    o = o + _dot_exact_rhs(qf * k, ones_blk) * v
    bc = e_ref[n_lev * tt:(n_lev + 1) * tt, :]
    st = st_ref[...]
    st_m = jnp.where(ones_blk > 0, st, 0.0).astype(BF16)
    o = o + _dot_nt((qf * jnp.exp(bc)).astype(BF16), st_m)
    b_last = bc[tt - 1:tt, :]
    k_out = (k * jnp.exp(b_last - bc)).astype(BF16)
    st_ref[...] = st * jnp.exp(b_last) + _dot_tn(vb, k_out)
    ms = _dot_exact_rhs(o * o, ones_blk) * (1.0 / HEAD_DIM)
    o = o * lax.rsqrt(ms + EPS) * ng_ref[...]
    o_ref[...] = o * (g * _sigmoid(g))


def _hgrn_level_matrices(tt):
    n_lev = int(math.log2(tt))
    t = np.arange(tt)[:, None]
    j = np.arange(tt)[None, :]
    mats = []
    for lev in range(n_lev):
        half = 1 << lev
        mid = (t >> (lev + 1) << (lev + 1)) + half
        right = ((t >> lev) & 1) == 1
        mats.append(np.where(right, (j >= mid) & (j <= t), (j > t) & (j < mid)).astype(np.float32))
    mats.append((j <= t).astype(np.float32))
    level = np.full((tt, tt), -1, np.int32)
    lower = np.broadcast_to(j < t, (tt, tt))
    level[lower] = np.floor(np.log2((t ^ j)[lower])).astype(np.int32)
    return jnp.asarray(np.concatenate(mats, axis=0), BF16), jnp.asarray(level)


def _head_block_ones():
    i = np.arange(HEADS * HEAD_DIM)
    return jnp.asarray((i[:, None] // HEAD_DIM == i[None, :] // HEAD_DIM).astype(np.float32), BF16)


def _head_lane_masks(rows):
    i = np.arange(HEADS * HEAD_DIM)
    m = (i[None, :] // HEAD_DIM == np.arange(HEADS)[:, None]).astype(np.float32)
    return jnp.asarray(np.broadcast_to(m[:, None, :], (HEADS, rows, HEADS * HEAD_DIM)), BF16)


def hgrn2_mixer(h_hg, lb, norm_g, bsz, seq):
    t = bsz * seq
    tt = TT_HG
    width = HEADS * HEAD_DIM
    aall, level = _hgrn_level_matrices(tt)
    nt = seq // tt
    nb = NB_HG
    y = pl.pallas_call(
        _hgrn_kernel,
        grid=(bsz // nb, nt),
        in_specs=[pl.BlockSpec((nb, tt, 4 * width), lambda b, i: (b, i, 0)),
                  _const_spec((1, width)), _const_spec((1, width)),
                  _const_spec(aall.shape), _const_spec(level.shape), _const_spec((HEADS, tt, width)),
                  _const_spec((width, width))],
        out_specs=pl.BlockSpec((nb, tt, width), lambda b, i: (b, i, 0)),
        out_shape=jax.ShapeDtypeStruct((bsz, seq, width), F32),
        scratch_shapes=[pltpu.VMEM((nb, width, width), F32), pltpu.VMEM((nb, aall.shape[0], width), F32)],
        compiler_params=_cparams("parallel", "arbitrary"),
        name="hgrn2_mixer",
    )(h_hg.reshape(bsz, seq, 4 * width), lb.reshape(1, width), norm_g.reshape(1, width), aall, level,
      _head_lane_masks(tt), _head_block_ones())
    return y.reshape(t, width)


def _ret_kernel(h_ref, cos_ref, sin_ref, dmat_ref, qdec_ref, kdec_ref, cdec_ref, blk_ref, g_ref, b_ref,
                o_ref, st_ref):
    @pl.when(pl.program_id(1) == 0)
    def _():
        st_ref[...] = jnp.zeros_like(st_ref)

    for row in range(h_ref.shape[0]):
        _ret_tile(h_ref.at[row], cos_ref.at[row], sin_ref.at[row], dmat_ref, qdec_ref, kdec_ref, cdec_ref,
                  blk_ref, g_ref, b_ref, o_ref.at[row], st_ref.at[row])


def _ret_tile(h_ref, cos_ref, sin_ref, dmat_ref, qdec_ref, kdec_ref, cdec_ref, blk_ref, g_ref, b_ref,
              o_ref, st_ref):
    tt = h_ref.shape[0]
    width = HEADS * HEAD_DIM

    cos = cos_ref[...]
    sin = sin_ref[...]
    q = (h_ref[:, 0:width] * cos + h_ref[:, 4 * width:5 * width] * sin) * (HEAD_DIM ** -0.5)
    k = h_ref[:, width:2 * width] * cos + h_ref[:, 5 * width:6 * width] * sin
    v = h_ref[:, 2 * width:3 * width]
    g = h_ref[:, 3 * width:4 * width]
    kb = k.astype(BF16)
    vb = v.astype(BF16)
    ones_blk = blk_ref[...]
    o = jnp.zeros((tt, width), F32)
    for hd in range(HEADS):
        hm = _head_mask((tt, width), hd, 1)
        s = _dot_nt(jnp.where(hm, q, 0.0).astype(BF16), kb) * dmat_ref[hd]
        o = o + _dot(s.astype(BF16), jnp.where(hm, vb, jnp.zeros_like(vb)))
    st = st_ref[...]
    st_m = jnp.where(ones_blk > 0, st, 0.0).astype(BF16)
    o = o + _dot((q * qdec_ref[...]).astype(BF16), st_m)
    st_ref[...] = st * cdec_ref[...] + _dot_tn((k * kdec_ref[...]).astype(BF16), vb)
    mu = _dot_exact_rhs(o, ones_blk) * (1.0 / HEAD_DIM)
    oc = o - mu
    var = _dot_exact_rhs(oc * oc, ones_blk) * (1.0 / HEAD_DIM)
    o = oc * lax.rsqrt(var + EPS) * g_ref[...] + b_ref[...]
    o_ref[...] = (g * _sigmoid(g)) * o


def _retention_tables(tt):
    log_gamma = jnp.log(1.0 - 2.0 ** (-5.0 - jnp.arange(HEADS, dtype=F32)))
    t_idx = jnp.arange(tt, dtype=F32)
    rel = t_idx[:, None] - t_idx[None, :]
    dmat = jnp.where(rel >= 0, jnp.exp(jnp.maximum(rel, 0.0)[None] * log_gamma[:, None, None]), 0.0)
    lanes = lambda a: jnp.repeat(a, HEAD_DIM, axis=-1)
    qdec = lanes(jnp.exp((t_idx + 1.0)[:, None] * log_gamma[None, :]))
    kdec = lanes(jnp.exp((tt - 1 - t_idx)[:, None] * log_gamma[None, :]))
    cdec = lanes(jnp.exp(tt * log_gamma)[None, :])
    return dmat, qdec, kdec, cdec


def retention_mixer(h_ret, cos_r, sin_r, gn_g, gn_b, bsz, seq):
    t = bsz * seq
    tt = TT_RET
    width = HEADS * HEAD_DIM
    nt = seq // tt
    dmat, qdec, kdec, cdec = _retention_tables(tt)
    nb = NB_RET
    row = lambda w: pl.BlockSpec((nb, tt, w), lambda b, i: (b, i, 0))
    y = pl.pallas_call(
        _ret_kernel,
        grid=(bsz // nb, nt),
        in_specs=[row(W_RET_IN), row(width), row(width), _const_spec((HEADS, tt, tt)), _const_spec((tt, width)),
                  _const_spec((tt, width)), _const_spec((1, width)), _const_spec((width, width)),
                  _const_spec((1, width)), _const_spec((1, width))],
        out_specs=row(width),
        out_shape=jax.ShapeDtypeStruct((bsz, seq, width), F32),
        scratch_shapes=[pltpu.VMEM((nb, width, width), F32)],
        compiler_params=_cparams("parallel", "arbitrary"),
        name="retention_mixer",
    )(h_ret.reshape(bsz, seq, W_RET_IN), cos_r.reshape(bsz, seq, width), sin_r.reshape(bsz, seq, width),
      dmat, qdec, kdec, cdec, _head_block_ones(), gn_g.reshape(1, width), gn_b.reshape(1, width))
    return y.reshape(t, width)


def _store_token_tiles(ref, val):
    tm = val.shape[0]
    for j in range(D_MODEL // LANES):
        ref[pl.ds(j, tm, stride=SUBLANES), :] = val[:, j * LANES:(j + 1) * LANES]


def _load_token_tiles(ref, tm):
    return jnp.concatenate([ref[pl.ds(j, tm, stride=SUBLANES), :] for j in range(D_MODEL // LANES)], axis=-1)


def _mix_kernel(x_ref, ya_ref, yb_ref, yc_ref, yd_ref, wg_ref, wb_ref, wo_ref, g_ref, b_ref, o_ref):
    x = x_ref[...]
    xb = x.astype(BF16)
    acc = jnp.zeros(x.shape, F32)
    for n, y_ref in enumerate((ya_ref, yb_ref, yc_ref, yd_ref)):
        gate = _sigmoid(_dot(xb, wg_ref[:, n * D_MODEL:(n + 1) * D_MODEL]))
        acc = acc + gate * _dot(y_ref[...].astype(BF16), wb_ref[n])
    z = ALPHA * x + _dot(acc.astype(BF16), wo_ref[...])
    o_ref[...] = _layer_norm_rows(z, g_ref[...], b_ref[...])


def mix_branches(x, ys, w_gate, w_branch, w_o, ln_g, ln_b):
    t = x.shape[0]
    tm = TM_PROJ
    row = lambda w: pl.BlockSpec((tm, w), lambda i: (i, 0))
    return pl.pallas_call(
        _mix_kernel,
        grid=(t // tm,),
        in_specs=[row(D_MODEL)] + [row(MIX_WIDTH)] * 4 +
                 [_const_spec((D_MODEL, N_BRANCH * D_MODEL)), _const_spec((N_BRANCH, MIX_WIDTH, D_MODEL)),
                  _const_spec((D_MODEL, D_MODEL)), _const_spec((1, D_MODEL)), _const_spec((1, D_MODEL))],
        out_specs=row(D_MODEL),
        out_shape=jax.ShapeDtypeStruct((t, D_MODEL), F32),
        compiler_params=_cparams("parallel"),
        name="mix_branches",
    )(x, *ys, w_gate, w_branch, w_o, ln_g.reshape(1, D_MODEL), ln_b.reshape(1, D_MODEL))


def _ffn_kernel(x_ref, wg_ref, wu_ref, wd_ref, o_ref, xb_ref, acc_ref):
    c = pl.program_id(1)

    @pl.when(c == 0)
    def _():
        xb_ref[...] = x_ref[...].astype(BF16)
        acc_ref[...] = jnp.zeros_like(acc_ref)

    xb = xb_ref[...]
    gate = _dot(xb, wg_ref[...])
    act = gate * _sigmoid(gate) * _dot(xb, wu_ref[...])
    acc_ref[...] += _dot(act.astype(BF16), wd_ref[...])

    @pl.when(c == pl.num_programs(1) - 1)
    def _():
        o_ref[...] = acc_ref[...]


def dense_ffn(x, w_gate, w_up, w_down):
    t = x.shape[0]
    tm, tf = TM_FFN, TF_FFN
    return pl.pallas_call(
        _ffn_kernel,
        grid=(t // tm, D_FF // tf),
        in_specs=[pl.BlockSpec((tm, D_MODEL), lambda i, c: (i, 0)),
                  pl.BlockSpec((D_MODEL, tf), lambda i, c: (0, c)),
                  pl.BlockSpec((D_MODEL, tf), lambda i, c: (0, c)),
                  pl.BlockSpec((tf, D_MODEL), lambda i, c: (c, 0))],
        out_specs=pl.BlockSpec((tm, D_MODEL), lambda i, c: (i, 0)),
        out_shape=jax.ShapeDtypeStruct((t, D_MODEL), F32),
        scratch_shapes=[pltpu.VMEM((tm, D_MODEL), BF16), pltpu.VMEM((tm, D_MODEL), F32)],
        compiler_params=_cparams("parallel", "arbitrary"),
        name="dense_ffn",
    )(x, w_gate, w_up, w_down)


def _router_kernel(x_ref, wh_ref, wl_ref, upper_ref, meta_ref, cnt_ref, carry_ref):
    tm = x_ref.shape[0]

    @pl.when(pl.program_id(0) == 0)
    def _():
        carry_ref[...] = jnp.zeros_like(carry_ref)

    xh, xl = _split2(x_ref[...])
    wh = wh_ref[...]
    logits = _dot_nt(wh, xh) + _dot_nt(wh, xl) + _dot_nt(wl_ref[...], xh)
    e_idx = lax.broadcasted_iota(jnp.int32, logits.shape, 0)
    m1 = jnp.max(logits, axis=0, keepdims=True)
    i1 = jnp.min(jnp.where(logits == m1, e_idx, N_EXPERTS), axis=0, keepdims=True)
    rest = jnp.where(e_idx == i1, -jnp.inf, logits)
    m2 = jnp.max(rest, axis=0, keepdims=True)
    i2 = jnp.min(jnp.where(rest == m2, e_idx, N_EXPERTS), axis=0, keepdims=True)
    ex = jnp.exp(m2 - m1)
    w1 = 1.0 / (1.0 + ex)
    w2 = ex / (1.0 + ex)
    sel1 = e_idx == i1
    sel2 = e_idx == i2
    onehot = jnp.where(sel1 | sel2, 1.0, 0.0)
    carry = carry_ref[...]
    ranks = _dot(onehot.astype(BF16), upper_ref[...]) + carry[:, 0:1]
    r1 = jnp.sum(jnp.where(sel1, ranks, 0.0), axis=0, keepdims=True)
    r2 = jnp.sum(jnp.where(sel2, ranks, 0.0), axis=0, keepdims=True)
    row = lax.broadcasted_iota(jnp.int32, (SUBLANES, tm), 0)
    vals = (i1.astype(F32), i2.astype(F32), w1, w2, r1, r2)
    meta = jnp.zeros((SUBLANES, tm), F32)
    for n, val in enumerate(vals):
        meta = jnp.where(row == n, val, meta)
    meta_ref[...] = meta
    carry = carry + jnp.sum(onehot, axis=1, keepdims=True)
    carry_ref[...] = carry
    cnt_ref[...] = carry


def moe_router(x, w_router):
    t = x.shape[0]
    tm = TM_ROUTE
    wt = w_router.T
    wh = wt.astype(BF16)
    wl = (wt - wh.astype(F32)).astype(BF16)
    upper = jnp.asarray(np.triu(np.ones((tm, tm), np.float32), 1), BF16)
    return pl.pallas_call(
        _router_kernel,
        grid=(t // tm,),
        in_specs=[pl.BlockSpec((tm, D_MODEL), lambda i: (i, 0)), _const_spec((N_EXPERTS, D_MODEL)),
                  _const_spec((N_EXPERTS, D_MODEL)), _const_spec((tm, tm))],
        out_specs=[pl.BlockSpec((SUBLANES, tm), lambda i: (0, i)), pl.BlockSpec((N_EXPERTS, LANES), lambda i: (0, 0))],
        out_shape=[jax.ShapeDtypeStruct((SUBLANES, t), F32), jax.ShapeDtypeStruct((N_EXPERTS, LANES), F32)],
        scratch_shapes=[pltpu.VMEM((N_EXPERTS, LANES), F32)],
        compiler_params=_cparams("arbitrary"),
        name="moe_router",
    )(x, wh, wl, upper)


def _token_rows(ref, tok):
    return ref.at[pl.ds(pl.multiple_of(tok * SUBLANES, SUBLANES), SUBLANES), :]


def _start_then_wait(n, copies):
    def start(r, carry):
        for idx, cp in enumerate(copies(r)):
            cp.start(priority=idx % 2)
        return carry

    def wait(r, carry):
        for cp in copies(r):
            cp.wait()
        return carry

    lax.fori_loop(0, n, start, 0)
    return lambda: lax.fori_loop(0, n, wait, 0)


def _dispatch_kernel(dest_ref, x_ref, init_ref, xs_ref, buf_ref, sem):
    del init_ref
    td = x_ref.shape[0]
    _store_token_tiles(buf_ref, x_ref[...])

    def copies(r):
        src = _token_rows(buf_ref, r)
        return (pltpu.make_async_copy(src, _token_rows(xs_ref, dest_ref[0, 0, 2 * r]), sem),
                pltpu.make_async_copy(src, _token_rows(xs_ref, dest_ref[0, 0, 2 * r + 1]), sem))

    _start_then_wait(td, copies)()


def moe_dispatch(x, dest3, p_len):
    t = x.shape[0]
    td = dest3.shape[2] // 2
    init = jnp.zeros((p_len * SUBLANES, LANES), F32)
    return pl.pallas_call(
        _dispatch_kernel,
        grid=(t // td,),
        in_specs=[pl.BlockSpec((1, 1, 2 * td), lambda i: (i, 0, 0), memory_space=pltpu.SMEM),
                  pl.BlockSpec((td, D_MODEL), lambda i: (i, 0)), pl.BlockSpec(memory_space=pl.ANY)],
        out_specs=pl.BlockSpec(memory_space=pl.ANY),
        out_shape=jax.ShapeDtypeStruct((p_len * SUBLANES, LANES), F32),
        scratch_shapes=[pltpu.VMEM((td * SUBLANES, LANES), F32), pltpu.SemaphoreType.DMA(())],
        input_output_aliases={2: 0},
        compiler_params=pltpu.CompilerParams(dimension_semantics=("arbitrary",), has_side_effects=True),
        name="moe_dispatch",
    )(dest3, x, init)


def _expert_kernel(te_ref, nv_ref, xs_ref, wg_ref, wu_ref, wd_ref, ys_ref, xb_ref, acc_ref):
    i = pl.program_id(0)
    c = pl.program_id(1)
    tm = xb_ref.shape[0]
    valid = i < nv_ref[0]

    @pl.when(valid & (c == 0))
    def _():
        xb_ref[...] = _load_token_tiles(xs_ref, tm).astype(BF16)
        acc_ref[...] = jnp.zeros_like(acc_ref)

    @pl.when(valid)
    def _():
        xb = xb_ref[...]
        gate = _dot(xb, wg_ref[...].astype(BF16))
        act = gate * _sigmoid(gate) * _dot(xb, wu_ref[...].astype(BF16))
        acc_ref[...] += _dot(act.astype(BF16), wd_ref[...].astype(BF16))

    @pl.when(c == pl.num_programs(1) - 1)
    def _():
        _store_token_tiles(ys_ref, jnp.where(valid, acc_ref[...], 0.0))


def moe_experts(xs, tile_expert, n_valid, layer, w_gate, w_up, w_down):
    tm, tf = TM_MOE, TF_FFN
    n_tiles = xs.shape[0] // (tm * SUBLANES)
    nc = D_FF // tf

    def chunk(i, c, nv):
        return jnp.where(i < nv[0], c, nc - 1)

    grid_spec = pltpu.PrefetchScalarGridSpec(
        num_scalar_prefetch=2,
        grid=(n_tiles, nc),
        in_specs=[pl.BlockSpec((tm * SUBLANES, LANES), lambda i, c, te, nv: (jnp.minimum(i, nv[0] - 1), 0)),
                  pl.BlockSpec((None, None, D_MODEL, tf), lambda i, c, te, nv: (layer, te[i], 0, chunk(i, c, nv))),
                  pl.BlockSpec((None, None, D_MODEL, tf), lambda i, c, te, nv: (layer, te[i], 0, chunk(i, c, nv))),
                  pl.BlockSpec((None, None, tf, D_MODEL), lambda i, c, te, nv: (layer, te[i], chunk(i, c, nv), 0))],
        out_specs=pl.BlockSpec((tm * SUBLANES, LANES), lambda i, c, te, nv: (i, 0)),
        scratch_shapes=[pltpu.VMEM((tm, D_MODEL), BF16), pltpu.VMEM((tm, D_MODEL), F32)],
    )
    return pl.pallas_call(
        _expert_kernel,
        grid_spec=grid_spec,
        out_shape=jax.ShapeDtypeStruct(xs.shape, F32),
        compiler_params=_cparams("arbitrary", "arbitrary"),
        name="moe_experts",
    )(tile_expert, n_valid, xs, w_gate, w_up, w_down)


def moe_ffn(x, w_router, layer, w_gate, w_up, w_down):
    t = x.shape[0]
    tm = TM_MOE
    meta, cnt = moe_router(x, w_router)
    counts = cnt[:, 0].astype(jnp.int32)
    padded = (counts + tm - 1) // tm * tm
    pad_end = jnp.cumsum(padded)
    pad_start = pad_end - padded
    experts = meta[0:2].astype(jnp.int32)
    start_of = jnp.sum(jnp.where(experts[..., None] == jnp.arange(N_EXPERTS), pad_start, 0), axis=-1)
    dest = (start_of + meta[4:6].astype(jnp.int32)).T
    dest3 = dest.reshape(t // TD_DISPATCH, 1, 2 * TD_DISPATCH)
    n_tiles = (2 * t) // tm + N_EXPERTS
    n_valid = (pad_end[-1] // tm).astype(jnp.int32).reshape(1)
    tile_start = jnp.arange(n_tiles, dtype=jnp.int32) * tm
    tile_expert = jnp.minimum(jnp.sum(pad_end[None, :] <= tile_start[:, None], axis=1), N_EXPERTS - 1)
    last_expert = jnp.sum(jnp.where(jnp.arange(n_tiles) == n_valid[0] - 1, tile_expert, 0))
    tile_expert = jnp.where(jnp.arange(n_tiles) < n_valid[0], tile_expert, last_expert).astype(jnp.int32)
    xs = moe_dispatch(x, dest3, n_tiles * tm)
    ys = moe_experts(xs, tile_expert, n_valid, layer, w_gate, w_up, w_down)
    return ys, dest3, meta[2:4].T


def _post_kernel(x_ref, p_ref, wpg_ref, wpp_ref, g_ref, b_ref, f_ref, o_ref):
    x = x_ref[...]
    ple = _sigmoid(_dot(x.astype(BF16), wpg_ref[...])) * _dot(p_ref[...].astype(BF16), wpp_ref[...])
    o_ref[...] = _layer_norm_rows(ALPHA * x + f_ref[...] + ple, g_ref[...], b_ref[...])


def _post_moe_kernel(dest_ref, x_ref, p_ref, wpg_ref, wpp_ref, g_ref, b_ref, w_ref, ys_ref, o_ref,
                     buf0_ref, buf1_ref, sem):
    x = x_ref[...]
    tm = x.shape[0]

    def copies(r):
        return (pltpu.make_async_copy(_token_rows(ys_ref, dest_ref[0, 0, 2 * r]), _token_rows(buf0_ref, r), sem),
                pltpu.make_async_copy(_token_rows(ys_ref, dest_ref[0, 0, 2 * r + 1]), _token_rows(buf1_ref, r), sem))

    wait_all = _start_then_wait(tm, copies)
    ple = _sigmoid(_dot(x.astype(BF16), wpg_ref[...])) * _dot(p_ref[...].astype(BF16), wpp_ref[...])
    wait_all()
    w = w_ref[...]
    f = w[:, 0:1] * _load_token_tiles(buf0_ref, tm) + w[:, 1:2] * _load_token_tiles(buf1_ref, tm)
    o_ref[...] = _layer_norm_rows(ALPHA * x + f + ple, g_ref[...], b_ref[...])


def post_layer(x, p, layer, w_ple_gate, w_ple_proj, ln_g, ln_b, f=None, moe=None):
    t = x.shape[0]
    tm = TM_PROJ if moe is None else TD_DISPATCH
    row = lambda w: pl.BlockSpec((tm, w), lambda i: (i, 0))
    in_specs = [row(D_MODEL), pl.BlockSpec((None, tm, PLE_DIM), lambda i: (layer, i, 0)),
                _const_spec((D_MODEL, D_MODEL)), _const_spec((PLE_DIM, D_MODEL)),
                _const_spec((1, D_MODEL)), _const_spec((1, D_MODEL))]
    args = [x, p, w_ple_gate, w_ple_proj, ln_g.reshape(1, D_MODEL), ln_b.reshape(1, D_MODEL)]
    scratch = []
    if moe is None:
        body = _post_kernel
        in_specs += [row(D_MODEL)]
        args += [f]
    else:
        body = _post_moe_kernel
        ys, dest3, w2 = moe
        in_specs = [pl.BlockSpec((1, 1, 2 * tm), lambda i: (i, 0, 0), memory_space=pltpu.SMEM)] + in_specs
        in_specs += [row(2), pl.BlockSpec(memory_space=pl.ANY)]
        args = [dest3] + args + [w2, ys]
        scratch = [pltpu.VMEM((tm * SUBLANES, LANES), F32), pltpu.VMEM((tm * SUBLANES, LANES), F32),
                   pltpu.SemaphoreType.DMA(())]
    return pl.pallas_call(
        body,
        grid=(t // tm,),
        in_specs=in_specs,
        out_specs=row(D_MODEL),
        out_shape=jax.ShapeDtypeStruct((t, D_MODEL), F32),
        scratch_shapes=scratch,
        compiler_params=_cparams("parallel" if moe is None else "arbitrary"),
        name="post_layer",
    )(*args)


def split_in_weights(w):
    sizes = (MIX_WIDTH, MLA_Q_LORA, MLA_KV_LORA, MLA_ROPE) + (HEADS * HEAD_DIM,) * 8 + (N_BRANCH * D_MODEL,)
    parts, start = [], 0
    for n in sizes:
        parts.append(w[:, start:start + n])
        start += n
    u_s5, c_q, c_kv, k_rope, hq, hf, hi, hg, rq, rk, rv, rg, gate = parts
    k_rot = jnp.concatenate([-k_rope[:, MLA_ROPE // 2:], k_rope[:, :MLA_ROPE // 2]], axis=1)
    w_mla = jnp.zeros((D_MODEL, W_MLA_IN), F32)
    w_mla = w_mla.at[:, 0:MLA_Q_LORA].set(c_q).at[:, 256:384].set(c_kv)
    w_mla = w_mla.at[:, 384 + MLA_NOPE:384 + MLA_NOPE + MLA_ROPE].set(k_rope)
    w_mla = w_mla.at[:, 512 + MLA_NOPE:512 + MLA_NOPE + MLA_ROPE].set(k_rot)
    w_hg = jnp.concatenate([hq, hf, hi, hg], axis=1)
    w_ret = jnp.concatenate([rq, rk, rv, rg, _rot_half_cols(rq, HEAD_DIM), _rot_half_cols(rk, HEAD_DIM)], axis=1)
    return [a.astype(BF16) for a in (w_mla, u_s5, w_hg, w_ret)], gate.astype(BF16)


def kernel(x, p, positions, w_in, s5_a_re, s5_a_im, s5_log_dt, s5_b_re, s5_b_im, s5_c_re, s5_c_im, s5_d, s5_w_glu, mla_q_norm, mla_kv_norm, mla_w_uq, mla_w_ukv, hg_lb_raw, hg_norm, ret_gn_g, ret_gn_b, w_branch, w_o, ln1_g, ln1_b, ff_w_gate, ff_w_up, ff_w_down, moe_router, moe_w_gate, moe_w_up, moe_w_down, ple_w_gate, ple_w_proj, ln2_g, ln2_b):
    bsz, seq, _ = x.shape
    t = bsz * seq
    depth = w_in.shape[0]
    x = x.reshape(t, D_MODEL)
    p = p.reshape(depth, t, PLE_DIM)
    cos_m, sin_m, cos_r, sin_r = rope_tables(positions)
    lb_all = jnp.cumsum(jax.nn.softmax(hg_lb_raw.astype(F32), axis=0), axis=0)
    lb_all = lb_all - lb_all[0]
    for i in range(depth):
        in_ws, w_gate = split_in_weights(w_in[i])
        h_mla, h_s5, h_hg, h_ret = in_projection(x, in_ws)
        y_a = s5_mixer(h_s5, bsz, seq, s5_params(s5_a_re[i], s5_a_im[i], s5_log_dt[i], s5_b_re[i], s5_b_im[i],
                                                s5_c_re[i], s5_c_im[i], s5_d[i], s5_w_glu[i]))
        y_b = mla_mixer(h_mla, cos_m, sin_m, bsz, seq,
                        mla_params(mla_q_norm[i], mla_kv_norm[i], mla_w_uq[i], mla_w_ukv[i]))
        y_c = hgrn2_mixer(h_hg, lb_all[i], hg_norm[i], bsz, seq)
        y_d = retention_mixer(h_ret, cos_r, sin_r, ret_gn_g[i], ret_gn_b[i], bsz, seq)
        x = mix_branches(x, (y_a, y_b, y_c, y_d), w_gate, w_branch[i].astype(BF16), w_o[i].astype(BF16),
                         ln1_g[i], ln1_b[i])
        ple_args = (p, i, ple_w_gate[i].astype(BF16), ple_w_proj[i].astype(BF16), ln2_g[i], ln2_b[i])
        if i % 2 == 1:
            x = post_layer(x, *ple_args, moe=moe_ffn(x, moe_router[i // 2], i // 2, moe_w_gate, moe_w_up,
                                                     moe_w_down))
        else:
            f = dense_ffn(x, ff_w_gate[i // 2].astype(BF16), ff_w_up[i // 2].astype(BF16),
                          ff_w_down[i // 2].astype(BF16))
            x = post_layer(x, *ple_args, f=f)
    return x.reshape(bsz, seq, D_MODEL)
```

```python
import functools
import math

import numpy as np
import jax
import jax.numpy as jnp
from jax import lax
from jax.experimental import pallas as pl
from jax.experimental.pallas import tpu as pltpu

F32 = jnp.float32
BF16 = jnp.bfloat16

D_MODEL = 1024
DEPTH = 4
PLE_DIM = 256
MIX_WIDTH = 256
N_BRANCH = 4
S5_GROUP = 16
S5_GROUPS = 16
S5_STATE = 64
MLA_HEADS = 4
MLA_NOPE = 64
MLA_ROPE = 32
MLA_V = 64
MLA_Q_LORA = 192
MLA_KV_LORA = 128
HEADS = 4
HEAD_DIM = 64
D_FF = 3584
N_EXPERTS = 8
ROPE_BASE = 10000.0
EPS = 1e-5
NEG_INF = -1e30
ALPHA = (2 * DEPTH) ** 0.25

LANES = 128
SUBLANES = 8
VMEM_LIMIT = 56 * 1024 * 1024

TM_PROJ = 512
TS_S5 = 64
TQ_ATT = 256
NB_ATT = 2
TT_HG = 128
NB_HG = 4
TT_RET = 256
NB_RET = 4
TM_FFN = 1024
TF_FFN = 512
TM_MOE = 1024
TM_ROUTE = 512
TD_DISPATCH = 512
DMA_UNROLL = 8


def _cparams(*sem):
    return pltpu.CompilerParams(dimension_semantics=sem, vmem_limit_bytes=VMEM_LIMIT)


def _const_spec(shape):
    nd = len(shape)
    return pl.BlockSpec(shape, lambda *_: (0,) * nd, pipeline_mode=pl.Buffered(1))


def _layer_spec(arr, layer):
    shape = arr.shape[1:]
    nd = len(shape)
    return pl.BlockSpec((None,) + shape, lambda *_: (layer,) + (0,) * nd, pipeline_mode=pl.Buffered(1))


def _dot(a, b):
    return jnp.dot(a, b, preferred_element_type=F32)


def _dot_nt(a, b):
    return lax.dot_general(a, b, (((1,), (1,)), ((), ())), preferred_element_type=F32)


def _dot_tn(a, b):
    return lax.dot_general(a, b, (((0,), (0,)), ((), ())), preferred_element_type=F32)


def _split2(x):
    hi = x.astype(BF16)
    lo = (x - hi.astype(F32)).astype(BF16)
    return hi, lo


def _split3(x):
    h1 = x.astype(BF16)
    r1 = x - h1.astype(F32)
    h2 = r1.astype(BF16)
    h3 = (r1 - h2.astype(F32)).astype(BF16)
    return h1, h2, h3


def _dot_exact_rhs(x, m):
    hi, lo = _split2(x)
    return _dot(hi, m) + _dot(lo, m)


def _dot_exact_lhs(m, x):
    hi, lo = _split2(x)
    return _dot(m, hi) + _dot(m, lo)


def _sigmoid(x):
    return 1.0 / (1.0 + jnp.exp(-x))


def _layer_norm_rows(z, g, b):
    mu = jnp.mean(z, axis=-1, keepdims=True)
    zc = z - mu
    var = jnp.mean(zc * zc, axis=-1, keepdims=True)
    return zc * lax.rsqrt(var + EPS) * g + b


def _rope_kernel(pos_ref, fm_ref, fr_ref, cm_ref, sm_ref, cr_ref, sr_ref):
    pos = pos_ref[...]
    ang_m = pos * fm_ref[...]
    lane = lax.broadcasted_iota(jnp.int32, ang_m.shape, 1)
    in_pe = (lane >= MLA_NOPE) & (lane < MLA_NOPE + MLA_ROPE)
    cm_ref[...] = jnp.where(lane < MLA_NOPE, 1.0, jnp.where(in_pe, jnp.cos(ang_m), 0.0))
    sm_ref[...] = jnp.where(in_pe, jnp.sin(ang_m), 0.0)
    ang_r = pos * fr_ref[...]
    cr_ref[...] = jnp.cos(ang_r)
    sr_ref[...] = jnp.sin(ang_r)


def rope_tables(positions):
    t = positions.size
    tm = TM_PROJ
    pos = positions.reshape(t, 1).astype(F32)
    inv_m = ROPE_BASE ** (-jnp.arange(MLA_ROPE // 2, dtype=F32) / (MLA_ROPE // 2))
    inv_r = ROPE_BASE ** (-jnp.arange(HEAD_DIM // 2, dtype=F32) / (HEAD_DIM // 2))
    fm = jnp.zeros((1, LANES), F32).at[0, MLA_NOPE:MLA_NOPE + MLA_ROPE].set(jnp.concatenate([inv_m, inv_m]))
    fr = jnp.tile(jnp.concatenate([inv_r, inv_r]), HEADS).reshape(1, HEADS * HEAD_DIM)
    row = lambda w: pl.BlockSpec((tm, w), lambda i: (i, 0))
    return pl.pallas_call(
        _rope_kernel,
        grid=(t // tm,),
        in_specs=[row(1), _const_spec((1, LANES)), _const_spec((1, 256))],
        out_specs=[row(LANES), row(LANES), row(256), row(256)],
        out_shape=[jax.ShapeDtypeStruct((t, LANES), F32), jax.ShapeDtypeStruct((t, LANES), F32),
                   jax.ShapeDtypeStruct((t, 256), F32), jax.ShapeDtypeStruct((t, 256), F32)],
        compiler_params=_cparams("parallel"),
        name="rope_tables",
    )(pos, fm, fr)


W_MLA_IN = 640
W_RET_IN = 1536
_N_CHUNK = 256


def _inproj_kernel(x_ref, *refs):
    n = len(refs) // 2
    xb = x_ref[...].astype(BF16)
    for w_ref, o_ref in zip(refs[:n], refs[n:]):
        width = w_ref.shape[1]
        for c0 in range(0, width, _N_CHUNK):
            c1 = min(c0 + _N_CHUNK, width)
            o_ref[:, c0:c1] = _dot(xb, w_ref[:, c0:c1])


def in_projection(x, weights, layer):
    t = x.shape[0]
    tm = TM_PROJ
    return pl.pallas_call(
        _inproj_kernel,
        grid=(t // tm,),
        in_specs=[pl.BlockSpec((tm, D_MODEL), lambda i: (i, 0))] + [_layer_spec(w, layer) for w in weights],
        out_specs=[pl.BlockSpec((tm, w.shape[2]), lambda i: (i, 0)) for w in weights],
        out_shape=[jax.ShapeDtypeStruct((t, w.shape[2]), F32) for w in weights],
        compiler_params=_cparams("parallel"),
        name="in_projection",
    )(x, *weights)


S5_HALF = S5_GROUPS * S5_STATE


def _s5_kernel(u_ref, perm_ref, permt_ref, bmat_ref, ar_ref, ai_ref, cmat_ref, d_ref, wglu_ref, y_ref,
               bu_ref, hs_ref, h_ref):
    nb, ts, width = u_ref.shape
    n_half = S5_HALF // LANES

    @pl.when(pl.program_id(0) == 0)
    def _():
        h_ref[...] = jnp.zeros_like(h_ref)

    u = u_ref[...].reshape(nb * ts, width)
    u_tm = _dot(perm_ref[...], u.astype(BF16)).astype(BF16)
    bu_ref[...] = _dot(u_tm, bmat_ref[...])
    ar = [jnp.broadcast_to(ar_ref[:, c * LANES:(c + 1) * LANES], (nb, LANES)) for c in range(n_half)]
    ai = [jnp.broadcast_to(ai_ref[:, c * LANES:(c + 1) * LANES], (nb, LANES)) for c in range(n_half)]
    h_re = [h_ref[:, c * LANES:(c + 1) * LANES] for c in range(n_half)]
    h_im = [h_ref[:, S5_HALF + c * LANES:S5_HALF + (c + 1) * LANES] for c in range(n_half)]
    for n in range(ts):
        rows = slice(n * nb, (n + 1) * nb)
        for c in range(n_half):
            re_cols = slice(c * LANES, (c + 1) * LANES)
            im_cols = slice(S5_HALF + c * LANES, S5_HALF + (c + 1) * LANES)
            new_re = ar[c] * h_re[c] - ai[c] * h_im[c] + bu_ref[rows, re_cols]
            new_im = ar[c] * h_im[c] + ai[c] * h_re[c] + bu_ref[rows, im_cols]
            h_re[c], h_im[c] = new_re, new_im
            hs_ref[rows, re_cols] = new_re
            hs_ref[rows, im_cols] = new_im
    for c in range(n_half):
        h_ref[:, c * LANES:(c + 1) * LANES] = h_re[c]
        h_ref[:, S5_HALF + c * LANES:S5_HALF + (c + 1) * LANES] = h_im[c]
    y_tm = _dot(hs_ref[...].astype(BF16), cmat_ref[...])
    y = _dot_exact_lhs(permt_ref[...], y_tm) + d_ref[...] * u
    y = jax.nn.gelu(y)
    y = y * _sigmoid(_dot(y.astype(BF16), wglu_ref[...]))
    y_ref[...] = y.reshape(nb, ts, width)


def _step_major_permutation(nb, ts):
    p = np.zeros((nb * ts, nb * ts), np.float32)
    b, n = np.meshgrid(np.arange(nb), np.arange(ts), indexing="ij")
    p[(n * nb + b).ravel(), (b * ts + n).ravel()] = 1.0
    return jnp.asarray(p, BF16), jnp.asarray(p.T, BF16)


def s5_mixer(u, bsz, seq, prm, layer):
    ts = TS_S5
    u3 = u.reshape(bsz, seq, MIX_WIDTH)
    blk = pl.BlockSpec((bsz, ts, MIX_WIDTH), lambda i: (0, i, 0))
    perm, perm_t = _step_major_permutation(bsz, ts)
    names = ("bmat", "a_re", "a_im", "cmat", "d", "w_glu")
    y = pl.pallas_call(
        _s5_kernel,
        grid=(seq // ts,),
        in_specs=[blk, _const_spec(perm.shape), _const_spec(perm.shape)] +
                 [_layer_spec(prm[n], layer) for n in names],
        out_specs=blk,
        out_shape=jax.ShapeDtypeStruct((bsz, seq, MIX_WIDTH), F32),
        scratch_shapes=[pltpu.VMEM((bsz * ts, 2 * S5_HALF), F32), pltpu.VMEM((bsz * ts, 2 * S5_HALF), F32),
                        pltpu.VMEM((bsz, 2 * S5_HALF), F32)],
        compiler_params=_cparams("arbitrary"),
        name="s5_mixer",
    )(u3, perm, perm_t, *[prm[n] for n in names])
    return y.reshape(bsz * seq, MIX_WIDTH)


def s5_params(a_re, a_im, log_dt, b_re, b_im, c_re, c_im, d_skip, w_glu):
    dt = jnp.exp(log_dt)[:, None]
    mag = jnp.exp(a_re * dt)
    abar_re, abar_im = mag * jnp.cos(a_im * dt), mag * jnp.sin(a_im * dt)
    den = a_re * a_re + a_im * a_im
    num_re, num_im = abar_re - 1.0, abar_im
    coef_re = (num_re * a_re + num_im * a_im) / den
    coef_im = (num_im * a_re - num_re * a_im) / den
    bbar_re = coef_re[..., None] * b_re - coef_im[..., None] * b_im
    bbar_im = coef_re[..., None] * b_im + coef_im[..., None] * b_re
    eye = jnp.eye(S5_GROUPS, dtype=F32)
    b_bd_re = jnp.einsum("gpc,gh->gchp", bbar_re, eye).reshape(MIX_WIDTH, S5_HALF)
    b_bd_im = jnp.einsum("gpc,gh->gchp", bbar_im, eye).reshape(MIX_WIDTH, S5_HALF)
    c_bd_re = jnp.einsum("gcp,gh->gphc", c_re, eye).reshape(S5_HALF, MIX_WIDTH)
    c_bd_im = jnp.einsum("gcp,gh->gphc", c_im, eye).reshape(S5_HALF, MIX_WIDTH)
    return {
        "bmat": jnp.concatenate([b_bd_re, b_bd_im], axis=1).astype(BF16),
        "cmat": jnp.concatenate([c_bd_re, -c_bd_im], axis=0).astype(BF16),
        "a_re": abar_re.reshape(1, S5_HALF), "a_im": abar_im.reshape(1, S5_HALF),
        "d": d_skip.reshape(1, MIX_WIDTH), "w_glu": w_glu.astype(BF16),
    }


MLA_QK_PAD = 128


def _mla_prep_kernel(h_ref, cos_ref, sin_ref, gq_ref, gkv_ref, wq_ref, wqr_ref, wk_ref, wv_ref,
                     q_ref, k_ref, v_ref):
    h = h_ref[...]
    cos = cos_ref[...]
    sin = sin_ref[...]
    c_q = h[:, 0:256]
    c_kv = h[:, 256:384]
    k_pe = h[:, 384:512] * cos + h[:, 512:640] * sin
    qn = c_q * lax.rsqrt(jnp.sum(c_q * c_q, axis=-1, keepdims=True) / MLA_Q_LORA + EPS) * gq_ref[...]
    kvn = c_kv * lax.rsqrt(jnp.mean(c_kv * c_kv, axis=-1, keepdims=True) + EPS) * gkv_ref[...]
    qn = qn.astype(BF16)
    kvn = kvn.astype(BF16)
    scale = (MLA_NOPE + MLA_ROPE) ** -0.5 * math.log2(math.e)
    for hd in range(MLA_HEADS):
        q = _dot(qn, wq_ref[hd]) * cos + _dot(qn, wqr_ref[hd]) * sin
        q_ref[hd] = (q * scale).astype(BF16)
        k_ref[hd] = (_dot(kvn, wk_ref[hd]) + k_pe).astype(BF16)
        v_ref[hd] = _dot(kvn, wv_ref[hd]).astype(BF16)


def _flash_kernel(q_ref, k_ref, v_ref, o_ref, m_ref, l_ref, acc_ref):
    qi = pl.program_id(1)
    _, nb, tq, _ = q_ref.shape
    n_part = tq // LANES
    row = lax.broadcasted_iota(jnp.int32, (tq, tq), 0)
    col = lax.broadcasted_iota(jnp.int32, (tq, tq), 1)
    units = [(hd, r) for r in range(nb) for hd in range(MLA_HEADS)]

    def scores(hd, r, rows, masked):
        s = _dot_nt(q_ref[hd, r], k_ref[hd, r, rows, :])
        if masked:
            s = jnp.where(col <= row, s, NEG_INF)
        return [s[:, c * LANES:(c + 1) * LANES] for c in range(n_part)]

    def chunk(j, masked):
        rows = pl.ds(pl.multiple_of(j * tq, tq), tq)
        nxt = scores(*units[0], rows, masked)
        for u, (hd, r) in enumerate(units):
            parts = nxt
            if u + 1 < len(units):
                nxt = scores(*units[u + 1], rows, masked)
            m_old = m_ref[u]
            row_max = jnp.max(functools.reduce(jnp.maximum, parts), axis=-1, keepdims=True)
            m_new = jnp.maximum(m_old, row_max)
            p = [jnp.exp2(part - m_new) for part in parts]
            corr = jnp.exp2(m_old - m_new)
            l_ref[u] = corr * l_ref[u] + functools.reduce(jnp.add, p)
            m_ref[u] = m_new
            prob = jnp.concatenate(p, axis=-1).astype(BF16)
            acc_ref[u] = corr[:, :MLA_V] * acc_ref[u] + _dot(prob, v_ref[hd, r, rows, :])

    def body(j, carry):
        chunk(j, False)
        return carry

    m_ref[...] = jnp.full_like(m_ref, NEG_INF)
    l_ref[...] = jnp.zeros_like(l_ref)
    acc_ref[...] = jnp.zeros_like(acc_ref)
    lax.fori_loop(0, qi, body, 0)
    chunk(qi, True)
    for r in range(nb):
        o_ref[r] = jnp.concatenate(
            [acc_ref[r * MLA_HEADS + hd] / jnp.sum(l_ref[r * MLA_HEADS + hd], axis=-1, keepdims=True)
             for hd in range(MLA_HEADS)], axis=-1)


def mla_mixer(h_mla, cos_m, sin_m, bsz, seq, prm, layer):
    t = bsz * seq
    tm = TM_PROJ
    row = lambda w: pl.BlockSpec((tm, w), lambda i: (i, 0))
    hrow = lambda w: pl.BlockSpec((MLA_HEADS, tm, w), lambda i: (0, i, 0))
    names = ("gq", "gkv", "wq", "wqr", "wk", "wv")
    q, k, v = pl.pallas_call(
        _mla_prep_kernel,
        grid=(t // tm,),
        in_specs=[row(W_MLA_IN), row(LANES), row(LANES)] + [_layer_spec(prm[n], layer) for n in names],
        out_specs=[hrow(MLA_QK_PAD), hrow(MLA_QK_PAD), hrow(MLA_V)],
        out_shape=[jax.ShapeDtypeStruct((MLA_HEADS, t, MLA_QK_PAD), BF16),
                   jax.ShapeDtypeStruct((MLA_HEADS, t, MLA_QK_PAD), BF16),
                   jax.ShapeDtypeStruct((MLA_HEADS, t, MLA_V), BF16)],
        compiler_params=_cparams("parallel"),
        name="mla_prep",
    )(h_mla, cos_m, sin_m, *[prm[n] for n in names])

    tq = TQ_ATT
    nb = NB_ATT
    n_unit = nb * MLA_HEADS
    by_row = lambda a: a.reshape(MLA_HEADS, bsz, seq, a.shape[-1])
    q_spec = pl.BlockSpec((MLA_HEADS, nb, tq, MLA_QK_PAD), lambda b, i: (0, b, i, 0))
    k_spec = pl.BlockSpec((MLA_HEADS, nb, seq, MLA_QK_PAD), lambda b, i: (0, b, 0, 0))
    v_spec = pl.BlockSpec((MLA_HEADS, nb, seq, MLA_V), lambda b, i: (0, b, 0, 0))
    y = pl.pallas_call(
        _flash_kernel,
        grid=(bsz // nb, seq // tq),
        in_specs=[q_spec, k_spec, v_spec],
        out_specs=pl.BlockSpec((nb, tq, MLA_HEADS * MLA_V), lambda b, i: (b, i, 0)),
        out_shape=jax.ShapeDtypeStruct((bsz, seq, MLA_HEADS * MLA_V), F32),
        scratch_shapes=[pltpu.VMEM((n_unit, tq, LANES), F32), pltpu.VMEM((n_unit, tq, LANES), F32),
                        pltpu.VMEM((n_unit, tq, MLA_V), F32)],
        compiler_params=_cparams("parallel", "arbitrary"),
        name="mla_flash",
    )(by_row(q), by_row(k), by_row(v))
    return y.reshape(t, MLA_HEADS * MLA_V)


def _rot_half_cols(w, width):
    shp = w.shape
    w = w.reshape(shp[0], -1, 2, width // 2)
    return jnp.stack([-w[:, :, 1], w[:, :, 0]], axis=2).reshape(shp)


def mla_params(q_norm, kv_norm, w_uq, w_ukv):
    dq = MLA_NOPE + MLA_ROPE
    wq = jnp.zeros((MLA_HEADS, 256, MLA_QK_PAD), F32)
    wqr = jnp.zeros((MLA_HEADS, 256, MLA_QK_PAD), F32)
    w_uq_h = w_uq.reshape(MLA_Q_LORA, MLA_HEADS, dq).transpose(1, 0, 2)
    wq = wq.at[:, :MLA_Q_LORA, :dq].set(w_uq_h)
    pe = w_uq_h[:, :, MLA_NOPE:]
    pe_rot = jnp.concatenate([-pe[..., MLA_ROPE // 2:], pe[..., :MLA_ROPE // 2]], axis=-1)
    wqr = wqr.at[:, :MLA_Q_LORA, MLA_NOPE:dq].set(pe_rot)
    w_kv_h = w_ukv.reshape(MLA_KV_LORA, MLA_HEADS, MLA_NOPE + MLA_V).transpose(1, 0, 2)
    wk = jnp.zeros((MLA_HEADS, MLA_KV_LORA, MLA_QK_PAD), F32).at[:, :, :MLA_NOPE].set(w_kv_h[:, :, :MLA_NOPE])
    wv = w_kv_h[:, :, MLA_NOPE:]
    gq = jnp.zeros((1, 256), F32).at[0, :MLA_Q_LORA].set(q_norm)
    return {"wq": wq.astype(BF16), "wqr": wqr.astype(BF16), "wk": wk.astype(BF16), "wv": wv.astype(BF16),
            "gq": gq, "gkv": kv_norm.reshape(1, MLA_KV_LORA)}


def _head_mask(shape, hd, axis):
    return (lax.broadcasted_iota(jnp.int32, shape, axis) // HEAD_DIM) == hd


def _hgrn_kernel(h_ref, lb_ref, ng_ref, aall_ref, lvl_ref, hm_ref, blk_ref, o_ref, st_ref, e_ref):
    @pl.when(pl.program_id(1) == 0)
    def _():
        st_ref[...] = jnp.zeros_like(st_ref)

    for row in range(h_ref.shape[0]):
        _hgrn_tile(h_ref.at[row], lb_ref, ng_ref, aall_ref, lvl_ref, hm_ref, blk_ref, o_ref.at[row],
                   st_ref.at[row], e_ref.at[row])


def _hgrn_tile(h_ref, lb_ref, ng_ref, aall_ref, lvl_ref, hm_ref, blk_ref, o_ref, st_ref, e_ref):
    tt = h_ref.shape[0]
    width = HEADS * HEAD_DIM
    n_lev = aall_ref.shape[0] // tt - 1

    q = h_ref[:, 0:width]
    f = h_ref[:, width:2 * width]
    v = h_ref[:, 2 * width:3 * width]
    g = h_ref[:, 3 * width:4 * width]
    lb = lb_ref[...]
    log_sig = jnp.minimum(f, 0.0) - jnp.log1p(jnp.exp(-jnp.abs(f)))
    a = jnp.log(lb)
    b = jnp.log1p(-lb) + log_sig
    log_f = jnp.maximum(a, b) + jnp.log1p(jnp.exp(-jnp.abs(a - b)))
    k = (1.0 - lb) * _sigmoid(-f)
    qf = q * _sigmoid(q)
    e2 = _dot(aall_ref[...], jnp.concatenate(_split2(log_f), axis=-1))
    e_ref[...] = e2[:, 0:width] + e2[:, width:2 * width]

    t_row = lax.broadcasted_iota(jnp.int32, (tt, width), 0)
    lvl = lvl_ref[...]
    attn = [jnp.zeros((tt, tt), F32) for _ in range(HEADS)]
    for lev in range(n_lev):
        x = jnp.exp(e_ref[lev * tt:(lev + 1) * tt, :])
        right = ((t_row >> lev) & 1) == 1
        qt = jnp.where(right, qf * x, 0.0).astype(BF16)
        kt = jnp.where(right, 0.0, k * x).astype(BF16)
        q_heads = jnp.concatenate([qt * hm_ref[hd] for hd in range(HEADS)], axis=0)
        s = _dot_nt(q_heads, kt)
        here = lvl == lev
        for hd in range(HEADS):
            attn[hd] = jnp.where(here, s[hd * tt:(hd + 1) * tt, :], attn[hd])
    vb = v.astype(BF16)
    o = jnp.zeros((tt, width), F32)
    for hd in range(HEADS):
        o = o + _dot(attn[hd].astype(BF16), vb * hm_ref[hd])
    ones_blk = blk_ref[...]
    o = o + _dot_exact_rhs(qf * k, ones_blk) * v
    bc = e_ref[n_lev * tt:(n_lev + 1) * tt, :]
    st = st_ref[...]
    st_m = jnp.where(ones_blk > 0, st, 0.0).astype(BF16)
    o = o + _dot_nt((qf * jnp.exp(bc)).astype(BF16), st_m)
    b_last = bc[tt - 1:tt, :]
    k_out = (k * jnp.exp(b_last - bc)).astype(BF16)
    st_ref[...] = st * jnp.exp(b_last) + _dot_tn(vb, k_out)
    ms = _dot_exact_rhs(o * o, ones_blk) * (1.0 / HEAD_DIM)
    o = o * lax.rsqrt(ms + EPS) * ng_ref[...]
    o_ref[...] = o * (g * _sigmoid(g))


def _hgrn_level_matrices(tt):
    n_lev = int(math.log2(tt))
    t = np.arange(tt)[:, None]
    j = np.arange(tt)[None, :]
    mats = []
    for lev in range(n_lev):
        half = 1 << lev
        mid = (t >> (lev + 1) << (lev + 1)) + half
        right = ((t >> lev) & 1) == 1
        mats.append(np.where(right, (j >= mid) & (j <= t), (j > t) & (j < mid)).astype(np.float32))
    mats.append((j <= t).astype(np.float32))
    level = np.full((tt, tt), -1, np.int32)
    lower = np.broadcast_to(j < t, (tt, tt))
    level[lower] = np.floor(np.log2((t ^ j)[lower])).astype(np.int32)
    return jnp.asarray(np.concatenate(mats, axis=0), BF16), jnp.asarray(level)


def _head_block_ones():
    i = np.arange(HEADS * HEAD_DIM)
    return jnp.asarray((i[:, None] // HEAD_DIM == i[None, :] // HEAD_DIM).astype(np.float32), BF16)


def _head_lane_masks(rows):
    i = np.arange(HEADS * HEAD_DIM)
    m = (i[None, :] // HEAD_DIM == np.arange(HEADS)[:, None]).astype(np.float32)
    return jnp.asarray(np.broadcast_to(m[:, None, :], (HEADS, rows, HEADS * HEAD_DIM)), BF16)


def hgrn2_mixer(h_hg, lb, norm_g, bsz, seq, layer):
    t = bsz * seq
    tt = TT_HG
    width = HEADS * HEAD_DIM
    aall, level = _hgrn_level_matrices(tt)
    nt = seq // tt
    nb = NB_HG
    y = pl.pallas_call(
        _hgrn_kernel,
        grid=(bsz // nb, nt),
        in_specs=[pl.BlockSpec((nb, tt, 4 * width), lambda b, i: (b, i, 0)),
                  _layer_spec(lb, layer), _layer_spec(norm_g, layer),
                  _const_spec(aall.shape), _const_spec(level.shape), _const_spec((HEADS, tt, width)),
                  _const_spec((width, width))],
        out_specs=pl.BlockSpec((nb, tt, width), lambda b, i: (b, i, 0)),
        out_shape=jax.ShapeDtypeStruct((bsz, seq, width), F32),
        scratch_shapes=[pltpu.VMEM((nb, width, width), F32), pltpu.VMEM((nb, aall.shape[0], width), F32)],
        compiler_params=_cparams("parallel", "arbitrary"),
        name="hgrn2_mixer",
    )(h_hg.reshape(bsz, seq, 4 * width), lb, norm_g, aall, level, _head_lane_masks(tt), _head_block_ones())
    return y.reshape(t, width)


def _ret_kernel(h_ref, cos_ref, sin_ref, dmat_ref, qdec_ref, kdec_ref, cdec_ref, blk_ref, g_ref, b_ref,
                o_ref, st_ref):
    @pl.when(pl.program_id(1) == 0)
    def _():
        st_ref[...] = jnp.zeros_like(st_ref)

    for row in range(h_ref.shape[0]):
        _ret_tile(h_ref.at[row], cos_ref.at[row], sin_ref.at[row], dmat_ref, qdec_ref, kdec_ref, cdec_ref,
                  blk_ref, g_ref, b_ref, o_ref.at[row], st_ref.at[row])


def _ret_tile(h_ref, cos_ref, sin_ref, dmat_ref, qdec_ref, kdec_ref, cdec_ref, blk_ref, g_ref, b_ref,
              o_ref, st_ref):
    tt = h_ref.shape[0]
    width = HEADS * HEAD_DIM

    cos = cos_ref[...]
    sin = sin_ref[...]
    q = (h_ref[:, 0:width] * cos + h_ref[:, 4 * width:5 * width] * sin) * (HEAD_DIM ** -0.5)
    k = h_ref[:, width:2 * width] * cos + h_ref[:, 5 * width:6 * width] * sin
    v = h_ref[:, 2 * width:3 * width]
    g = h_ref[:, 3 * width:4 * width]
    kb = k.astype(BF16)
    vb = v.astype(BF16)
    ones_blk = blk_ref[...]
    o = jnp.zeros((tt, width), F32)
    for hd in range(HEADS):
        hm = _head_mask((tt, width), hd, 1)
        s = _dot_nt(jnp.where(hm, q, 0.0).astype(BF16), kb) * dmat_ref[hd]
        o = o + _dot(s.astype(BF16), jnp.where(hm, vb, jnp.zeros_like(vb)))
    st = st_ref[...]
    st_m = jnp.where(ones_blk > 0, st, 0.0).astype(BF16)
    o = o + _dot((q * qdec_ref[...]).astype(BF16), st_m)
    st_ref[...] = st * cdec_ref[...] + _dot_tn((k * kdec_ref[...]).astype(BF16), vb)
    mu = _dot_exact_rhs(o, ones_blk) * (1.0 / HEAD_DIM)
    oc = o - mu
    var = _dot_exact_rhs(oc * oc, ones_blk) * (1.0 / HEAD_DIM)
    o = oc * lax.rsqrt(var + EPS) * g_ref[...] + b_ref[...]
    o_ref[...] = (g * _sigmoid(g)) * o


def _retention_tables(tt):
    log_gamma = jnp.log(1.0 - 2.0 ** (-5.0 - jnp.arange(HEADS, dtype=F32)))
    t_idx = jnp.arange(tt, dtype=F32)
    rel = t_idx[:, None] - t_idx[None, :]
    dmat = jnp.where(rel >= 0, jnp.exp(jnp.maximum(rel, 0.0)[None] * log_gamma[:, None, None]), 0.0)
    lanes = lambda a: jnp.repeat(a, HEAD_DIM, axis=-1)
    qdec = lanes(jnp.exp((t_idx + 1.0)[:, None] * log_gamma[None, :]))
    kdec = lanes(jnp.exp((tt - 1 - t_idx)[:, None] * log_gamma[None, :]))
    cdec = lanes(jnp.exp(tt * log_gamma)[None, :])
    return dmat, qdec, kdec, cdec


def retention_mixer(h_ret, cos_r, sin_r, gn_g, gn_b, bsz, seq, layer):
    t = bsz * seq
    tt = TT_RET
    width = HEADS * HEAD_DIM
    nt = seq // tt
    dmat, qdec, kdec, cdec = _retention_tables(tt)
    nb = NB_RET
    row = lambda w: pl.BlockSpec((nb, tt, w), lambda b, i: (b, i, 0))
    y = pl.pallas_call(
        _ret_kernel,
        grid=(bsz // nb, nt),
        in_specs=[row(W_RET_IN), row(width), row(width), _const_spec((HEADS, tt, tt)), _const_spec((tt, width)),
                  _const_spec((tt, width)), _const_spec((1, width)), _const_spec((width, width)),
                  _layer_spec(gn_g, layer), _layer_spec(gn_b, layer)],
        out_specs=row(width),
        out_shape=jax.ShapeDtypeStruct((bsz, seq, width), F32),
        scratch_shapes=[pltpu.VMEM((nb, width, width), F32)],
        compiler_params=_cparams("parallel", "arbitrary"),
        name="retention_mixer",
    )(h_ret.reshape(bsz, seq, W_RET_IN), cos_r.reshape(bsz, seq, width), sin_r.reshape(bsz, seq, width),
      dmat, qdec, kdec, cdec, _head_block_ones(), gn_g, gn_b)
    return y.reshape(t, width)


def _store_token_tiles(ref, val):
    tm = val.shape[0]
    for j in range(D_MODEL // LANES):
        ref[pl.ds(j, tm, stride=SUBLANES), :] = val[:, j * LANES:(j + 1) * LANES]


def _load_token_tiles(ref, tm):
    return jnp.concatenate([ref[pl.ds(j, tm, stride=SUBLANES), :] for j in range(D_MODEL // LANES)], axis=-1)


def _mix_kernel(x_ref, ya_ref, yb_ref, yc_ref, yd_ref, wg_ref, wb_ref, wo_ref, g_ref, b_ref, o_ref):
    x = x_ref[...]
    xb = x.astype(BF16)
    acc = jnp.zeros(x.shape, F32)
    for n, y_ref in enumerate((ya_ref, yb_ref, yc_ref, yd_ref)):
        gate = _sigmoid(_dot(xb, wg_ref[:, n * D_MODEL:(n + 1) * D_MODEL]))
        acc = acc + gate * _dot(y_ref[...].astype(BF16), wb_ref[n])
    z = ALPHA * x + _dot(acc.astype(BF16), wo_ref[...])
    o_ref[...] = _layer_norm_rows(z, g_ref[...], b_ref[...])


def mix_branches(x, ys, params, layer):
    t = x.shape[0]
    tm = TM_PROJ
    row = lambda w: pl.BlockSpec((tm, w), lambda i: (i, 0))
    return pl.pallas_call(
        _mix_kernel,
        grid=(t // tm,),
        in_specs=[row(D_MODEL)] + [row(MIX_WIDTH)] * 4 + [_layer_spec(a, layer) for a in params],
        out_specs=row(D_MODEL),
        out_shape=jax.ShapeDtypeStruct((t, D_MODEL), F32),
        compiler_params=_cparams("parallel"),
        name="mix_branches",
    )(x, *ys, *params)


def _ffn_kernel(x_ref, wg_ref, wu_ref, wd_ref, o_ref, xb_ref, acc_ref):
    c = pl.program_id(1)

    @pl.when(c == 0)
    def _():
        xb_ref[...] = x_ref[...].astype(BF16)
        acc_ref[...] = jnp.zeros_like(acc_ref)

    xb = xb_ref[...]
    gate = _dot(xb, wg_ref[...])
    act = gate * _sigmoid(gate) * _dot(xb, wu_ref[...])
    acc_ref[...] += _dot(act.astype(BF16), wd_ref[...])

    @pl.when(c == pl.num_programs(1) - 1)
    def _():
        o_ref[...] = acc_ref[...]


def dense_ffn(x, w_gate, w_up, w_down, layer):
    t = x.shape[0]
    tm, tf = TM_FFN, TF_FFN
    return pl.pallas_call(
        _ffn_kernel,
        grid=(t // tm, D_FF // tf),
        in_specs=[pl.BlockSpec((tm, D_MODEL), lambda i, c: (i, 0)),
                  pl.BlockSpec((None, D_MODEL, tf), lambda i, c: (layer, 0, c)),
                  pl.BlockSpec((None, D_MODEL, tf), lambda i, c: (layer, 0, c)),
                  pl.BlockSpec((None, tf, D_MODEL), lambda i, c: (layer, c, 0))],
        out_specs=pl.BlockSpec((tm, D_MODEL), lambda i, c: (i, 0)),
        out_shape=jax.ShapeDtypeStruct((t, D_MODEL), F32),
        scratch_shapes=[pltpu.VMEM((tm, D_MODEL), BF16), pltpu.VMEM((tm, D_MODEL), F32)],
        compiler_params=_cparams("parallel", "arbitrary"),
        name="dense_ffn",
    )(x, w_gate, w_up, w_down)


def _router_kernel(x_ref, wh_ref, wl_ref, upper_ref, meta_ref, cnt_ref, carry_ref):
    tm = x_ref.shape[0]

    @pl.when(pl.program_id(0) == 0)
    def _():
        carry_ref[...] = jnp.zeros_like(carry_ref)

    xh, xl = _split2(x_ref[...])
    wh = wh_ref[...]
    logits = _dot_nt(wh, xh) + _dot_nt(wh, xl) + _dot_nt(wl_ref[...], xh)
    e_idx = lax.broadcasted_iota(jnp.int32, logits.shape, 0)
    m1 = jnp.max(logits, axis=0, keepdims=True)
    i1 = jnp.min(jnp.where(logits == m1, e_idx, N_EXPERTS), axis=0, keepdims=True)
    rest = jnp.where(e_idx == i1, -jnp.inf, logits)
    m2 = jnp.max(rest, axis=0, keepdims=True)
    i2 = jnp.min(jnp.where(rest == m2, e_idx, N_EXPERTS), axis=0, keepdims=True)
    ex = jnp.exp(m2 - m1)
    w1 = 1.0 / (1.0 + ex)
    w2 = ex / (1.0 + ex)
    sel1 = e_idx == i1
    sel2 = e_idx == i2
    onehot = jnp.where(sel1 | sel2, 1.0, 0.0)
    carry = carry_ref[...]
    ranks = _dot(onehot.astype(BF16), upper_ref[...]) + carry[:, 0:1]
    r1 = jnp.sum(jnp.where(sel1, ranks, 0.0), axis=0, keepdims=True)
    r2 = jnp.sum(jnp.where(sel2, ranks, 0.0), axis=0, keepdims=True)
    row = lax.broadcasted_iota(jnp.int32, (SUBLANES, tm), 0)
    vals = (i1.astype(F32), i2.astype(F32), w1, w2, r1, r2)
    meta = jnp.zeros((SUBLANES, tm), F32)
    for n, val in enumerate(vals):
        meta = jnp.where(row == n, val, meta)
    meta_ref[...] = meta
    carry = carry + jnp.sum(onehot, axis=1, keepdims=True)
    carry_ref[...] = carry
    cnt_ref[...] = carry


def moe_router(x, w_router):
    t = x.shape[0]
    tm = TM_ROUTE
    wt = w_router.T
    wh = wt.astype(BF16)
    wl = (wt - wh.astype(F32)).astype(BF16)
    upper = jnp.asarray(np.triu(np.ones((tm, tm), np.float32), 1), BF16)
    return pl.pallas_call(
        _router_kernel,
        grid=(t // tm,),
        in_specs=[pl.BlockSpec((tm, D_MODEL), lambda i: (i, 0)), _const_spec((N_EXPERTS, D_MODEL)),
                  _const_spec((N_EXPERTS, D_MODEL)), _const_spec((tm, tm))],
        out_specs=[pl.BlockSpec((SUBLANES, tm), lambda i: (0, i)), pl.BlockSpec((N_EXPERTS, LANES), lambda i: (0, 0))],
        out_shape=[jax.ShapeDtypeStruct((SUBLANES, t), F32), jax.ShapeDtypeStruct((N_EXPERTS, LANES), F32)],
        scratch_shapes=[pltpu.VMEM((N_EXPERTS, LANES), F32)],
        compiler_params=_cparams("arbitrary"),
        name="moe_router",
    )(x, wh, wl, upper)


def _token_rows(ref, tok):
    return ref.at[pl.ds(pl.multiple_of(tok * SUBLANES, SUBLANES), SUBLANES), :]


def _start_then_wait(n, copies):
    def start(r, carry):
        for idx, cp in enumerate(copies(r)):
            cp.start(priority=idx % 2)
        return carry

    def wait(r, carry):
        for cp in copies(r):
            cp.wait()
        return carry

    lax.fori_loop(0, n, start, 0, unroll=DMA_UNROLL)
    return lambda: lax.fori_loop(0, n, wait, 0, unroll=DMA_UNROLL)


def _dispatch_kernel(dest_ref, x_ref, init_ref, xs_ref, buf_ref, sem):
    del init_ref
    td = x_ref.shape[0]
    _store_token_tiles(buf_ref, x_ref[...])

    def copies(r):
        src = _token_rows(buf_ref, r)
        return (pltpu.make_async_copy(src, _token_rows(xs_ref, dest_ref[0, 0, 2 * r]), sem),
                pltpu.make_async_copy(src, _token_rows(xs_ref, dest_ref[0, 0, 2 * r + 1]), sem))

    _start_then_wait(td, copies)()


def moe_dispatch(x, dest3, p_len):
    t = x.shape[0]
    td = dest3.shape[2] // 2
    init = jnp.zeros((p_len * SUBLANES, LANES), F32)
    return pl.pallas_call(
        _dispatch_kernel,
        grid=(t // td,),
        in_specs=[pl.BlockSpec((1, 1, 2 * td), lambda i: (i, 0, 0), memory_space=pltpu.SMEM),
                  pl.BlockSpec((td, D_MODEL), lambda i: (i, 0)), pl.BlockSpec(memory_space=pl.ANY)],
        out_specs=pl.BlockSpec(memory_space=pl.ANY),
        out_shape=jax.ShapeDtypeStruct((p_len * SUBLANES, LANES), F32),
        scratch_shapes=[pltpu.VMEM((td * SUBLANES, LANES), F32), pltpu.SemaphoreType.DMA(())],
        input_output_aliases={2: 0},
        compiler_params=pltpu.CompilerParams(dimension_semantics=("arbitrary",), has_side_effects=True),
        name="moe_dispatch",
    )(dest3, x, init)


def _expert_kernel(te_ref, nv_ref, xs_ref, wg_ref, wu_ref, wd_ref, ys_ref, xb_ref, acc_ref):
    i = pl.program_id(0)
    c = pl.program_id(1)
    tm = xb_ref.shape[0]
    valid = i < nv_ref[0]

    @pl.when(valid & (c == 0))
    def _():
        xb_ref[...] = _load_token_tiles(xs_ref, tm).astype(BF16)
        acc_ref[...] = jnp.zeros_like(acc_ref)

    @pl.when(valid)
    def _():
        xb = xb_ref[...]
        gate = _dot(xb, wg_ref[...].astype(BF16))
        act = gate * _sigmoid(gate) * _dot(xb, wu_ref[...].astype(BF16))
        acc_ref[...] += _dot(act.astype(BF16), wd_ref[...].astype(BF16))

    @pl.when(c == pl.num_programs(1) - 1)
    def _():
        _store_token_tiles(ys_ref, jnp.where(valid, acc_ref[...], 0.0))


def moe_experts(xs, tile_expert, n_valid, layer, w_gate, w_up, w_down):
    tm, tf = TM_MOE, TF_FFN
    n_tiles = xs.shape[0] // (tm * SUBLANES)
    nc = D_FF // tf

    def chunk(i, c, nv):
        return jnp.where(i < nv[0], c, nc - 1)

    grid_spec = pltpu.PrefetchScalarGridSpec(
        num_scalar_prefetch=2,
        grid=(n_tiles, nc),
        in_specs=[pl.BlockSpec((tm * SUBLANES, LANES), lambda i, c, te, nv: (jnp.minimum(i, nv[0] - 1), 0)),
                  pl.BlockSpec((None, None, D_MODEL, tf), lambda i, c, te, nv: (layer, te[i], 0, chunk(i, c, nv))),
                  pl.BlockSpec((None, None, D_MODEL, tf), lambda i, c, te, nv: (layer, te[i], 0, chunk(i, c, nv))),
                  pl.BlockSpec((None, None, tf, D_MODEL), lambda i, c, te, nv: (layer, te[i], chunk(i, c, nv), 0))],
        out_specs=pl.BlockSpec((tm * SUBLANES, LANES), lambda i, c, te, nv: (i, 0)),
        scratch_shapes=[pltpu.VMEM((tm, D_MODEL), BF16), pltpu.VMEM((tm, D_MODEL), F32)],
    )
    return pl.pallas_call(
        _expert_kernel,
        grid_spec=grid_spec,
        out_shape=jax.ShapeDtypeStruct(xs.shape, F32),
        compiler_params=_cparams("arbitrary", "arbitrary"),
        name="moe_experts",
    )(tile_expert, n_valid, xs, w_gate, w_up, w_down)


def moe_ffn(x, w_router, layer, w_gate, w_up, w_down):
    t = x.shape[0]
    tm = TM_MOE
    meta, cnt = moe_router(x, w_router)
    counts = cnt[:, 0].astype(jnp.int32)
    padded = (counts + tm - 1) // tm * tm
    pad_end = jnp.cumsum(padded)
    pad_start = pad_end - padded
    experts = meta[0:2].astype(jnp.int32)
    start_of = jnp.sum(jnp.where(experts[..., None] == jnp.arange(N_EXPERTS), pad_start, 0), axis=-1)
    dest = (start_of + meta[4:6].astype(jnp.int32)).T
    dest3 = dest.reshape(t // TD_DISPATCH, 1, 2 * TD_DISPATCH)
    n_tiles = (2 * t) // tm + N_EXPERTS
    n_valid = (pad_end[-1] // tm).astype(jnp.int32).reshape(1)
    tile_start = jnp.arange(n_tiles, dtype=jnp.int32) * tm
    tile_expert = jnp.minimum(jnp.sum(pad_end[None, :] <= tile_start[:, None], axis=1), N_EXPERTS - 1)
    last_expert = jnp.sum(jnp.where(jnp.arange(n_tiles) == n_valid[0] - 1, tile_expert, 0))
    tile_expert = jnp.where(jnp.arange(n_tiles) < n_valid[0], tile_expert, last_expert).astype(jnp.int32)
    xs = moe_dispatch(x, dest3, n_tiles * tm)
    ys = moe_experts(xs, tile_expert, n_valid, layer, w_gate, w_up, w_down)
    return ys, dest3, meta[2:4].T


def _post_kernel(x_ref, p_ref, wpg_ref, wpp_ref, g_ref, b_ref, f_ref, o_ref):
    x = x_ref[...]
    ple = _sigmoid(_dot(x.astype(BF16), wpg_ref[...])) * _dot(p_ref[...].astype(BF16), wpp_ref[...])
    o_ref[...] = _layer_norm_rows(ALPHA * x + f_ref[...] + ple, g_ref[...], b_ref[...])


def _post_moe_kernel(dest_ref, x_ref, p_ref, wpg_ref, wpp_ref, g_ref, b_ref, w_ref, ys_ref, o_ref,
                     buf0_ref, buf1_ref, sem):
    x = x_ref[...]
    tm = x.shape[0]

    def copies(r):
        return (pltpu.make_async_copy(_token_rows(ys_ref, dest_ref[0, 0, 2 * r]), _token_rows(buf0_ref, r), sem),
                pltpu.make_async_copy(_token_rows(ys_ref, dest_ref[0, 0, 2 * r + 1]), _token_rows(buf1_ref, r), sem))

    wait_all = _start_then_wait(tm, copies)
    ple = _sigmoid(_dot(x.astype(BF16), wpg_ref[...])) * _dot(p_ref[...].astype(BF16), wpp_ref[...])
    wait_all()
    w = w_ref[...]
    f = w[:, 0:1] * _load_token_tiles(buf0_ref, tm) + w[:, 1:2] * _load_token_tiles(buf1_ref, tm)
    o_ref[...] = _layer_norm_rows(ALPHA * x + f + ple, g_ref[...], b_ref[...])


def post_layer(x, p, layer, params, f=None, moe=None):
    t = x.shape[0]
    tm = TM_PROJ if moe is None else TD_DISPATCH
    row = lambda w: pl.BlockSpec((tm, w), lambda i: (i, 0))
    in_specs = [row(D_MODEL), pl.BlockSpec((None, tm, PLE_DIM), lambda i: (layer, i, 0))]
    in_specs += [_layer_spec(a, layer) for a in params]
    args = [x, p, *params]
    scratch = []
    if moe is None:
        body = _post_kernel
        in_specs += [row(D_MODEL)]
        args += [f]
    else:
        body = _post_moe_kernel
        ys, dest3, w2 = moe
        in_specs = [pl.BlockSpec((1, 1, 2 * tm), lambda i: (i, 0, 0), memory_space=pltpu.SMEM)] + in_specs
        in_specs += [row(2), pl.BlockSpec(memory_space=pl.ANY)]
        args = [dest3] + args + [w2, ys]
        scratch = [pltpu.VMEM((tm * SUBLANES, LANES), F32), pltpu.VMEM((tm * SUBLANES, LANES), F32),
                   pltpu.SemaphoreType.DMA(())]
    return pl.pallas_call(
        body,
        grid=(t // tm,),
        in_specs=in_specs,
        out_specs=row(D_MODEL),
        out_shape=jax.ShapeDtypeStruct((t, D_MODEL), F32),
        scratch_shapes=scratch,
        compiler_params=_cparams("parallel" if moe is None else "arbitrary"),
        name="post_layer",
    )(*args)


def split_in_weights(w):
    sizes = (MIX_WIDTH, MLA_Q_LORA, MLA_KV_LORA, MLA_ROPE) + (HEADS * HEAD_DIM,) * 8 + (N_BRANCH * D_MODEL,)
    parts, start = [], 0
    for n in sizes:
        parts.append(w[:, start:start + n])
        start += n
    u_s5, c_q, c_kv, k_rope, hq, hf, hi, hg, rq, rk, rv, rg, gate = parts
    k_rot = jnp.concatenate([-k_rope[:, MLA_ROPE // 2:], k_rope[:, :MLA_ROPE // 2]], axis=1)
    w_mla = jnp.zeros((D_MODEL, W_MLA_IN), F32)
    w_mla = w_mla.at[:, 0:MLA_Q_LORA].set(c_q).at[:, 256:384].set(c_kv)
    w_mla = w_mla.at[:, 384 + MLA_NOPE:384 + MLA_NOPE + MLA_ROPE].set(k_rope)
    w_mla = w_mla.at[:, 512 + MLA_NOPE:512 + MLA_NOPE + MLA_ROPE].set(k_rot)
    w_hg = jnp.concatenate([hq, hf, hi, hg], axis=1)
    w_ret = jnp.concatenate([rq, rk, rv, rg, _rot_half_cols(rq, HEAD_DIM), _rot_half_cols(rk, HEAD_DIM)], axis=1)
    return [a.astype(BF16) for a in (w_mla, u_s5, w_hg, w_ret)], gate.astype(BF16)


def kernel(x, p, positions, w_in, s5_a_re, s5_a_im, s5_log_dt, s5_b_re, s5_b_im, s5_c_re, s5_c_im, s5_d, s5_w_glu, mla_q_norm, mla_kv_norm, mla_w_uq, mla_w_ukv, hg_lb_raw, hg_norm, ret_gn_g, ret_gn_b, w_branch, w_o, ln1_g, ln1_b, ff_w_gate, ff_w_up, ff_w_down, moe_router, moe_w_gate, moe_w_up, moe_w_down, ple_w_gate, ple_w_proj, ln2_g, ln2_b):
    bsz, seq, _ = x.shape
    t = bsz * seq
    depth = w_in.shape[0]
    x = x.reshape(t, D_MODEL)
    p = p.reshape(depth, t, PLE_DIM)
    cos_m, sin_m, cos_r, sin_r = rope_tables(positions)
    lb_all = jnp.cumsum(jax.nn.softmax(hg_lb_raw.astype(F32), axis=0), axis=0)
    lb_all = lb_all - lb_all[0]
    rows = lambda a: a.astype(F32).reshape(depth, 1, -1)
    in_ws, w_gate = jax.vmap(split_in_weights)(w_in)
    s5_prm = jax.vmap(s5_params)(s5_a_re, s5_a_im, s5_log_dt, s5_b_re, s5_b_im, s5_c_re, s5_c_im, s5_d, s5_w_glu)
    mla_prm = jax.vmap(mla_params)(mla_q_norm, mla_kv_norm, mla_w_uq, mla_w_ukv)
    mix_prm = (w_gate, w_branch.astype(BF16), w_o.astype(BF16), rows(ln1_g), rows(ln1_b))
    post_prm = (ple_w_gate.astype(BF16), ple_w_proj.astype(BF16), rows(ln2_g), rows(ln2_b))
    ffn_w = (ff_w_gate.astype(BF16), ff_w_up.astype(BF16), ff_w_down.astype(BF16))
    lb_rows, hg_norm_rows, gn_g_rows, gn_b_rows = rows(lb_all), rows(hg_norm), rows(ret_gn_g), rows(ret_gn_b)
    for i in range(depth):
        h_mla, h_s5, h_hg, h_ret = in_projection(x, in_ws, i)
        y_a = s5_mixer(h_s5, bsz, seq, s5_prm, i)
        y_b = mla_mixer(h_mla, cos_m, sin_m, bsz, seq, mla_prm, i)
        y_c = hgrn2_mixer(h_hg, lb_rows, hg_norm_rows, bsz, seq, i)
        y_d = retention_mixer(h_ret, cos_r, sin_r, gn_g_rows, gn_b_rows, bsz, seq, i)
        x = mix_branches(x, (y_a, y_b, y_c, y_d), mix_prm, i)
        if i % 2 == 1:
            x = post_layer(x, p, i, post_prm, moe=moe_ffn(x, moe_router[i // 2], i // 2, moe_w_gate, moe_w_up,
                                                           moe_w_down))
        else:
            x = post_layer(x, p, i, post_prm, f=dense_ffn(x, *ffn_w, i // 2))
    return x.reshape(bsz, seq, D_MODEL)
```

```python
import functools
import math

import numpy as np
import jax
import jax.numpy as jnp
from jax import lax
from jax.experimental import pallas as pl
from jax.experimental.pallas import tpu as pltpu

F32 = jnp.float32
BF16 = jnp.bfloat16

D_MODEL = 1024
DEPTH = 4
PLE_DIM = 256
MIX_WIDTH = 256
N_BRANCH = 4
S5_GROUP = 16
S5_GROUPS = 16
S5_STATE = 64
MLA_HEADS = 4
MLA_NOPE = 64
MLA_ROPE = 32
MLA_V = 64
MLA_Q_LORA = 192
MLA_KV_LORA = 128
HEADS = 4
HEAD_DIM = 64
D_FF = 3584
N_EXPERTS = 8
ROPE_BASE = 10000.0
EPS = 1e-5
NEG_INF = -1e30
ALPHA = (2 * DEPTH) ** 0.25

LANES = 128
SUBLANES = 8
VMEM_LIMIT = 56 * 1024 * 1024

TM_PROJ = 512
TS_S5 = 64
TQ_ATT = 256
NB_ATT = 2
TT_HG = 128
NB_HG = 4
TT_RET = 256
NB_RET = 4
TM_FFN = 1024
TF_FFN = 512
TM_MOE = 1024
TM_ROUTE = 512
TD_DISPATCH = 512
DMA_UNROLL = 8


def _cparams(*sem):
    return pltpu.CompilerParams(dimension_semantics=sem, vmem_limit_bytes=VMEM_LIMIT)


def _const_spec(shape):
    nd = len(shape)
    return pl.BlockSpec(shape, lambda *_: (0,) * nd, pipeline_mode=pl.Buffered(1))


def _layer_spec(arr, layer):
    shape = arr.shape[1:]
    nd = len(shape)
    return pl.BlockSpec((None,) + shape, lambda *_: (layer,) + (0,) * nd, pipeline_mode=pl.Buffered(1))


def _dot(a, b):
    return jnp.dot(a, b, preferred_element_type=F32)


def _dot_nt(a, b):
    return lax.dot_general(a, b, (((1,), (1,)), ((), ())), preferred_element_type=F32)


def _dot_tn(a, b):
    return lax.dot_general(a, b, (((0,), (0,)), ((), ())), preferred_element_type=F32)


def _split2(x):
    hi = x.astype(BF16)
    lo = (x - hi.astype(F32)).astype(BF16)
    return hi, lo


def _split3(x):
    h1 = x.astype(BF16)
    r1 = x - h1.astype(F32)
    h2 = r1.astype(BF16)
    h3 = (r1 - h2.astype(F32)).astype(BF16)
    return h1, h2, h3


def _dot_exact_rhs(x, m):
    hi, lo = _split2(x)
    return _dot(hi, m) + _dot(lo, m)


def _dot_exact_lhs(m, x):
    hi, lo = _split2(x)
    return _dot(m, hi) + _dot(m, lo)


def _sigmoid(x):
    return 1.0 / (1.0 + jnp.exp(-x))


def _layer_norm_rows(z, g, b):
    mu = jnp.mean(z, axis=-1, keepdims=True)
    zc = z - mu
    var = jnp.mean(zc * zc, axis=-1, keepdims=True)
    return zc * lax.rsqrt(var + EPS) * g + b


def _rope_kernel(pos_ref, fm_ref, fr_ref, cm_ref, sm_ref, cr_ref, sr_ref):
    pos = pos_ref[...]
    ang_m = pos * fm_ref[...]
    lane = lax.broadcasted_iota(jnp.int32, ang_m.shape, 1)
    in_pe = (lane >= MLA_NOPE) & (lane < MLA_NOPE + MLA_ROPE)
    cm_ref[...] = jnp.where(lane < MLA_NOPE, 1.0, jnp.where(in_pe, jnp.cos(ang_m), 0.0))
    sm_ref[...] = jnp.where(in_pe, jnp.sin(ang_m), 0.0)
    ang_r = pos * fr_ref[...]
    cr_ref[...] = jnp.cos(ang_r)
    sr_ref[...] = jnp.sin(ang_r)


def rope_tables(positions):
    t = positions.size
    tm = TM_PROJ
    pos = positions.reshape(t, 1).astype(F32)
    inv_m = ROPE_BASE ** (-jnp.arange(MLA_ROPE // 2, dtype=F32) / (MLA_ROPE // 2))
    inv_r = ROPE_BASE ** (-jnp.arange(HEAD_DIM // 2, dtype=F32) / (HEAD_DIM // 2))
    fm = jnp.zeros((1, LANES), F32).at[0, MLA_NOPE:MLA_NOPE + MLA_ROPE].set(jnp.concatenate([inv_m, inv_m]))
    fr = jnp.tile(jnp.concatenate([inv_r, inv_r]), HEADS).reshape(1, HEADS * HEAD_DIM)
    row = lambda w: pl.BlockSpec((tm, w), lambda i: (i, 0))
    return pl.pallas_call(
        _rope_kernel,
        grid=(t // tm,),
        in_specs=[row(1), _const_spec((1, LANES)), _const_spec((1, 256))],
        out_specs=[row(LANES), row(LANES), row(256), row(256)],
        out_shape=[jax.ShapeDtypeStruct((t, LANES), F32), jax.ShapeDtypeStruct((t, LANES), F32),
                   jax.ShapeDtypeStruct((t, 256), F32), jax.ShapeDtypeStruct((t, 256), F32)],
        compiler_params=_cparams("parallel"),
        name="rope_tables",
    )(pos, fm, fr)


W_MLA_IN = 640
W_RET_IN = 1536
_N_CHUNK = 256


def _inproj_kernel(x_ref, *refs):
    n = len(refs) // 2
    xb = x_ref[...].astype(BF16)
    for w_ref, o_ref in zip(refs[:n], refs[n:]):
        width = w_ref.shape[1]
        for c0 in range(0, width, _N_CHUNK):
            c1 = min(c0 + _N_CHUNK, width)
            o_ref[:, c0:c1] = _dot(xb, w_ref[:, c0:c1])


def in_projection(x, weights, layer):
    t = x.shape[0]
    tm = TM_PROJ
    return pl.pallas_call(
        _inproj_kernel,
        grid=(t // tm,),
        in_specs=[pl.BlockSpec((tm, D_MODEL), lambda i: (i, 0))] + [_layer_spec(w, layer) for w in weights],
        out_specs=[pl.BlockSpec((tm, w.shape[2]), lambda i: (i, 0)) for w in weights],
        out_shape=[jax.ShapeDtypeStruct((t, w.shape[2]), F32) for w in weights],
        compiler_params=_cparams("parallel"),
        name="in_projection",
    )(x, *weights)


S5_HALF = S5_GROUPS * S5_STATE


def _s5_kernel(u_ref, perm_ref, permt_ref, bmat_ref, ar_ref, ai_ref, cmat_ref, d_ref, wglu_ref, y_ref,
               bu_ref, hs_ref, h_ref):
    nb, ts, width = u_ref.shape
    n_half = S5_HALF // LANES

    @pl.when(pl.program_id(0) == 0)
    def _():
        h_ref[...] = jnp.zeros_like(h_ref)

    u = u_ref[...].reshape(nb * ts, width)
    u_tm = _dot(perm_ref[...], u.astype(BF16)).astype(BF16)
    bu_ref[...] = _dot(u_tm, bmat_ref[...])
    ar = [jnp.broadcast_to(ar_ref[:, c * LANES:(c + 1) * LANES], (nb, LANES)) for c in range(n_half)]
    ai = [jnp.broadcast_to(ai_ref[:, c * LANES:(c + 1) * LANES], (nb, LANES)) for c in range(n_half)]
    h_re = [h_ref[:, c * LANES:(c + 1) * LANES] for c in range(n_half)]
    h_im = [h_ref[:, S5_HALF + c * LANES:S5_HALF + (c + 1) * LANES] for c in range(n_half)]
    for n in range(ts):
        rows = slice(n * nb, (n + 1) * nb)
        for c in range(n_half):
            re_cols = slice(c * LANES, (c + 1) * LANES)
            im_cols = slice(S5_HALF + c * LANES, S5_HALF + (c + 1) * LANES)
            new_re = ar[c] * h_re[c] - ai[c] * h_im[c] + bu_ref[rows, re_cols]
            new_im = ar[c] * h_im[c] + ai[c] * h_re[c] + bu_ref[rows, im_cols]
            h_re[c], h_im[c] = new_re, new_im
            hs_ref[rows, re_cols] = new_re
            hs_ref[rows, im_cols] = new_im
    for c in range(n_half):
        h_ref[:, c * LANES:(c + 1) * LANES] = h_re[c]
        h_ref[:, S5_HALF + c * LANES:S5_HALF + (c + 1) * LANES] = h_im[c]
    y_tm = _dot(hs_ref[...].astype(BF16), cmat_ref[...])
    y = _dot_exact_lhs(permt_ref[...], y_tm) + d_ref[...] * u
    y = jax.nn.gelu(y)
    y = y * _sigmoid(_dot(y.astype(BF16), wglu_ref[...]))
    y_ref[...] = y.reshape(nb, ts, width)


def _step_major_permutation(nb, ts):
    p = np.zeros((nb * ts, nb * ts), np.float32)
    b, n = np.meshgrid(np.arange(nb), np.arange(ts), indexing="ij")
    p[(n * nb + b).ravel(), (b * ts + n).ravel()] = 1.0
    return jnp.asarray(p, BF16), jnp.asarray(p.T, BF16)


def s5_mixer(u, bsz, seq, prm, layer):
    ts = TS_S5
    u3 = u.reshape(bsz, seq, MIX_WIDTH)
    blk = pl.BlockSpec((bsz, ts, MIX_WIDTH), lambda i: (0, i, 0))
    perm, perm_t = _step_major_permutation(bsz, ts)
    names = ("bmat", "a_re", "a_im", "cmat", "d", "w_glu")
    y = pl.pallas_call(
        _s5_kernel,
        grid=(seq // ts,),
        in_specs=[blk, _const_spec(perm.shape), _const_spec(perm.shape)] +
                 [_layer_spec(prm[n], layer) for n in names],
        out_specs=blk,
        out_shape=jax.ShapeDtypeStruct((bsz, seq, MIX_WIDTH), F32),
        scratch_shapes=[pltpu.VMEM((bsz * ts, 2 * S5_HALF), F32), pltpu.VMEM((bsz * ts, 2 * S5_HALF), F32),
                        pltpu.VMEM((bsz, 2 * S5_HALF), F32)],
        compiler_params=_cparams("arbitrary"),
        name="s5_mixer",
    )(u3, perm, perm_t, *[prm[n] for n in names])
    return y.reshape(bsz * seq, MIX_WIDTH)


def s5_params(a_re, a_im, log_dt, b_re, b_im, c_re, c_im, d_skip, w_glu):
    dt = jnp.exp(log_dt)[:, None]
    mag = jnp.exp(a_re * dt)
    abar_re, abar_im = mag * jnp.cos(a_im * dt), mag * jnp.sin(a_im * dt)
    den = a_re * a_re + a_im * a_im
    num_re, num_im = abar_re - 1.0, abar_im
    coef_re = (num_re * a_re + num_im * a_im) / den
    coef_im = (num_im * a_re - num_re * a_im) / den
    bbar_re = coef_re[..., None] * b_re - coef_im[..., None] * b_im
    bbar_im = coef_re[..., None] * b_im + coef_im[..., None] * b_re
    eye = jnp.eye(S5_GROUPS, dtype=F32)
    b_bd_re = jnp.einsum("gpc,gh->gchp", bbar_re, eye).reshape(MIX_WIDTH, S5_HALF)
    b_bd_im = jnp.einsum("gpc,gh->gchp", bbar_im, eye).reshape(MIX_WIDTH, S5_HALF)
    c_bd_re = jnp.einsum("gcp,gh->gphc", c_re, eye).reshape(S5_HALF, MIX_WIDTH)
    c_bd_im = jnp.einsum("gcp,gh->gphc", c_im, eye).reshape(S5_HALF, MIX_WIDTH)
    return {
        "bmat": jnp.concatenate([b_bd_re, b_bd_im], axis=1).astype(BF16),
        "cmat": jnp.concatenate([c_bd_re, -c_bd_im], axis=0).astype(BF16),
        "a_re": abar_re.reshape(1, S5_HALF), "a_im": abar_im.reshape(1, S5_HALF),
        "d": d_skip.reshape(1, MIX_WIDTH), "w_glu": w_glu.astype(BF16),
    }


MLA_QK_PAD = 128


def _mla_prep_kernel(h_ref, cos_ref, sin_ref, gq_ref, gkv_ref, wq_ref, wqr_ref, wk_ref, wv_ref,
                     q_ref, k_ref, v_ref):
    h = h_ref[...]
    cos = cos_ref[...]
    sin = sin_ref[...]
    c_q = h[:, 0:256]
    c_kv = h[:, 256:384]
    k_pe = h[:, 384:512] * cos + h[:, 512:640] * sin
    qn = c_q * lax.rsqrt(jnp.sum(c_q * c_q, axis=-1, keepdims=True) / MLA_Q_LORA + EPS) * gq_ref[...]
    kvn = c_kv * lax.rsqrt(jnp.mean(c_kv * c_kv, axis=-1, keepdims=True) + EPS) * gkv_ref[...]
    qn = qn.astype(BF16)
    kvn = kvn.astype(BF16)
    scale = (MLA_NOPE + MLA_ROPE) ** -0.5 * math.log2(math.e)
    heads = lambda a: jnp.concatenate([a] * MLA_HEADS, axis=-1)
    q = (_dot(qn, wq_ref[...]) * heads(cos) + _dot(qn, wqr_ref[...]) * heads(sin)) * scale
    k = _dot(kvn, wk_ref[...]) + heads(k_pe)
    v = _dot(kvn, wv_ref[...])
    for hd in range(MLA_HEADS):
        q_ref[hd] = q[:, hd * MLA_QK_PAD:(hd + 1) * MLA_QK_PAD].astype(BF16)
        k_ref[hd] = k[:, hd * MLA_QK_PAD:(hd + 1) * MLA_QK_PAD].astype(BF16)
        v_ref[hd] = v[:, hd * MLA_V:(hd + 1) * MLA_V].astype(BF16)


def _flash_kernel(q_ref, k_ref, v_ref, o_ref, m_ref, l_ref, acc_ref):
    qi = pl.program_id(1)
    _, nb, tq, _ = q_ref.shape
    n_part = tq // LANES
    row = lax.broadcasted_iota(jnp.int32, (tq, tq), 0)
    col = lax.broadcasted_iota(jnp.int32, (tq, tq), 1)
    units = [(hd, r) for r in range(nb) for hd in range(MLA_HEADS)]

    def scores(hd, r, rows, masked):
        s = _dot_nt(q_ref[hd, r], k_ref[hd, r, rows, :])
        if masked:
            s = jnp.where(col <= row, s, NEG_INF)
        return [s[:, c * LANES:(c + 1) * LANES] for c in range(n_part)]

    def chunk(j, masked):
        rows = pl.ds(pl.multiple_of(j * tq, tq), tq)
        nxt = scores(*units[0], rows, masked)
        for u, (hd, r) in enumerate(units):
            parts = nxt
            if u + 1 < len(units):
                nxt = scores(*units[u + 1], rows, masked)
            m_old = m_ref[u]
            row_max = jnp.max(functools.reduce(jnp.maximum, parts), axis=-1, keepdims=True)
            m_new = jnp.maximum(m_old, row_max)
            p = [jnp.exp2(part - m_new) for part in parts]
            corr = jnp.exp2(m_old - m_new)
            l_ref[u] = corr * l_ref[u] + functools.reduce(jnp.add, p)
            m_ref[u] = m_new
            prob = jnp.concatenate(p, axis=-1).astype(BF16)
            acc_ref[u] = corr[:, :MLA_V] * acc_ref[u] + _dot(prob, v_ref[hd, r, rows, :])

    def body(j, carry):
        chunk(j, False)
        return carry

    m_ref[...] = jnp.full_like(m_ref, NEG_INF)
    l_ref[...] = jnp.zeros_like(l_ref)
    acc_ref[...] = jnp.zeros_like(acc_ref)
    lax.fori_loop(0, qi, body, 0)
    chunk(qi, True)
    for r in range(nb):
        o_ref[r] = jnp.concatenate(
            [acc_ref[r * MLA_HEADS + hd] / jnp.sum(l_ref[r * MLA_HEADS + hd], axis=-1, keepdims=True)
             for hd in range(MLA_HEADS)], axis=-1)


def mla_mixer(h_mla, cos_m, sin_m, bsz, seq, prm, layer):
    t = bsz * seq
    tm = TM_PROJ
    row = lambda w: pl.BlockSpec((tm, w), lambda i: (i, 0))
    hrow = lambda w: pl.BlockSpec((MLA_HEADS, tm, w), lambda i: (0, i, 0))
    names = ("gq", "gkv", "wq", "wqr", "wk", "wv")
    q, k, v = pl.pallas_call(
        _mla_prep_kernel,
        grid=(t // tm,),
        in_specs=[row(W_MLA_IN), row(LANES), row(LANES)] + [_layer_spec(prm[n], layer) for n in names],
        out_specs=[hrow(MLA_QK_PAD), hrow(MLA_QK_PAD), hrow(MLA_V)],
        out_shape=[jax.ShapeDtypeStruct((MLA_HEADS, t, MLA_QK_PAD), BF16),
                   jax.ShapeDtypeStruct((MLA_HEADS, t, MLA_QK_PAD), BF16),
                   jax.ShapeDtypeStruct((MLA_HEADS, t, MLA_V), BF16)],
        compiler_params=_cparams("parallel"),
        name="mla_prep",
    )(h_mla, cos_m, sin_m, *[prm[n] for n in names])

    tq = TQ_ATT
    nb = NB_ATT
    assert bsz % nb == 0 and seq % tq == 0
    n_unit = nb * MLA_HEADS
    by_row = lambda a: a.reshape(MLA_HEADS, bsz, seq, a.shape[-1])
    q_spec = pl.BlockSpec((MLA_HEADS, nb, tq, MLA_QK_PAD), lambda b, i: (0, b, i, 0))
    k_spec = pl.BlockSpec((MLA_HEADS, nb, seq, MLA_QK_PAD), lambda b, i: (0, b, 0, 0))
    v_spec = pl.BlockSpec((MLA_HEADS, nb, seq, MLA_V), lambda b, i: (0, b, 0, 0))
    y = pl.pallas_call(
        _flash_kernel,
        grid=(bsz // nb, seq // tq),
        in_specs=[q_spec, k_spec, v_spec],
        out_specs=pl.BlockSpec((nb, tq, MLA_HEADS * MLA_V), lambda b, i: (b, i, 0)),
        out_shape=jax.ShapeDtypeStruct((bsz, seq, MLA_HEADS * MLA_V), F32),
        scratch_shapes=[pltpu.VMEM((n_unit, tq, LANES), F32), pltpu.VMEM((n_unit, tq, LANES), F32),
                        pltpu.VMEM((n_unit, tq, MLA_V), F32)],
        compiler_params=_cparams("parallel", "arbitrary"),
        name="mla_flash",
    )(by_row(q), by_row(k), by_row(v))
    return y.reshape(t, MLA_HEADS * MLA_V)


def _rot_half_cols(w, width):
    shp = w.shape
    w = w.reshape(shp[0], -1, 2, width // 2)
    return jnp.stack([-w[:, :, 1], w[:, :, 0]], axis=2).reshape(shp)


def mla_params(q_norm, kv_norm, w_uq, w_ukv):
    dq = MLA_NOPE + MLA_ROPE
    wq = jnp.zeros((MLA_HEADS, 256, MLA_QK_PAD), F32)
    wqr = jnp.zeros((MLA_HEADS, 256, MLA_QK_PAD), F32)
    w_uq_h = w_uq.reshape(MLA_Q_LORA, MLA_HEADS, dq).transpose(1, 0, 2)
    wq = wq.at[:, :MLA_Q_LORA, :dq].set(w_uq_h)
    pe = w_uq_h[:, :, MLA_NOPE:]
    pe_rot = jnp.concatenate([-pe[..., MLA_ROPE // 2:], pe[..., :MLA_ROPE // 2]], axis=-1)
    wqr = wqr.at[:, :MLA_Q_LORA, MLA_NOPE:dq].set(pe_rot)
    w_kv_h = w_ukv.reshape(MLA_KV_LORA, MLA_HEADS, MLA_NOPE + MLA_V).transpose(1, 0, 2)
    wk = jnp.zeros((MLA_HEADS, MLA_KV_LORA, MLA_QK_PAD), F32).at[:, :, :MLA_NOPE].set(w_kv_h[:, :, :MLA_NOPE])
    wv = w_kv_h[:, :, MLA_NOPE:]
    gq = jnp.zeros((1, 256), F32).at[0, :MLA_Q_LORA].set(q_norm)
    side_by_side = lambda w: w.transpose(1, 0, 2).reshape(w.shape[1], -1).astype(BF16)
    return {"wq": side_by_side(wq), "wqr": side_by_side(wqr), "wk": side_by_side(wk), "wv": side_by_side(wv),
            "gq": gq, "gkv": kv_norm.reshape(1, MLA_KV_LORA)}


def _head_mask(shape, hd, axis):
    return (lax.broadcasted_iota(jnp.int32, shape, axis) // HEAD_DIM) == hd


def _hgrn_kernel(h_ref, lb_ref, ng_ref, aall_ref, lvl_ref, hm_ref, blk_ref, o_ref, st_ref, e_ref):
    @pl.when(pl.program_id(1) == 0)
    def _():
        st_ref[...] = jnp.zeros_like(st_ref)

    for row in range(h_ref.shape[0]):
        _hgrn_tile(h_ref.at[row], lb_ref, ng_ref, aall_ref, lvl_ref, hm_ref, blk_ref, o_ref.at[row],
                   st_ref.at[row], e_ref.at[row])


def _hgrn_tile(h_ref, lb_ref, ng_ref, aall_ref, lvl_ref, hm_ref, blk_ref, o_ref, st_ref, e_ref):
    tt = h_ref.shape[0]
    width = HEADS * HEAD_DIM
    n_lev = aall_ref.shape[0] // tt - 1

    q = h_ref[:, 0:width]
    f = h_ref[:, width:2 * width]
    v = h_ref[:, 2 * width:3 * width]
    g = h_ref[:, 3 * width:4 * width]
    lb = lb_ref[...]
    log_sig = jnp.minimum(f, 0.0) - jnp.log1p(jnp.exp(-jnp.abs(f)))
    a = jnp.log(lb)
    b = jnp.log1p(-lb) + log_sig
    log_f = jnp.maximum(a, b) + jnp.log1p(jnp.exp(-jnp.abs(a - b)))
    k = (1.0 - lb) * _sigmoid(-f)
    qf = q * _sigmoid(q)
    e2 = _dot(aall_ref[...], jnp.concatenate(_split2(log_f), axis=-1))
    e_ref[...] = e2[:, 0:width] + e2[:, width:2 * width]

    t_row = lax.broadcasted_iota(jnp.int32, (tt, width), 0)
    lvl = lvl_ref[...]
    attn = [jnp.zeros((tt, tt), F32) for _ in range(HEADS)]
    for lev in range(n_lev):
        x = jnp.exp(e_ref[lev * tt:(lev + 1) * tt, :])
        right = ((t_row >> lev) & 1) == 1
        qt = jnp.where(right, qf * x, 0.0).astype(BF16)
        kt = jnp.where(right, 0.0, k * x).astype(BF16)
        q_heads = jnp.concatenate([qt * hm_ref[hd] for hd in range(HEADS)], axis=0)
        s = _dot_nt(q_heads, kt)
        here = lvl == lev
        for hd in range(HEADS):
            attn[hd] = jnp.where(here, s[hd * tt:(hd + 1) * tt, :], attn[hd])
    vb = v.astype(BF16)
    o = jnp.zeros((tt, width), F32)
    for hd in range(HEADS):
        o = o + _dot(attn[hd].astype(BF16), vb * hm_ref[hd])
    ones_blk = blk_ref[...]
    o = o + _dot_exact_rhs(qf * k, ones_blk) * v
    bc = e_ref[n_lev * tt:(n_lev + 1) * tt, :]
    st = st_ref[...]
    st_m = jnp.where(ones_blk > 0, st, 0.0).astype(BF16)
    o = o + _dot_nt((qf * jnp.exp(bc)).astype(BF16), st_m)
    b_last = bc[tt - 1:tt, :]
    k_out = (k * jnp.exp(b_last - bc)).astype(BF16)
    st_ref[...] = st * jnp.exp(b_last) + _dot_tn(vb, k_out)
    ms = _dot_exact_rhs(o * o, ones_blk) * (1.0 / HEAD_DIM)
    o = o * lax.rsqrt(ms + EPS) * ng_ref[...]
    o_ref[...] = o * (g * _sigmoid(g))


def _hgrn_level_matrices(tt):
    n_lev = int(math.log2(tt))
    t = np.arange(tt)[:, None]
    j = np.arange(tt)[None, :]
    mats = []
    for lev in range(n_lev):
        half = 1 << lev
        mid = (t >> (lev + 1) << (lev + 1)) + half
        right = ((t >> lev) & 1) == 1
        mats.append(np.where(right, (j >= mid) & (j <= t), (j > t) & (j < mid)).astype(np.float32))
    mats.append((j <= t).astype(np.float32))
    level = np.full((tt, tt), -1, np.int32)
    lower = np.broadcast_to(j < t, (tt, tt))
    level[lower] = np.floor(np.log2((t ^ j)[lower])).astype(np.int32)
    return jnp.asarray(np.concatenate(mats, axis=0), BF16), jnp.asarray(level)


def _head_block_ones():
    i = np.arange(HEADS * HEAD_DIM)
    return jnp.asarray((i[:, None] // HEAD_DIM == i[None, :] // HEAD_DIM).astype(np.float32), BF16)


def _head_lane_masks(rows):
    i = np.arange(HEADS * HEAD_DIM)
    m = (i[None, :] // HEAD_DIM == np.arange(HEADS)[:, None]).astype(np.float32)
    return jnp.asarray(np.broadcast_to(m[:, None, :], (HEADS, rows, HEADS * HEAD_DIM)), BF16)


def hgrn2_mixer(h_hg, lb, norm_g, bsz, seq, layer):
    t = bsz * seq
    tt = TT_HG
    width = HEADS * HEAD_DIM
    aall, level = _hgrn_level_matrices(tt)
    nt = seq // tt
    nb = NB_HG
    assert bsz % nb == 0 and seq % tt == 0
    y = pl.pallas_call(
        _hgrn_kernel,
        grid=(bsz // nb, nt),
        in_specs=[pl.BlockSpec((nb, tt, 4 * width), lambda b, i: (b, i, 0)),
                  _layer_spec(lb, layer), _layer_spec(norm_g, layer),
                  _const_spec(aall.shape), _const_spec(level.shape), _const_spec((HEADS, tt, width)),
                  _const_spec((width, width))],
        out_specs=pl.BlockSpec((nb, tt, width), lambda b, i: (b, i, 0)),
        out_shape=jax.ShapeDtypeStruct((bsz, seq, width), F32),
        scratch_shapes=[pltpu.VMEM((nb, width, width), F32), pltpu.VMEM((nb, aall.shape[0], width), F32)],
        compiler_params=_cparams("parallel", "arbitrary"),
        name="hgrn2_mixer",
    )(h_hg.reshape(bsz, seq, 4 * width), lb, norm_g, aall, level, _head_lane_masks(tt), _head_block_ones())
    return y.reshape(t, width)


def _ret_kernel(h_ref, cos_ref, sin_ref, dmat_ref, qdec_ref, kdec_ref, cdec_ref, blk_ref, g_ref, b_ref,
                o_ref, st_ref):
    @pl.when(pl.program_id(1) == 0)
    def _():
        st_ref[...] = jnp.zeros_like(st_ref)

    for row in range(h_ref.shape[0]):
        _ret_tile(h_ref.at[row], cos_ref.at[row], sin_ref.at[row], dmat_ref, qdec_ref, kdec_ref, cdec_ref,
                  blk_ref, g_ref, b_ref, o_ref.at[row], st_ref.at[row])


def _ret_tile(h_ref, cos_ref, sin_ref, dmat_ref, qdec_ref, kdec_ref, cdec_ref, blk_ref, g_ref, b_ref,
              o_ref, st_ref):
    tt = h_ref.shape[0]
    width = HEADS * HEAD_DIM

    cos = cos_ref[...]
    sin = sin_ref[...]
    q = (h_ref[:, 0:width] * cos + h_ref[:, 4 * width:5 * width] * sin) * (HEAD_DIM ** -0.5)
    k = h_ref[:, width:2 * width] * cos + h_ref[:, 5 * width:6 * width] * sin
    v = h_ref[:, 2 * width:3 * width]
    g = h_ref[:, 3 * width:4 * width]
    kb = k.astype(BF16)
    vb = v.astype(BF16)
    ones_blk = blk_ref[...]
    o = jnp.zeros((tt, width), F32)
    for hd in range(HEADS):
        hm = _head_mask((tt, width), hd, 1)
        s = _dot_nt(jnp.where(hm, q, 0.0).astype(BF16), kb) * dmat_ref[hd]
        o = o + _dot(s.astype(BF16), jnp.where(hm, vb, jnp.zeros_like(vb)))
    st = st_ref[...]
    st_m = jnp.where(ones_blk > 0, st, 0.0).astype(BF16)
    o = o + _dot((q * qdec_ref[...]).astype(BF16), st_m)
    st_ref[...] = st * cdec_ref[...] + _dot_tn((k * kdec_ref[...]).astype(BF16), vb)
    mu = _dot_exact_rhs(o, ones_blk) * (1.0 / HEAD_DIM)
    oc = o - mu
    var = _dot_exact_rhs(oc * oc, ones_blk) * (1.0 / HEAD_DIM)
    o = oc * lax.rsqrt(var + EPS) * g_ref[...] + b_ref[...]
    o_ref[...] = (g * _sigmoid(g)) * o


def _retention_tables(tt):
    log_gamma = jnp.log(1.0 - 2.0 ** (-5.0 - jnp.arange(HEADS, dtype=F32)))
    t_idx = jnp.arange(tt, dtype=F32)
    rel = t_idx[:, None] - t_idx[None, :]
    dmat = jnp.where(rel >= 0, jnp.exp(jnp.maximum(rel, 0.0)[None] * log_gamma[:, None, None]), 0.0)
    lanes = lambda a: jnp.repeat(a, HEAD_DIM, axis=-1)
    qdec = lanes(jnp.exp((t_idx + 1.0)[:, None] * log_gamma[None, :]))
    kdec = lanes(jnp.exp((tt - 1 - t_idx)[:, None] * log_gamma[None, :]))
    cdec = lanes(jnp.exp(tt * log_gamma)[None, :])
    return dmat, qdec, kdec, cdec


def retention_mixer(h_ret, cos_r, sin_r, gn_g, gn_b, bsz, seq, layer):
    t = bsz * seq
    tt = TT_RET
    width = HEADS * HEAD_DIM
    nt = seq // tt
    dmat, qdec, kdec, cdec = _retention_tables(tt)
    nb = NB_RET
    assert bsz % nb == 0 and seq % tt == 0
    row = lambda w: pl.BlockSpec((nb, tt, w), lambda b, i: (b, i, 0))
    y = pl.pallas_call(
        _ret_kernel,
        grid=(bsz // nb, nt),
        in_specs=[row(W_RET_IN), row(width), row(width), _const_spec((HEADS, tt, tt)), _const_spec((tt, width)),
                  _const_spec((tt, width)), _const_spec((1, width)), _const_spec((width, width)),
                  _layer_spec(gn_g, layer), _layer_spec(gn_b, layer)],
        out_specs=row(width),
        out_shape=jax.ShapeDtypeStruct((bsz, seq, width), F32),
        scratch_shapes=[pltpu.VMEM((nb, width, width), F32)],
        compiler_params=_cparams("parallel", "arbitrary"),
        name="retention_mixer",
    )(h_ret.reshape(bsz, seq, W_RET_IN), cos_r.reshape(bsz, seq, width), sin_r.reshape(bsz, seq, width),
      dmat, qdec, kdec, cdec, _head_block_ones(), gn_g, gn_b)
    return y.reshape(t, width)


def _store_token_tiles(ref, val):
    tm = val.shape[0]
    for j in range(D_MODEL // LANES):
        ref[pl.ds(j, tm, stride=SUBLANES), :] = val[:, j * LANES:(j + 1) * LANES]


def _load_token_tiles(ref, tm):
    return jnp.concatenate([ref[pl.ds(j, tm, stride=SUBLANES), :] for j in range(D_MODEL // LANES)], axis=-1)


def _mix_kernel(x_ref, ya_ref, yb_ref, yc_ref, yd_ref, wg_ref, wb_ref, wo_ref, g_ref, b_ref, o_ref):
    x = x_ref[...]
    xb = x.astype(BF16)
    acc = jnp.zeros(x.shape, F32)
    for n, y_ref in enumerate((ya_ref, yb_ref, yc_ref, yd_ref)):
        gate = _sigmoid(_dot(xb, wg_ref[:, n * D_MODEL:(n + 1) * D_MODEL]))
        acc = acc + gate * _dot(y_ref[...].astype(BF16), wb_ref[n])
    z = ALPHA * x + _dot(acc.astype(BF16), wo_ref[...])
    o_ref[...] = _layer_norm_rows(z, g_ref[...], b_ref[...])


def mix_branches(x, ys, params, layer):
    t = x.shape[0]
    tm = TM_PROJ
    row = lambda w: pl.BlockSpec((tm, w), lambda i: (i, 0))
    return pl.pallas_call(
        _mix_kernel,
        grid=(t // tm,),
        in_specs=[row(D_MODEL)] + [row(MIX_WIDTH)] * 4 + [_layer_spec(a, layer) for a in params],
        out_specs=row(D_MODEL),
        out_shape=jax.ShapeDtypeStruct((t, D_MODEL), F32),
        compiler_params=_cparams("parallel"),
        name="mix_branches",
    )(x, *ys, *params)


def _ffn_kernel(x_ref, wg_ref, wu_ref, wd_ref, o_ref, xb_ref, acc_ref):
    c = pl.program_id(1)

    @pl.when(c == 0)
    def _():
        xb_ref[...] = x_ref[...].astype(BF16)
        acc_ref[...] = jnp.zeros_like(acc_ref)

    xb = xb_ref[...]
    gate = _dot(xb, wg_ref[...])
    act = gate * _sigmoid(gate) * _dot(xb, wu_ref[...])
    acc_ref[...] += _dot(act.astype(BF16), wd_ref[...])

    @pl.when(c == pl.num_programs(1) - 1)
    def _():
        o_ref[...] = acc_ref[...]


def dense_ffn(x, w_gate, w_up, w_down, layer):
    t = x.shape[0]
    tm, tf = TM_FFN, TF_FFN
    return pl.pallas_call(
        _ffn_kernel,
        grid=(t // tm, D_FF // tf),
        in_specs=[pl.BlockSpec((tm, D_MODEL), lambda i, c: (i, 0)),
                  pl.BlockSpec((None, D_MODEL, tf), lambda i, c: (layer, 0, c)),
                  pl.BlockSpec((None, D_MODEL, tf), lambda i, c: (layer, 0, c)),
                  pl.BlockSpec((None, tf, D_MODEL), lambda i, c: (layer, c, 0))],
        out_specs=pl.BlockSpec((tm, D_MODEL), lambda i, c: (i, 0)),
        out_shape=jax.ShapeDtypeStruct((t, D_MODEL), F32),
        scratch_shapes=[pltpu.VMEM((tm, D_MODEL), BF16), pltpu.VMEM((tm, D_MODEL), F32)],
        compiler_params=_cparams("parallel", "arbitrary"),
        name="dense_ffn",
    )(x, w_gate, w_up, w_down)


def _router_kernel(x_ref, wh_ref, wl_ref, upper_ref, meta_ref, cnt_ref, carry_ref):
    tm = x_ref.shape[0]

    @pl.when(pl.program_id(0) == 0)
    def _():
        carry_ref[...] = jnp.zeros_like(carry_ref)

    xh, xl = _split2(x_ref[...])
    wh = wh_ref[...]
    logits = _dot_nt(wh, xh) + _dot_nt(wh, xl) + _dot_nt(wl_ref[...], xh)
    e_idx = lax.broadcasted_iota(jnp.int32, logits.shape, 0)
    m1 = jnp.max(logits, axis=0, keepdims=True)
    i1 = jnp.min(jnp.where(logits == m1, e_idx, N_EXPERTS), axis=0, keepdims=True)
    rest = jnp.where(e_idx == i1, -jnp.inf, logits)
    m2 = jnp.max(rest, axis=0, keepdims=True)
    i2 = jnp.min(jnp.where(rest == m2, e_idx, N_EXPERTS), axis=0, keepdims=True)
    ex = jnp.exp(m2 - m1)
    w1 = 1.0 / (1.0 + ex)
    w2 = ex / (1.0 + ex)
    sel1 = e_idx == i1
    sel2 = e_idx == i2
    onehot = jnp.where(sel1 | sel2, 1.0, 0.0)
    carry = carry_ref[...]
    ranks = _dot(onehot.astype(BF16), upper_ref[...]) + carry[:, 0:1]
    r1 = jnp.sum(jnp.where(sel1, ranks, 0.0), axis=0, keepdims=True)
    r2 = jnp.sum(jnp.where(sel2, ranks, 0.0), axis=0, keepdims=True)
    row = lax.broadcasted_iota(jnp.int32, (SUBLANES, tm), 0)
    vals = (i1.astype(F32), i2.astype(F32), w1, w2, r1, r2)
    meta = jnp.zeros((SUBLANES, tm), F32)
    for n, val in enumerate(vals):
        meta = jnp.where(row == n, val, meta)
    meta_ref[...] = meta
    carry = carry + jnp.sum(onehot, axis=1, keepdims=True)
    carry_ref[...] = carry
    cnt_ref[...] = carry


def moe_router(x, w_router):
    t = x.shape[0]
    tm = TM_ROUTE
    wt = w_router.T
    wh = wt.astype(BF16)
    wl = (wt - wh.astype(F32)).astype(BF16)
    upper = jnp.asarray(np.triu(np.ones((tm, tm), np.float32), 1), BF16)
    return pl.pallas_call(
        _router_kernel,
        grid=(t // tm,),
        in_specs=[pl.BlockSpec((tm, D_MODEL), lambda i: (i, 0)), _const_spec((N_EXPERTS, D_MODEL)),
                  _const_spec((N_EXPERTS, D_MODEL)), _const_spec((tm, tm))],
        out_specs=[pl.BlockSpec((SUBLANES, tm), lambda i: (0, i)), pl.BlockSpec((N_EXPERTS, LANES), lambda i: (0, 0))],
        out_shape=[jax.ShapeDtypeStruct((SUBLANES, t), F32), jax.ShapeDtypeStruct((N_EXPERTS, LANES), F32)],
        scratch_shapes=[pltpu.VMEM((N_EXPERTS, LANES), F32)],
        compiler_params=_cparams("arbitrary"),
        name="moe_router",
    )(x, wh, wl, upper)


def _token_rows(ref, tok):
    return ref.at[pl.ds(pl.multiple_of(tok * SUBLANES, SUBLANES), SUBLANES), :]


def _start_then_wait(n, copies):
    def start(r, carry):
        for idx, cp in enumerate(copies(r)):
            cp.start(priority=idx % 2)
        return carry

    def wait(r, carry):
        for cp in copies(r):
            cp.wait()
        return carry

    lax.fori_loop(0, n, start, 0, unroll=DMA_UNROLL)
    return lambda: lax.fori_loop(0, n, wait, 0, unroll=DMA_UNROLL)


def _dispatch_kernel(dest_ref, x_ref, init_ref, xs_ref, buf_ref, sem):
    del init_ref
    td = x_ref.shape[0]
    _store_token_tiles(buf_ref, x_ref[...])

    def copies(r):
        src = _token_rows(buf_ref, r)
        return (pltpu.make_async_copy(src, _token_rows(xs_ref, dest_ref[0, 0, 2 * r]), sem),
                pltpu.make_async_copy(src, _token_rows(xs_ref, dest_ref[0, 0, 2 * r + 1]), sem))

    _start_then_wait(td, copies)()


def moe_dispatch(x, dest3, p_len):
    t = x.shape[0]
    td = dest3.shape[2] // 2
    init = jnp.zeros((p_len * SUBLANES, LANES), F32)
    return pl.pallas_call(
        _dispatch_kernel,
        grid=(t // td,),
        in_specs=[pl.BlockSpec((1, 1, 2 * td), lambda i: (i, 0, 0), memory_space=pltpu.SMEM),
                  pl.BlockSpec((td, D_MODEL), lambda i: (i, 0)), pl.BlockSpec(memory_space=pl.ANY)],
        out_specs=pl.BlockSpec(memory_space=pl.ANY),
        out_shape=jax.ShapeDtypeStruct((p_len * SUBLANES, LANES), F32),
        scratch_shapes=[pltpu.VMEM((td * SUBLANES, LANES), F32), pltpu.SemaphoreType.DMA(())],
        input_output_aliases={2: 0},
        compiler_params=pltpu.CompilerParams(dimension_semantics=("arbitrary",), has_side_effects=True),
        name="moe_dispatch",
    )(dest3, x, init)


def _expert_kernel(te_ref, nv_ref, tr_ref, xs_ref, wg_ref, wu_ref, wd_ref, ys_ref, xb_ref, acc_ref):
    i = pl.program_id(0)
    c = pl.program_id(1)
    tm = xb_ref.shape[0]
    valid = i < nv_ref[0]
    upper_used = tr_ref[i] > tm // 2

    @pl.when(valid & (c == 0))
    def _():
        xb_ref[...] = _load_token_tiles(xs_ref, tm).astype(BF16)
        acc_ref[...] = jnp.zeros_like(acc_ref)

    def swiglu_rows(m):
        xb = xb_ref[:m, :]
        gate = _dot(xb, wg_ref[...].astype(BF16))
        act = gate * _sigmoid(gate) * _dot(xb, wu_ref[...].astype(BF16))
        acc_ref[:m, :] += _dot(act.astype(BF16), wd_ref[...].astype(BF16))

    @pl.when(valid & upper_used)
    def _():
        swiglu_rows(tm)

    @pl.when(valid & jnp.logical_not(upper_used))
    def _():
        swiglu_rows(tm // 2)

    @pl.when(c == pl.num_programs(1) - 1)
    def _():
        _store_token_tiles(ys_ref, jnp.where(valid, acc_ref[...], 0.0))


def moe_experts(xs, tile_expert, n_valid, tile_rows, layer, w_gate, w_up, w_down):
    tm, tf = TM_MOE, TF_FFN
    n_tiles = xs.shape[0] // (tm * SUBLANES)
    nc = D_FF // tf

    def chunk(i, c, nv):
        return jnp.where(i < nv[0], c, nc - 1)

    grid_spec = pltpu.PrefetchScalarGridSpec(
        num_scalar_prefetch=3,
        grid=(n_tiles, nc),
        in_specs=[pl.BlockSpec((tm * SUBLANES, LANES), lambda i, c, te, nv, tr: (jnp.minimum(i, nv[0] - 1), 0)),
                  pl.BlockSpec((None, None, D_MODEL, tf), lambda i, c, te, nv, tr: (layer, te[i], 0, chunk(i, c, nv))),
                  pl.BlockSpec((None, None, D_MODEL, tf), lambda i, c, te, nv, tr: (layer, te[i], 0, chunk(i, c, nv))),
                  pl.BlockSpec((None, None, tf, D_MODEL), lambda i, c, te, nv, tr: (layer, te[i], chunk(i, c, nv), 0))],
        out_specs=pl.BlockSpec((tm * SUBLANES, LANES), lambda i, c, te, nv, tr: (i, 0)),
        scratch_shapes=[pltpu.VMEM((tm, D_MODEL), BF16), pltpu.VMEM((tm, D_MODEL), F32)],
    )
    return pl.pallas_call(
        _expert_kernel,
        grid_spec=grid_spec,
        out_shape=jax.ShapeDtypeStruct(xs.shape, F32),
        compiler_params=_cparams("arbitrary", "arbitrary"),
        name="moe_experts",
    )(tile_expert, n_valid, tile_rows, xs, w_gate, w_up, w_down)


def moe_ffn(x, w_router, layer, w_gate, w_up, w_down):
    t = x.shape[0]
    tm = TM_MOE
    meta, cnt = moe_router(x, w_router)
    counts = cnt[:, 0].astype(jnp.int32)
    padded = (counts + tm - 1) // tm * tm
    pad_end = jnp.cumsum(padded)
    pad_start = pad_end - padded
    experts = meta[0:2].astype(jnp.int32)
    start_of = jnp.sum(jnp.where(experts[..., None] == jnp.arange(N_EXPERTS), pad_start, 0), axis=-1)
    dest = (start_of + meta[4:6].astype(jnp.int32)).T
    dest3 = dest.reshape(t // TD_DISPATCH, 1, 2 * TD_DISPATCH)
    n_tiles = (2 * t) // tm + N_EXPERTS
    n_valid = (pad_end[-1] // tm).astype(jnp.int32).reshape(1)
    tile_start = jnp.arange(n_tiles, dtype=jnp.int32) * tm
    tile_expert = jnp.minimum(jnp.sum(pad_end[None, :] <= tile_start[:, None], axis=1), N_EXPERTS - 1)
    last_expert = jnp.sum(jnp.where(jnp.arange(n_tiles) == n_valid[0] - 1, tile_expert, 0))
    tile_expert = jnp.where(jnp.arange(n_tiles) < n_valid[0], tile_expert, last_expert).astype(jnp.int32)
    of_tile = lambda per_expert: jnp.sum(jnp.where(tile_expert[:, None] == jnp.arange(N_EXPERTS), per_expert, 0), -1)
    tile_rows = jnp.clip(of_tile(pad_start + counts) - tile_start, 0, tm).astype(jnp.int32)
    xs = moe_dispatch(x, dest3, n_tiles * tm)
    ys = moe_experts(xs, tile_expert, n_valid, tile_rows, layer, w_gate, w_up, w_down)
    return ys, dest3, meta[2:4].T


def _post_kernel(x_ref, p_ref, wpg_ref, wpp_ref, g_ref, b_ref, f_ref, o_ref):
    x = x_ref[...]
    ple = _sigmoid(_dot(x.astype(BF16), wpg_ref[...])) * _dot(p_ref[...].astype(BF16), wpp_ref[...])
    o_ref[...] = _layer_norm_rows(ALPHA * x + f_ref[...] + ple, g_ref[...], b_ref[...])


def _post_moe_kernel(dest_ref, x_ref, p_ref, wpg_ref, wpp_ref, g_ref, b_ref, w_ref, ys_ref, o_ref,
                     buf0_ref, buf1_ref, sem):
    x = x_ref[...]
    tm = x.shape[0]

    def copies(r):
        return (pltpu.make_async_copy(_token_rows(ys_ref, dest_ref[0, 0, 2 * r]), _token_rows(buf0_ref, r), sem),
                pltpu.make_async_copy(_token_rows(ys_ref, dest_ref[0, 0, 2 * r + 1]), _token_rows(buf1_ref, r), sem))

    wait_all = _start_then_wait(tm, copies)
    ple = _sigmoid(_dot(x.astype(BF16), wpg_ref[...])) * _dot(p_ref[...].astype(BF16), wpp_ref[...])
    wait_all()
    w = w_ref[...]
    f = w[:, 0:1] * _load_token_tiles(buf0_ref, tm) + w[:, 1:2] * _load_token_tiles(buf1_ref, tm)
    o_ref[...] = _layer_norm_rows(ALPHA * x + f + ple, g_ref[...], b_ref[...])


def post_layer(x, p, layer, params, f=None, moe=None):
    t = x.shape[0]
    tm = TM_PROJ if moe is None else TD_DISPATCH
    row = lambda w: pl.BlockSpec((tm, w), lambda i: (i, 0))
    in_specs = [row(D_MODEL), pl.BlockSpec((None, tm, PLE_DIM), lambda i: (layer, i, 0))]
    in_specs += [_layer_spec(a, layer) for a in params]
    args = [x, p, *params]
    scratch = []
    if moe is None:
        body = _post_kernel
        in_specs += [row(D_MODEL)]
        args += [f]
    else:
        body = _post_moe_kernel
        ys, dest3, w2 = moe
        in_specs = [pl.BlockSpec((1, 1, 2 * tm), lambda i: (i, 0, 0), memory_space=pltpu.SMEM)] + in_specs
        in_specs += [row(2), pl.BlockSpec(memory_space=pl.ANY)]
        args = [dest3] + args + [w2, ys]
        scratch = [pltpu.VMEM((tm * SUBLANES, LANES), F32), pltpu.VMEM((tm * SUBLANES, LANES), F32),
                   pltpu.SemaphoreType.DMA(())]
    return pl.pallas_call(
        body,
        grid=(t // tm,),
        in_specs=in_specs,
        out_specs=row(D_MODEL),
        out_shape=jax.ShapeDtypeStruct((t, D_MODEL), F32),
        scratch_shapes=scratch,
        compiler_params=_cparams("parallel" if moe is None else "arbitrary"),
        name="post_layer",
    )(*args)


def split_in_weights(w):
    sizes = (MIX_WIDTH, MLA_Q_LORA, MLA_KV_LORA, MLA_ROPE) + (HEADS * HEAD_DIM,) * 8 + (N_BRANCH * D_MODEL,)
    parts, start = [], 0
    for n in sizes:
        parts.append(w[:, start:start + n])
        start += n
    u_s5, c_q, c_kv, k_rope, hq, hf, hi, hg, rq, rk, rv, rg, gate = parts
    k_rot = jnp.concatenate([-k_rope[:, MLA_ROPE // 2:], k_rope[:, :MLA_ROPE // 2]], axis=1)
    w_mla = jnp.zeros((D_MODEL, W_MLA_IN), F32)
    w_mla = w_mla.at[:, 0:MLA_Q_LORA].set(c_q).at[:, 256:384].set(c_kv)
    w_mla = w_mla.at[:, 384 + MLA_NOPE:384 + MLA_NOPE + MLA_ROPE].set(k_rope)
    w_mla = w_mla.at[:, 512 + MLA_NOPE:512 + MLA_NOPE + MLA_ROPE].set(k_rot)
    w_hg = jnp.concatenate([hq, hf, hi, hg], axis=1)
    w_ret = jnp.concatenate([rq, rk, rv, rg, _rot_half_cols(rq, HEAD_DIM), _rot_half_cols(rk, HEAD_DIM)], axis=1)
    return [a.astype(BF16) for a in (w_mla, u_s5, w_hg, w_ret)], gate.astype(BF16)


def kernel(x, p, positions, w_in, s5_a_re, s5_a_im, s5_log_dt, s5_b_re, s5_b_im, s5_c_re, s5_c_im, s5_d, s5_w_glu, mla_q_norm, mla_kv_norm, mla_w_uq, mla_w_ukv, hg_lb_raw, hg_norm, ret_gn_g, ret_gn_b, w_branch, w_o, ln1_g, ln1_b, ff_w_gate, ff_w_up, ff_w_down, moe_router, moe_w_gate, moe_w_up, moe_w_down, ple_w_gate, ple_w_proj, ln2_g, ln2_b):
    bsz, seq, _ = x.shape
    t = bsz * seq
    depth = w_in.shape[0]
    x = x.reshape(t, D_MODEL)
    p = p.reshape(depth, t, PLE_DIM)
    cos_m, sin_m, cos_r, sin_r = rope_tables(positions)
    lb_all = jnp.cumsum(jax.nn.softmax(hg_lb_raw.astype(F32), axis=0), axis=0)
    lb_all = lb_all - lb_all[0]
    rows = lambda a: a.astype(F32).reshape(depth, 1, -1)
    in_ws, w_gate = jax.vmap(split_in_weights)(w_in)
    s5_prm = jax.vmap(s5_params)(s5_a_re, s5_a_im, s5_log_dt, s5_b_re, s5_b_im, s5_c_re, s5_c_im, s5_d, s5_w_glu)
    mla_prm = jax.vmap(mla_params)(mla_q_norm, mla_kv_norm, mla_w_uq, mla_w_ukv)
    mix_prm = (w_gate, w_branch.astype(BF16), w_o.astype(BF16), rows(ln1_g), rows(ln1_b))
    post_prm = (ple_w_gate.astype(BF16), ple_w_proj.astype(BF16), rows(ln2_g), rows(ln2_b))
    ffn_w = (ff_w_gate.astype(BF16), ff_w_up.astype(BF16), ff_w_down.astype(BF16))
    lb_rows, hg_norm_rows, gn_g_rows, gn_b_rows = rows(lb_all), rows(hg_norm), rows(ret_gn_g), rows(ret_gn_b)
    for i in range(depth):
        h_mla, h_s5, h_hg, h_ret = in_projection(x, in_ws, i)
        y_a = s5_mixer(h_s5, bsz, seq, s5_prm, i)
        y_b = mla_mixer(h_mla, cos_m, sin_m, bsz, seq, mla_prm, i)
        y_c = hgrn2_mixer(h_hg, lb_rows, hg_norm_rows, bsz, seq, i)
        y_d = retention_mixer(h_ret, cos_r, sin_r, gn_g_rows, gn_b_rows, bsz, seq, i)
        x = mix_branches(x, (y_a, y_b, y_c, y_d), mix_prm, i)
        if i % 2 == 1:
            x = post_layer(x, p, i, post_prm, moe=moe_ffn(x, moe_router[i // 2], i // 2, moe_w_gate, moe_w_up,
                                                           moe_w_down))
        else:
            x = post_layer(x, p, i, post_prm, f=dense_ffn(x, *ffn_w, i // 2))
    return x.reshape(bsz, seq, D_MODEL)
```

```python
import functools
import math

import numpy as np
import jax
import jax.numpy as jnp
from jax import lax
from jax.experimental import pallas as pl
from jax.experimental.pallas import tpu as pltpu

F32 = jnp.float32
BF16 = jnp.bfloat16

D_MODEL = 1024
DEPTH = 4
PLE_DIM = 256
MIX_WIDTH = 256
N_BRANCH = 4
S5_GROUP = 16
S5_GROUPS = 16
S5_STATE = 64
MLA_HEADS = 4
MLA_NOPE = 64
MLA_ROPE = 32
MLA_V = 64
MLA_Q_LORA = 192
MLA_KV_LORA = 128
HEADS = 4
HEAD_DIM = 64
D_FF = 3584
N_EXPERTS = 8
ROPE_BASE = 10000.0
EPS = 1e-5
NEG_INF = -1e30
ALPHA = (2 * DEPTH) ** 0.25

LANES = 128
SUBLANES = 8
VMEM_LIMIT = 56 * 1024 * 1024

TM_PROJ = 512
TS_S5 = 64
TQ_ATT = 256
NB_ATT = 2
TT_HG = 128
NB_HG = 4
TT_RET = 256
NB_RET = 4
TM_FFN = 1024
TF_FFN = 512
TM_MOE = 1024
TM_ROUTE = 512
TD_DISPATCH = 512
DMA_UNROLL = 8


def _cparams(*sem):
    return pltpu.CompilerParams(dimension_semantics=sem, vmem_limit_bytes=VMEM_LIMIT)


def _const_spec(shape):
    nd = len(shape)
    return pl.BlockSpec(shape, lambda *_: (0,) * nd, pipeline_mode=pl.Buffered(1))


def _layer_spec(arr, layer):
    shape = arr.shape[1:]
    nd = len(shape)
    return pl.BlockSpec((None,) + shape, lambda *_: (layer,) + (0,) * nd, pipeline_mode=pl.Buffered(1))


def _dot(a, b):
    return jnp.dot(a, b, preferred_element_type=F32)


def _dot_nt(a, b):
    return lax.dot_general(a, b, (((1,), (1,)), ((), ())), preferred_element_type=F32)


def _dot_tn(a, b):
    return lax.dot_general(a, b, (((0,), (0,)), ((), ())), preferred_element_type=F32)


def _split2(x):
    hi = x.astype(BF16)
    lo = (x - hi.astype(F32)).astype(BF16)
    return hi, lo


def _split3(x):
    h1 = x.astype(BF16)
    r1 = x - h1.astype(F32)
    h2 = r1.astype(BF16)
    h3 = (r1 - h2.astype(F32)).astype(BF16)
    return h1, h2, h3


def _dot_exact_rhs(x, m):
    hi, lo = _split2(x)
    return _dot(hi, m) + _dot(lo, m)


def _dot_exact_lhs(m, x):
    hi, lo = _split2(x)
    return _dot(m, hi) + _dot(m, lo)


def _sigmoid(x):
    return 1.0 / (1.0 + jnp.exp(-x))


def _layer_norm_rows(z, g, b):
    mu = jnp.mean(z, axis=-1, keepdims=True)
    zc = z - mu
    var = jnp.mean(zc * zc, axis=-1, keepdims=True)
    return zc * lax.rsqrt(var + EPS) * g + b


_HALF_M = MLA_ROPE // 2
_HALF_R = HEAD_DIM // 2


def _rope_kernel(pos_ref, freq_ref, cm_ref, sm_ref, cr_ref, sr_ref):
    ang = pos_ref[...] * freq_ref[...]
    lane = lax.broadcasted_iota(jnp.int32, ang.shape, 1)

    def mla_layout(table, nope_value):
        first = pltpu.roll(table, MLA_NOPE, axis=1)
        second = pltpu.roll(table, MLA_NOPE + _HALF_M, axis=1)
        return jnp.where(lane < MLA_NOPE, nope_value,
                         jnp.where(lane < MLA_NOPE + _HALF_M, first,
                                   jnp.where(lane < MLA_NOPE + MLA_ROPE, second, 0.0)))

    def ret_layout(table):
        quarter = [pltpu.roll(table, (q * _HALF_R - _HALF_M) % LANES, axis=1) for q in range(LANES // _HALF_R)]
        half = jnp.where(lane < _HALF_R, quarter[0],
                         jnp.where(lane < 2 * _HALF_R, quarter[1],
                                   jnp.where(lane < 3 * _HALF_R, quarter[2], quarter[3])))
        return jnp.concatenate([half, half], axis=1)

    cos, sin = jnp.cos(ang), jnp.sin(ang)
    cm_ref[...] = mla_layout(cos, 1.0)
    sm_ref[...] = mla_layout(sin, 0.0)
    cr_ref[...] = ret_layout(cos)
    sr_ref[...] = ret_layout(sin)


def rope_tables(positions):
    t = positions.size
    tm = TM_PROJ
    pos = positions.reshape(t, 1).astype(F32)
    inv_m = ROPE_BASE ** (-jnp.arange(_HALF_M, dtype=F32) / _HALF_M)
    inv_r = ROPE_BASE ** (-jnp.arange(_HALF_R, dtype=F32) / _HALF_R)
    freq = jnp.concatenate([inv_m, inv_r, jnp.zeros((LANES - _HALF_M - _HALF_R,), F32)]).reshape(1, LANES)
    row = lambda w: pl.BlockSpec((tm, w), lambda i: (i, 0))
    return pl.pallas_call(
        _rope_kernel,
        grid=(t // tm,),
        in_specs=[row(1), _const_spec((1, LANES))],
        out_specs=[row(LANES), row(LANES), row(256), row(256)],
        out_shape=[jax.ShapeDtypeStruct((t, LANES), F32), jax.ShapeDtypeStruct((t, LANES), F32),
                   jax.ShapeDtypeStruct((t, 256), F32), jax.ShapeDtypeStruct((t, 256), F32)],
        compiler_params=_cparams("parallel"),
        name="rope_tables",
    )(pos, freq)


W_MLA_IN = 640
W_RET_IN = 1536
_N_CHUNK = 256


def _inproj_kernel(x_ref, *refs):
    n = len(refs) // 2
    xb = x_ref[...].astype(BF16)
    for w_ref, o_ref in zip(refs[:n], refs[n:]):
        width = w_ref.shape[1]
        for c0 in range(0, width, _N_CHUNK):
            c1 = min(c0 + _N_CHUNK, width)
            o_ref[:, c0:c1] = _dot(xb, w_ref[:, c0:c1])


def in_projection(x, weights, layer):
    t = x.shape[0]
    tm = TM_PROJ
    return pl.pallas_call(
        _inproj_kernel,
        grid=(t // tm,),
        in_specs=[pl.BlockSpec((tm, D_MODEL), lambda i: (i, 0))] + [_layer_spec(w, layer) for w in weights],
        out_specs=[pl.BlockSpec((tm, w.shape[2]), lambda i: (i, 0)) for w in weights],
        out_shape=[jax.ShapeDtypeStruct((t, w.shape[2]), F32) for w in weights],
        compiler_params=_cparams("parallel"),
        name="in_projection",
    )(x, *weights)


S5_HALF = S5_GROUPS * S5_STATE


def _s5_kernel(u_ref, perm_ref, permt_ref, bmat_ref, ar_ref, ai_ref, cmat_ref, d_ref, wglu_ref, y_ref,
               bu_ref, hs_ref, h_ref):
    nb, ts, width = u_ref.shape
    n_half = S5_HALF // LANES

    @pl.when(pl.program_id(0) == 0)
    def _():
        h_ref[...] = jnp.zeros_like(h_ref)

    u = u_ref[...].reshape(nb * ts, width)
    u_tm = _dot(perm_ref[...], u.astype(BF16)).astype(BF16)
    bu_ref[...] = _dot(u_tm, bmat_ref[...])
    ar = [jnp.broadcast_to(ar_ref[:, c * LANES:(c + 1) * LANES], (nb, LANES)) for c in range(n_half)]
    ai = [jnp.broadcast_to(ai_ref[:, c * LANES:(c + 1) * LANES], (nb, LANES)) for c in range(n_half)]
    h_re = [h_ref[:, c * LANES:(c + 1) * LANES] for c in range(n_half)]
    h_im = [h_ref[:, S5_HALF + c * LANES:S5_HALF + (c + 1) * LANES] for c in range(n_half)]
    for n in range(ts):
        rows = slice(n * nb, (n + 1) * nb)
        for c in range(n_half):
            re_cols = slice(c * LANES, (c + 1) * LANES)
            im_cols = slice(S5_HALF + c * LANES, S5_HALF + (c + 1) * LANES)
            new_re = ar[c] * h_re[c] - ai[c] * h_im[c] + bu_ref[rows, re_cols]
            new_im = ar[c] * h_im[c] + ai[c] * h_re[c] + bu_ref[rows, im_cols]
            h_re[c], h_im[c] = new_re, new_im
            hs_ref[rows, re_cols] = new_re
            hs_ref[rows, im_cols] = new_im
    for c in range(n_half):
        h_ref[:, c * LANES:(c + 1) * LANES] = h_re[c]
        h_ref[:, S5_HALF + c * LANES:S5_HALF + (c + 1) * LANES] = h_im[c]
    y_tm = _dot(hs_ref[...].astype(BF16), cmat_ref[...])
    y = _dot_exact_lhs(permt_ref[...], y_tm) + d_ref[...] * u
    y = jax.nn.gelu(y)
    y = y * _sigmoid(_dot(y.astype(BF16), wglu_ref[...]))
    y_ref[...] = y.reshape(nb, ts, width)


def _step_major_permutation(nb, ts):
    p = np.zeros((nb * ts, nb * ts), np.float32)
    b, n = np.meshgrid(np.arange(nb), np.arange(ts), indexing="ij")
    p[(n * nb + b).ravel(), (b * ts + n).ravel()] = 1.0
    return jnp.asarray(p, BF16), jnp.asarray(p.T, BF16)


def s5_mixer(u, bsz, seq, prm, layer):
    ts = TS_S5
    u3 = u.reshape(bsz, seq, MIX_WIDTH)
    blk = pl.BlockSpec((bsz, ts, MIX_WIDTH), lambda i: (0, i, 0))
    perm, perm_t = _step_major_permutation(bsz, ts)
    names = ("bmat", "a_re", "a_im", "cmat", "d", "w_glu")
    y = pl.pallas_call(
        _s5_kernel,
        grid=(seq // ts,),
        in_specs=[blk, _const_spec(perm.shape), _const_spec(perm.shape)] +
                 [_layer_spec(prm[n], layer) for n in names],
        out_specs=blk,
        out_shape=jax.ShapeDtypeStruct((bsz, seq, MIX_WIDTH), F32),
        scratch_shapes=[pltpu.VMEM((bsz * ts, 2 * S5_HALF), F32), pltpu.VMEM((bsz * ts, 2 * S5_HALF), F32),
                        pltpu.VMEM((bsz, 2 * S5_HALF), F32)],
        compiler_params=_cparams("arbitrary"),
        name="s5_mixer",
    )(u3, perm, perm_t, *[prm[n] for n in names])
    return y.reshape(bsz * seq, MIX_WIDTH)


def s5_params(a_re, a_im, log_dt, b_re, b_im, c_re, c_im, d_skip, w_glu):
    dt = jnp.exp(log_dt)[:, None]
    mag = jnp.exp(a_re * dt)
    abar_re, abar_im = mag * jnp.cos(a_im * dt), mag * jnp.sin(a_im * dt)
    den = a_re * a_re + a_im * a_im
    num_re, num_im = abar_re - 1.0, abar_im
    coef_re = (num_re * a_re + num_im * a_im) / den
    coef_im = (num_im * a_re - num_re * a_im) / den
    bbar_re = coef_re[..., None] * b_re - coef_im[..., None] * b_im
    bbar_im = coef_re[..., None] * b_im + coef_im[..., None] * b_re
    eye = jnp.eye(S5_GROUPS, dtype=F32)
    b_bd_re = jnp.einsum("gpc,gh->gchp", bbar_re, eye).reshape(MIX_WIDTH, S5_HALF)
    b_bd_im = jnp.einsum("gpc,gh->gchp", bbar_im, eye).reshape(MIX_WIDTH, S5_HALF)
    c_bd_re = jnp.einsum("gcp,gh->gphc", c_re, eye).reshape(S5_HALF, MIX_WIDTH)
    c_bd_im = jnp.einsum("gcp,gh->gphc", c_im, eye).reshape(S5_HALF, MIX_WIDTH)
    return {
        "bmat": jnp.concatenate([b_bd_re, b_bd_im], axis=1).astype(BF16),
        "cmat": jnp.concatenate([c_bd_re, -c_bd_im], axis=0).astype(BF16),
        "a_re": abar_re.reshape(1, S5_HALF), "a_im": abar_im.reshape(1, S5_HALF),
        "d": d_skip.reshape(1, MIX_WIDTH), "w_glu": w_glu.astype(BF16),
    }


MLA_QK_PAD = 128


def _mla_prep_kernel(h_ref, cos_ref, sin_ref, gq_ref, gkv_ref, wq_ref, wqr_ref, wk_ref, wv_ref,
                     q_ref, k_ref, v_ref):
    h = h_ref[...]
    cos = cos_ref[...]
    sin = sin_ref[...]
    c_q = h[:, 0:256]
    c_kv = h[:, 256:384]
    k_pe = h[:, 384:512] * cos + h[:, 512:640] * sin
    qn = c_q * lax.rsqrt(jnp.sum(c_q * c_q, axis=-1, keepdims=True) / MLA_Q_LORA + EPS) * gq_ref[...]
    kvn = c_kv * lax.rsqrt(jnp.mean(c_kv * c_kv, axis=-1, keepdims=True) + EPS) * gkv_ref[...]
    qn = qn.astype(BF16)
    kvn = kvn.astype(BF16)
    scale = (MLA_NOPE + MLA_ROPE) ** -0.5 * math.log2(math.e)
    heads = lambda a: jnp.concatenate([a] * MLA_HEADS, axis=-1)
    q = (_dot(qn, wq_ref[...]) * heads(cos) + _dot(qn, wqr_ref[...]) * heads(sin)) * scale
    k = _dot(kvn, wk_ref[...]) + heads(k_pe)
    v = _dot(kvn, wv_ref[...])
    for hd in range(MLA_HEADS):
        q_ref[hd] = q[:, hd * MLA_QK_PAD:(hd + 1) * MLA_QK_PAD].astype(BF16)
        k_ref[hd] = k[:, hd * MLA_QK_PAD:(hd + 1) * MLA_QK_PAD].astype(BF16)
        v_ref[hd] = v[:, hd * MLA_V:(hd + 1) * MLA_V].astype(BF16)


def _flash_kernel(q_ref, k_ref, v_ref, o_ref, m_ref, l_ref, acc_ref):
    qi = pl.program_id(1)
    _, nb, tq, _ = q_ref.shape
    n_part = tq // LANES
    row = lax.broadcasted_iota(jnp.int32, (tq, tq), 0)
    col = lax.broadcasted_iota(jnp.int32, (tq, tq), 1)
    units = [(hd, r) for r in range(nb) for hd in range(MLA_HEADS)]

    def scores(hd, r, rows, masked):
        s = _dot_nt(q_ref[hd, r], k_ref[hd, r, rows, :])
        if masked:
            s = jnp.where(col <= row, s, NEG_INF)
        return [s[:, c * LANES:(c + 1) * LANES] for c in range(n_part)]

    def chunk(j, masked):
        rows = pl.ds(pl.multiple_of(j * tq, tq), tq)
        nxt = scores(*units[0], rows, masked)
        for u, (hd, r) in enumerate(units):
            parts = nxt
            if u + 1 < len(units):
                nxt = scores(*units[u + 1], rows, masked)
            m_old = m_ref[u]
            row_max = jnp.max(functools.reduce(jnp.maximum, parts), axis=-1, keepdims=True)
            m_new = jnp.maximum(m_old, row_max)
            p = [jnp.exp2(part - m_new) for part in parts]
            corr = jnp.exp2(m_old - m_new)
            l_ref[u] = corr * l_ref[u] + functools.reduce(jnp.add, p)
            m_ref[u] = m_new
            prob = jnp.concatenate(p, axis=-1).astype(BF16)
            acc_ref[u] = corr[:, :MLA_V] * acc_ref[u] + _dot(prob, v_ref[hd, r, rows, :])

    def body(j, carry):
        chunk(j, False)
        return carry

    m_ref[...] = jnp.full_like(m_ref, NEG_INF)
    l_ref[...] = jnp.zeros_like(l_ref)
    acc_ref[...] = jnp.zeros_like(acc_ref)
    lax.fori_loop(0, qi, body, 0)
    chunk(qi, True)
    for r in range(nb):
        o_ref[r] = jnp.concatenate(
            [acc_ref[r * MLA_HEADS + hd] / jnp.sum(l_ref[r * MLA_HEADS + hd], axis=-1, keepdims=True)
             for hd in range(MLA_HEADS)], axis=-1)


def mla_mixer(h_mla, cos_m, sin_m, bsz, seq, prm, layer):
    t = bsz * seq
    tm = TM_PROJ
    row = lambda w: pl.BlockSpec((tm, w), lambda i: (i, 0))
    hrow = lambda w: pl.BlockSpec((MLA_HEADS, tm, w), lambda i: (0, i, 0))
    names = ("gq", "gkv", "wq", "wqr", "wk", "wv")
    q, k, v = pl.pallas_call(
        _mla_prep_kernel,
        grid=(t // tm,),
        in_specs=[row(W_MLA_IN), row(LANES), row(LANES)] + [_layer_spec(prm[n], layer) for n in names],
        out_specs=[hrow(MLA_QK_PAD), hrow(MLA_QK_PAD), hrow(MLA_V)],
        out_shape=[jax.ShapeDtypeStruct((MLA_HEADS, t, MLA_QK_PAD), BF16),
                   jax.ShapeDtypeStruct((MLA_HEADS, t, MLA_QK_PAD), BF16),
                   jax.ShapeDtypeStruct((MLA_HEADS, t, MLA_V), BF16)],
        compiler_params=_cparams("parallel"),
        name="mla_prep",
    )(h_mla, cos_m, sin_m, *[prm[n] for n in names])

    tq = TQ_ATT
    nb = NB_ATT
    assert bsz % nb == 0 and seq % tq == 0
    n_unit = nb * MLA_HEADS
    by_row = lambda a: a.reshape(MLA_HEADS, bsz, seq, a.shape[-1])
    q_spec = pl.BlockSpec((MLA_HEADS, nb, tq, MLA_QK_PAD), lambda b, i: (0, b, i, 0))
    k_spec = pl.BlockSpec((MLA_HEADS, nb, seq, MLA_QK_PAD), lambda b, i: (0, b, 0, 0))
    v_spec = pl.BlockSpec((MLA_HEADS, nb, seq, MLA_V), lambda b, i: (0, b, 0, 0))
    y = pl.pallas_call(
        _flash_kernel,
        grid=(bsz // nb, seq // tq),
        in_specs=[q_spec, k_spec, v_spec],
        out_specs=pl.BlockSpec((nb, tq, MLA_HEADS * MLA_V), lambda b, i: (b, i, 0)),
        out_shape=jax.ShapeDtypeStruct((bsz, seq, MLA_HEADS * MLA_V), F32),
        scratch_shapes=[pltpu.VMEM((n_unit, tq, LANES), F32), pltpu.VMEM((n_unit, tq, LANES), F32),
                        pltpu.VMEM((n_unit, tq, MLA_V), F32)],
        compiler_params=_cparams("parallel", "arbitrary"),
        name="mla_flash",
    )(by_row(q), by_row(k), by_row(v))
    return y.reshape(t, MLA_HEADS * MLA_V)


def _rot_half_cols(w, width):
    shp = w.shape
    w = w.reshape(shp[0], -1, 2, width // 2)
    return jnp.stack([-w[:, :, 1], w[:, :, 0]], axis=2).reshape(shp)


def mla_params(q_norm, kv_norm, w_uq, w_ukv):
    dq = MLA_NOPE + MLA_ROPE
    row_pad = 256 - MLA_Q_LORA
    w_uq_h = w_uq.reshape(MLA_Q_LORA, MLA_HEADS, dq)
    wq = jnp.pad(w_uq_h, ((0, row_pad), (0, 0), (0, MLA_QK_PAD - dq)))
    pe = w_uq_h[:, :, MLA_NOPE:]
    pe_rot = jnp.concatenate([-pe[..., MLA_ROPE // 2:], pe[..., :MLA_ROPE // 2]], axis=-1)
    wqr = jnp.pad(pe_rot, ((0, row_pad), (0, 0), (MLA_NOPE, MLA_QK_PAD - dq)))
    w_kv_h = w_ukv.reshape(MLA_KV_LORA, MLA_HEADS, MLA_NOPE + MLA_V)
    wk = jnp.pad(w_kv_h[:, :, :MLA_NOPE], ((0, 0), (0, 0), (0, MLA_QK_PAD - MLA_NOPE)))
    wv = w_kv_h[:, :, MLA_NOPE:]
    gq = jnp.pad(q_norm, (0, row_pad)).reshape(1, 256)
    flat = lambda w: w.reshape(w.shape[0], -1).astype(BF16)
    return {"wq": flat(wq), "wqr": flat(wqr), "wk": flat(wk), "wv": flat(wv),
            "gq": gq, "gkv": kv_norm.reshape(1, MLA_KV_LORA)}


def _head_mask(shape, hd, axis):
    return (lax.broadcasted_iota(jnp.int32, shape, axis) // HEAD_DIM) == hd


def _hgrn_kernel(h_ref, lb_ref, ng_ref, aall_ref, lvl_ref, hm_ref, blk_ref, o_ref, st_ref, e_ref):
    @pl.when(pl.program_id(1) == 0)
    def _():
        st_ref[...] = jnp.zeros_like(st_ref)

    for row in range(h_ref.shape[0]):
        _hgrn_tile(h_ref.at[row], lb_ref, ng_ref, aall_ref, lvl_ref, hm_ref, blk_ref, o_ref.at[row],
                   st_ref.at[row], e_ref.at[row])


def _hgrn_tile(h_ref, lb_ref, ng_ref, aall_ref, lvl_ref, hm_ref, blk_ref, o_ref, st_ref, e_ref):
    tt = h_ref.shape[0]
    width = HEADS * HEAD_DIM
    n_lev = aall_ref.shape[0] // tt - 1

    q = h_ref[:, 0:width]
    f = h_ref[:, width:2 * width]
    v = h_ref[:, 2 * width:3 * width]
    g = h_ref[:, 3 * width:4 * width]
    lb = lb_ref[...]
    log_sig = jnp.minimum(f, 0.0) - jnp.log1p(jnp.exp(-jnp.abs(f)))
    a = jnp.log(lb)
    b = jnp.log1p(-lb) + log_sig
    log_f = jnp.maximum(a, b) + jnp.log1p(jnp.exp(-jnp.abs(a - b)))
    k = (1.0 - lb) * _sigmoid(-f)
    qf = q * _sigmoid(q)
    e2 = _dot(aall_ref[...], jnp.concatenate(_split2(log_f), axis=-1))
    e_ref[...] = e2[:, 0:width] + e2[:, width:2 * width]

    t_row = lax.broadcasted_iota(jnp.int32, (tt, width), 0)
    lvl = lvl_ref[...]
    attn = [jnp.zeros((tt, tt), F32) for _ in range(HEADS)]
    for lev in range(n_lev):
        x = jnp.exp(e_ref[lev * tt:(lev + 1) * tt, :])
        right = ((t_row >> lev) & 1) == 1
        qt = jnp.where(right, qf * x, 0.0).astype(BF16)
        kt = jnp.where(right, 0.0, k * x).astype(BF16)
        q_heads = jnp.concatenate([qt * hm_ref[hd] for hd in range(HEADS)], axis=0)
        s = _dot_nt(q_heads, kt)
        here = lvl == lev
        for hd in range(HEADS):
            attn[hd] = jnp.where(here, s[hd * tt:(hd + 1) * tt, :], attn[hd])
    vb = v.astype(BF16)
    o = jnp.zeros((tt, width), F32)
    for hd in range(HEADS):
        o = o + _dot(attn[hd].astype(BF16), vb * hm_ref[hd])
    ones_blk = blk_ref[...]
    o = o + _dot_exact_rhs(qf * k, ones_blk) * v
    bc = e_ref[n_lev * tt:(n_lev + 1) * tt, :]
    st = st_ref[...]
    st_m = jnp.where(ones_blk > 0, st, 0.0).astype(BF16)
    o = o + _dot_nt((qf * jnp.exp(bc)).astype(BF16), st_m)
    b_last = bc[tt - 1:tt, :]
    k_out = (k * jnp.exp(b_last - bc)).astype(BF16)
    st_ref[...] = st * jnp.exp(b_last) + _dot_tn(vb, k_out)
    ms = _dot_exact_rhs(o * o, ones_blk) * (1.0 / HEAD_DIM)
    o = o * lax.rsqrt(ms + EPS) * ng_ref[...]
    o_ref[...] = o * (g * _sigmoid(g))


def _hgrn_level_matrices(tt):
    n_lev = int(math.log2(tt))
    t = np.arange(tt)[:, None]
    j = np.arange(tt)[None, :]
    mats = []
    for lev in range(n_lev):
        half = 1 << lev
        mid = (t >> (lev + 1) << (lev + 1)) + half
        right = ((t >> lev) & 1) == 1
        mats.append(np.where(right, (j >= mid) & (j <= t), (j > t) & (j < mid)).astype(np.float32))
    mats.append((j <= t).astype(np.float32))
    level = np.full((tt, tt), -1, np.int32)
    lower = np.broadcast_to(j < t, (tt, tt))
    level[lower] = np.floor(np.log2((t ^ j)[lower])).astype(np.int32)
    return jnp.asarray(np.concatenate(mats, axis=0), BF16), jnp.asarray(level)


def _head_block_ones():
    i = np.arange(HEADS * HEAD_DIM)
    return jnp.asarray((i[:, None] // HEAD_DIM == i[None, :] // HEAD_DIM).astype(np.float32), BF16)


def _head_lane_masks(rows):
    i = np.arange(HEADS * HEAD_DIM)
    m = (i[None, :] // HEAD_DIM == np.arange(HEADS)[:, None]).astype(np.float32)
    return jnp.asarray(np.broadcast_to(m[:, None, :], (HEADS, rows, HEADS * HEAD_DIM)), BF16)


def hgrn2_mixer(h_hg, lb, norm_g, bsz, seq, layer):
    t = bsz * seq
    tt = TT_HG
    width = HEADS * HEAD_DIM
    aall, level = _hgrn_level_matrices(tt)
    nt = seq // tt
    nb = NB_HG
    assert bsz % nb == 0 and seq % tt == 0
    y = pl.pallas_call(
        _hgrn_kernel,
        grid=(bsz // nb, nt),
        in_specs=[pl.BlockSpec((nb, tt, 4 * width), lambda b, i: (b, i, 0)),
                  _layer_spec(lb, layer), _layer_spec(norm_g, layer),
                  _const_spec(aall.shape), _const_spec(level.shape), _const_spec((HEADS, tt, width)),
                  _const_spec((width, width))],
        out_specs=pl.BlockSpec((nb, tt, width), lambda b, i: (b, i, 0)),
        out_shape=jax.ShapeDtypeStruct((bsz, seq, width), F32),
        scratch_shapes=[pltpu.VMEM((nb, width, width), F32), pltpu.VMEM((nb, aall.shape[0], width), F32)],
        compiler_params=_cparams("parallel", "arbitrary"),
        name="hgrn2_mixer",
    )(h_hg.reshape(bsz, seq, 4 * width), lb, norm_g, aall, level, _head_lane_masks(tt), _head_block_ones())
    return y.reshape(t, width)


def _ret_kernel(h_ref, cos_ref, sin_ref, dmat_ref, qdec_ref, kdec_ref, cdec_ref, blk_ref, g_ref, b_ref,
                o_ref, st_ref):
    @pl.when(pl.program_id(1) == 0)
    def _():
        st_ref[...] = jnp.zeros_like(st_ref)

    for row in range(h_ref.shape[0]):
        _ret_tile(h_ref.at[row], cos_ref.at[row], sin_ref.at[row], dmat_ref, qdec_ref, kdec_ref, cdec_ref,
                  blk_ref, g_ref, b_ref, o_ref.at[row], st_ref.at[row])


def _ret_tile(h_ref, cos_ref, sin_ref, dmat_ref, qdec_ref, kdec_ref, cdec_ref, blk_ref, g_ref, b_ref,
              o_ref, st_ref):
    tt = h_ref.shape[0]
    width = HEADS * HEAD_DIM

    cos = cos_ref[...]
    sin = sin_ref[...]
    q = (h_ref[:, 0:width] * cos + h_ref[:, 4 * width:5 * width] * sin) * (HEAD_DIM ** -0.5)
    k = h_ref[:, width:2 * width] * cos + h_ref[:, 5 * width:6 * width] * sin
    v = h_ref[:, 2 * width:3 * width]
    g = h_ref[:, 3 * width:4 * width]
    kb = k.astype(BF16)
    vb = v.astype(BF16)
    ones_blk = blk_ref[...]
    o = jnp.zeros((tt, width), F32)
    for hd in range(HEADS):
        hm = _head_mask((tt, width), hd, 1)
        s = _dot_nt(jnp.where(hm, q, 0.0).astype(BF16), kb) * dmat_ref[hd]
        o = o + _dot(s.astype(BF16), jnp.where(hm, vb, jnp.zeros_like(vb)))
    st = st_ref[...]
    st_m = jnp.where(ones_blk > 0, st, 0.0).astype(BF16)
    o = o + _dot((q * qdec_ref[...]).astype(BF16), st_m)
    st_ref[...] = st * cdec_ref[...] + _dot_tn((k * kdec_ref[...]).astype(BF16), vb)
    mu = _dot_exact_rhs(o, ones_blk) * (1.0 / HEAD_DIM)
    oc = o - mu
    var = _dot_exact_rhs(oc * oc, ones_blk) * (1.0 / HEAD_DIM)
    o = oc * lax.rsqrt(var + EPS) * g_ref[...] + b_ref[...]
    o_ref[...] = (g * _sigmoid(g)) * o


def _retention_tables(tt):
    log_gamma = jnp.log(1.0 - 2.0 ** (-5.0 - jnp.arange(HEADS, dtype=F32)))
    t_idx = jnp.arange(tt, dtype=F32)
    rel = t_idx[:, None] - t_idx[None, :]
    dmat = jnp.where(rel >= 0, jnp.exp(jnp.maximum(rel, 0.0)[None] * log_gamma[:, None, None]), 0.0)
    lanes = lambda a: jnp.repeat(a, HEAD_DIM, axis=-1)
    qdec = lanes(jnp.exp((t_idx + 1.0)[:, None] * log_gamma[None, :]))
    kdec = lanes(jnp.exp((tt - 1 - t_idx)[:, None] * log_gamma[None, :]))
    cdec = lanes(jnp.exp(tt * log_gamma)[None, :])
    return dmat, qdec, kdec, cdec


def retention_mixer(h_ret, cos_r, sin_r, gn_g, gn_b, bsz, seq, layer):
    t = bsz * seq
    tt = TT_RET
    width = HEADS * HEAD_DIM
    nt = seq // tt
    dmat, qdec, kdec, cdec = _retention_tables(tt)
    nb = NB_RET
    assert bsz % nb == 0 and seq % tt == 0
    row = lambda w: pl.BlockSpec((nb, tt, w), lambda b, i: (b, i, 0))
    y = pl.pallas_call(
        _ret_kernel,
        grid=(bsz // nb, nt),
        in_specs=[row(W_RET_IN), row(width), row(width), _const_spec((HEADS, tt, tt)), _const_spec((tt, width)),
                  _const_spec((tt, width)), _const_spec((1, width)), _const_spec((width, width)),
                  _layer_spec(gn_g, layer), _layer_spec(gn_b, layer)],
        out_specs=row(width),
        out_shape=jax.ShapeDtypeStruct((bsz, seq, width), F32),
        scratch_shapes=[pltpu.VMEM((nb, width, width), F32)],
        compiler_params=_cparams("parallel", "arbitrary"),
        name="retention_mixer",
    )(h_ret.reshape(bsz, seq, W_RET_IN), cos_r.reshape(bsz, seq, width), sin_r.reshape(bsz, seq, width),
      dmat, qdec, kdec, cdec, _head_block_ones(), gn_g, gn_b)
    return y.reshape(t, width)


def _store_token_tiles(ref, val):
    tm = val.shape[0]
    for j in range(D_MODEL // LANES):
        ref[pl.ds(j, tm, stride=SUBLANES), :] = val[:, j * LANES:(j + 1) * LANES]


def _load_token_tiles(ref, tm):
    return jnp.concatenate([ref[pl.ds(j, tm, stride=SUBLANES), :] for j in range(D_MODEL // LANES)], axis=-1)


def _mix_kernel(x_ref, ya_ref, yb_ref, yc_ref, yd_ref, wg_ref, wb_ref, wo_ref, g_ref, b_ref, o_ref):
    x = x_ref[...]
    xb = x.astype(BF16)
    acc = jnp.zeros(x.shape, F32)
    for n, y_ref in enumerate((ya_ref, yb_ref, yc_ref, yd_ref)):
        gate = _sigmoid(_dot(xb, wg_ref[:, n * D_MODEL:(n + 1) * D_MODEL]))
        acc = acc + gate * _dot(y_ref[...].astype(BF16), wb_ref[n])
    z = ALPHA * x + _dot(acc.astype(BF16), wo_ref[...])
    o_ref[...] = _layer_norm_rows(z, g_ref[...], b_ref[...])


def mix_branches(x, ys, params, layer):
    t = x.shape[0]
    tm = TM_PROJ
    row = lambda w: pl.BlockSpec((tm, w), lambda i: (i, 0))
    return pl.pallas_call(
        _mix_kernel,
        grid=(t // tm,),
        in_specs=[row(D_MODEL)] + [row(MIX_WIDTH)] * 4 + [_layer_spec(a, layer) for a in params],
        out_specs=row(D_MODEL),
        out_shape=jax.ShapeDtypeStruct((t, D_MODEL), F32),
        compiler_params=_cparams("parallel"),
        name="mix_branches",
    )(x, *ys, *params)


def _ffn_kernel(x_ref, wg_ref, wu_ref, wd_ref, o_ref, xb_ref, acc_ref):
    c = pl.program_id(1)

    @pl.when(c == 0)
    def _():
        xb_ref[...] = x_ref[...].astype(BF16)
        acc_ref[...] = jnp.zeros_like(acc_ref)

    xb = xb_ref[...]
    gate = _dot(xb, wg_ref[...])
    act = gate * _sigmoid(gate) * _dot(xb, wu_ref[...])
    acc_ref[...] += _dot(act.astype(BF16), wd_ref[...])

    @pl.when(c == pl.num_programs(1) - 1)
    def _():
        o_ref[...] = acc_ref[...]


def dense_ffn(x, w_gate, w_up, w_down, layer):
    t = x.shape[0]
    tm, tf = TM_FFN, TF_FFN
    return pl.pallas_call(
        _ffn_kernel,
        grid=(t // tm, D_FF // tf),
        in_specs=[pl.BlockSpec((tm, D_MODEL), lambda i, c: (i, 0)),
                  pl.BlockSpec((None, D_MODEL, tf), lambda i, c: (layer, 0, c)),
                  pl.BlockSpec((None, D_MODEL, tf), lambda i, c: (layer, 0, c)),
                  pl.BlockSpec((None, tf, D_MODEL), lambda i, c: (layer, c, 0))],
        out_specs=pl.BlockSpec((tm, D_MODEL), lambda i, c: (i, 0)),
        out_shape=jax.ShapeDtypeStruct((t, D_MODEL), F32),
        scratch_shapes=[pltpu.VMEM((tm, D_MODEL), BF16), pltpu.VMEM((tm, D_MODEL), F32)],
        compiler_params=_cparams("parallel", "arbitrary"),
        name="dense_ffn",
    )(x, w_gate, w_up, w_down)


def _router_kernel(x_ref, wh_ref, wl_ref, upper_ref, meta_ref, cnt_ref, slots_ref, carry_ref):
    tm = x_ref.shape[0]
    slots_ref[...] = jnp.zeros_like(slots_ref)

    @pl.when(pl.program_id(0) == 0)
    def _():
        carry_ref[...] = jnp.zeros_like(carry_ref)

    xh, xl = _split2(x_ref[...])
    wh = wh_ref[...]
    logits = _dot_nt(wh, xh) + _dot_nt(wh, xl) + _dot_nt(wl_ref[...], xh)
    e_idx = lax.broadcasted_iota(jnp.int32, logits.shape, 0)
    m1 = jnp.max(logits, axis=0, keepdims=True)
    i1 = jnp.min(jnp.where(logits == m1, e_idx, N_EXPERTS), axis=0, keepdims=True)
    rest = jnp.where(e_idx == i1, -jnp.inf, logits)
    m2 = jnp.max(rest, axis=0, keepdims=True)
    i2 = jnp.min(jnp.where(rest == m2, e_idx, N_EXPERTS), axis=0, keepdims=True)
    ex = jnp.exp(m2 - m1)
    w1 = 1.0 / (1.0 + ex)
    w2 = ex / (1.0 + ex)
    sel1 = e_idx == i1
    sel2 = e_idx == i2
    onehot = jnp.where(sel1 | sel2, 1.0, 0.0)
    carry = carry_ref[...]
    ranks = _dot(onehot.astype(BF16), upper_ref[...]) + carry[:, 0:1]
    r1 = jnp.sum(jnp.where(sel1, ranks, 0.0), axis=0, keepdims=True)
    r2 = jnp.sum(jnp.where(sel2, ranks, 0.0), axis=0, keepdims=True)
    row = lax.broadcasted_iota(jnp.int32, (SUBLANES, tm), 0)
    vals = (i1.astype(F32), i2.astype(F32), w1, w2, r1, r2)
    meta = jnp.zeros((SUBLANES, tm), F32)
    for n, val in enumerate(vals):
        meta = jnp.where(row == n, val, meta)
    meta_ref[...] = meta
    carry = carry + jnp.sum(onehot, axis=1, keepdims=True)
    carry_ref[...] = carry
    cnt_ref[...] = carry


def moe_router(x, w_router, p_len):
    t = x.shape[0]
    tm = TM_ROUTE
    slot_rows = p_len * SUBLANES // (t // tm)
    assert slot_rows % SUBLANES == 0 and slot_rows * (t // tm) == p_len * SUBLANES
    wt = w_router.T
    wh = wt.astype(BF16)
    wl = (wt - wh.astype(F32)).astype(BF16)
    upper = jnp.asarray(np.triu(np.ones((tm, tm), np.float32), 1), BF16)
    return pl.pallas_call(
        _router_kernel,
        grid=(t // tm,),
        in_specs=[pl.BlockSpec((tm, D_MODEL), lambda i: (i, 0)), _const_spec((N_EXPERTS, D_MODEL)),
                  _const_spec((N_EXPERTS, D_MODEL)), _const_spec((tm, tm))],
        out_specs=[pl.BlockSpec((SUBLANES, tm), lambda i: (0, i)), pl.BlockSpec((N_EXPERTS, LANES), lambda i: (0, 0)),
                   pl.BlockSpec((slot_rows, LANES), lambda i: (i, 0))],
        out_shape=[jax.ShapeDtypeStruct((SUBLANES, t), F32), jax.ShapeDtypeStruct((N_EXPERTS, LANES), F32),
                   jax.ShapeDtypeStruct((p_len * SUBLANES, LANES), F32)],
        scratch_shapes=[pltpu.VMEM((N_EXPERTS, LANES), F32)],
        compiler_params=_cparams("arbitrary"),
        name="moe_router",
    )(x, wh, wl, upper)


def _token_rows(ref, tok):
    return ref.at[pl.ds(pl.multiple_of(tok * SUBLANES, SUBLANES), SUBLANES), :]


def _start_then_wait(n, copies):
    def start(r, carry):
        for idx, cp in enumerate(copies(r)):
            cp.start(priority=idx % 2)
        return carry

    def wait(r, carry):
        for cp in copies(r):
            cp.wait()
        return carry

    lax.fori_loop(0, n, start, 0, unroll=DMA_UNROLL)
    return lambda: lax.fori_loop(0, n, wait, 0, unroll=DMA_UNROLL)


def _dispatch_kernel(dest_ref, x_ref, init_ref, xs_ref, buf_ref, sem):
    del init_ref
    td = x_ref.shape[0]
    _store_token_tiles(buf_ref, x_ref[...])

    def copies(r):
        src = _token_rows(buf_ref, r)
        return (pltpu.make_async_copy(src, _token_rows(xs_ref, dest_ref[0, 0, 2 * r]), sem),
                pltpu.make_async_copy(src, _token_rows(xs_ref, dest_ref[0, 0, 2 * r + 1]), sem))

    _start_then_wait(td, copies)()


def moe_dispatch(x, dest3, init):
    t = x.shape[0]
    td = dest3.shape[2] // 2
    return pl.pallas_call(
        _dispatch_kernel,
        grid=(t // td,),
        in_specs=[pl.BlockSpec((1, 1, 2 * td), lambda i: (i, 0, 0), memory_space=pltpu.SMEM),
                  pl.BlockSpec((td, D_MODEL), lambda i: (i, 0)), pl.BlockSpec(memory_space=pl.ANY)],
        out_specs=pl.BlockSpec(memory_space=pl.ANY),
        out_shape=jax.ShapeDtypeStruct(init.shape, F32),
        scratch_shapes=[pltpu.VMEM((td * SUBLANES, LANES), F32), pltpu.SemaphoreType.DMA(())],
        input_output_aliases={2: 0},
        compiler_params=pltpu.CompilerParams(dimension_semantics=("arbitrary",), has_side_effects=True),
        name="moe_dispatch",
    )(dest3, x, init)


def _expert_kernel(te_ref, nv_ref, tr_ref, xs_ref, wg_ref, wu_ref, wd_ref, ys_ref, xb_ref, acc_ref):
    i = pl.program_id(0)
    c = pl.program_id(1)
    tm = xb_ref.shape[0]
    valid = i < nv_ref[0]
    upper_used = tr_ref[i] > tm // 2

    @pl.when(valid & (c == 0))
    def _():
        xb_ref[...] = _load_token_tiles(xs_ref, tm).astype(BF16)
        acc_ref[...] = jnp.zeros_like(acc_ref)

    def swiglu_rows(m):
        xb = xb_ref[:m, :]
        gate = _dot(xb, wg_ref[...].astype(BF16))
        act = gate * _sigmoid(gate) * _dot(xb, wu_ref[...].astype(BF16))
        acc_ref[:m, :] += _dot(act.astype(BF16), wd_ref[...].astype(BF16))

    @pl.when(valid & upper_used)
    def _():
        swiglu_rows(tm)

    @pl.when(valid & jnp.logical_not(upper_used))
    def _():
        swiglu_rows(tm // 2)

    @pl.when(c == pl.num_programs(1) - 1)
    def _():
        _store_token_tiles(ys_ref, jnp.where(valid, acc_ref[...], 0.0))


def moe_experts(xs, tile_expert, n_valid, tile_rows, layer, w_gate, w_up, w_down):
    tm, tf = TM_MOE, TF_FFN
    n_tiles = xs.shape[0] // (tm * SUBLANES)
    nc = D_FF // tf

    def chunk(i, c, nv):
        return jnp.where(i < nv[0], c, nc - 1)

    grid_spec = pltpu.PrefetchScalarGridSpec(
        num_scalar_prefetch=3,
        grid=(n_tiles, nc),
        in_specs=[pl.BlockSpec((tm * SUBLANES, LANES), lambda i, c, te, nv, tr: (jnp.minimum(i, nv[0] - 1), 0)),
                  pl.BlockSpec((None, None, D_MODEL, tf), lambda i, c, te, nv, tr: (layer, te[i], 0, chunk(i, c, nv))),
                  pl.BlockSpec((None, None, D_MODEL, tf), lambda i, c, te, nv, tr: (layer, te[i], 0, chunk(i, c, nv))),
                  pl.BlockSpec((None, None, tf, D_MODEL), lambda i, c, te, nv, tr: (layer, te[i], chunk(i, c, nv), 0))],
        out_specs=pl.BlockSpec((tm * SUBLANES, LANES), lambda i, c, te, nv, tr: (i, 0)),
        scratch_shapes=[pltpu.VMEM((tm, D_MODEL), BF16), pltpu.VMEM((tm, D_MODEL), F32)],
    )
    return pl.pallas_call(
        _expert_kernel,
        grid_spec=grid_spec,
        out_shape=jax.ShapeDtypeStruct(xs.shape, F32),
        compiler_params=_cparams("arbitrary", "arbitrary"),
        name="moe_experts",
    )(tile_expert, n_valid, tile_rows, xs, w_gate, w_up, w_down)


def moe_ffn(x, w_router, layer, w_gate, w_up, w_down):
    t = x.shape[0]
    tm = TM_MOE
    n_tiles = (2 * t) // tm + N_EXPERTS
    meta, cnt, slots = moe_router(x, w_router, n_tiles * tm)
    counts = cnt[:, 0].astype(jnp.int32)
    padded = (counts + tm - 1) // tm * tm
    pad_end = jnp.cumsum(padded)
    pad_start = pad_end - padded
    experts = meta[0:2].astype(jnp.int32)
    start_of = jnp.sum(jnp.where(experts[..., None] == jnp.arange(N_EXPERTS), pad_start, 0), axis=-1)
    dest = (start_of + meta[4:6].astype(jnp.int32)).T
    dest3 = dest.reshape(t // TD_DISPATCH, 1, 2 * TD_DISPATCH)
    n_valid = (pad_end[-1] // tm).astype(jnp.int32).reshape(1)
    tile_start = jnp.arange(n_tiles, dtype=jnp.int32) * tm
    tile_expert = jnp.minimum(jnp.sum(pad_end[None, :] <= tile_start[:, None], axis=1), N_EXPERTS - 1)
    last_expert = jnp.sum(jnp.where(jnp.arange(n_tiles) == n_valid[0] - 1, tile_expert, 0))
    tile_expert = jnp.where(jnp.arange(n_tiles) < n_valid[0], tile_expert, last_expert).astype(jnp.int32)
    of_tile = lambda per_expert: jnp.sum(jnp.where(tile_expert[:, None] == jnp.arange(N_EXPERTS), per_expert, 0), -1)
    tile_rows = jnp.clip(of_tile(pad_start + counts) - tile_start, 0, tm).astype(jnp.int32)
    xs = moe_dispatch(x, dest3, slots)
    ys = moe_experts(xs, tile_expert, n_valid, tile_rows, layer, w_gate, w_up, w_down)
    return ys, dest3, meta[2:4].T


def _post_kernel(x_ref, p_ref, wpg_ref, wpp_ref, g_ref, b_ref, f_ref, o_ref):
    x = x_ref[...]
    ple = _sigmoid(_dot(x.astype(BF16), wpg_ref[...])) * _dot(p_ref[...].astype(BF16), wpp_ref[...])
    o_ref[...] = _layer_norm_rows(ALPHA * x + f_ref[...] + ple, g_ref[...], b_ref[...])


def _post_moe_kernel(dest_ref, x_ref, p_ref, wpg_ref, wpp_ref, g_ref, b_ref, w_ref, ys_ref, o_ref,
                     buf0_ref, buf1_ref, sem):
    x = x_ref[...]
    tm = x.shape[0]

    def copies(r):
        return (pltpu.make_async_copy(_token_rows(ys_ref, dest_ref[0, 0, 2 * r]), _token_rows(buf0_ref, r), sem),
                pltpu.make_async_copy(_token_rows(ys_ref, dest_ref[0, 0, 2 * r + 1]), _token_rows(buf1_ref, r), sem))

    wait_all = _start_then_wait(tm, copies)
    ple = _sigmoid(_dot(x.astype(BF16), wpg_ref[...])) * _dot(p_ref[...].astype(BF16), wpp_ref[...])
    wait_all()
    w = w_ref[...]
    f = w[:, 0:1] * _load_token_tiles(buf0_ref, tm) + w[:, 1:2] * _load_token_tiles(buf1_ref, tm)
    o_ref[...] = _layer_norm_rows(ALPHA * x + f + ple, g_ref[...], b_ref[...])


def post_layer(x, p, layer, params, f=None, moe=None):
    t = x.shape[0]
    tm = TM_PROJ if moe is None else TD_DISPATCH
    row = lambda w: pl.BlockSpec((tm, w), lambda i: (i, 0))
    in_specs = [row(D_MODEL), pl.BlockSpec((None, tm, PLE_DIM), lambda i: (layer, i, 0))]
    in_specs += [_layer_spec(a, layer) for a in params]
    args = [x, p, *params]
    scratch = []
    if moe is None:
        body = _post_kernel
        in_specs += [row(D_MODEL)]
        args += [f]
    else:
        body = _post_moe_kernel
        ys, dest3, w2 = moe
        in_specs = [pl.BlockSpec((1, 1, 2 * tm), lambda i: (i, 0, 0), memory_space=pltpu.SMEM)] + in_specs
        in_specs += [row(2), pl.BlockSpec(memory_space=pl.ANY)]
        args = [dest3] + args + [w2, ys]
        scratch = [pltpu.VMEM((tm * SUBLANES, LANES), F32), pltpu.VMEM((tm * SUBLANES, LANES), F32),
                   pltpu.SemaphoreType.DMA(())]
    return pl.pallas_call(
        body,
        grid=(t // tm,),
        in_specs=in_specs,
        out_specs=row(D_MODEL),
        out_shape=jax.ShapeDtypeStruct((t, D_MODEL), F32),
        scratch_shapes=scratch,
        compiler_params=_cparams("parallel" if moe is None else "arbitrary"),
        name="post_layer",
    )(*args)


def split_in_weights(w):
    sizes = (MIX_WIDTH, MLA_Q_LORA, MLA_KV_LORA, MLA_ROPE) + (HEADS * HEAD_DIM,) * 8 + (N_BRANCH * D_MODEL,)
    parts, start = [], 0
    for n in sizes:
        parts.append(w[:, start:start + n])
        start += n
    u_s5, c_q, c_kv, k_rope, hq, hf, hi, hg, rq, rk, rv, rg, gate = parts
    k_rot = jnp.concatenate([-k_rope[:, MLA_ROPE // 2:], k_rope[:, :MLA_ROPE // 2]], axis=1)
    zeros = lambda n: jnp.zeros((D_MODEL, n), w.dtype)
    rope_slot = lambda cols: [zeros(MLA_NOPE), cols, zeros(MLA_QK_PAD - MLA_NOPE - MLA_ROPE)]
    w_mla = jnp.concatenate([c_q, zeros(256 - MLA_Q_LORA), c_kv] + rope_slot(k_rope) + rope_slot(k_rot), axis=1)
    w_hg = jnp.concatenate([hq, hf, hi, hg], axis=1)
    w_ret = jnp.concatenate([rq, rk, rv, rg, _rot_half_cols(rq, HEAD_DIM), _rot_half_cols(rk, HEAD_DIM)], axis=1)
    return [a.astype(BF16) for a in (w_mla, u_s5, w_hg, w_ret)], gate.astype(BF16)


def kernel(x, p, positions, w_in, s5_a_re, s5_a_im, s5_log_dt, s5_b_re, s5_b_im, s5_c_re, s5_c_im, s5_d, s5_w_glu, mla_q_norm, mla_kv_norm, mla_w_uq, mla_w_ukv, hg_lb_raw, hg_norm, ret_gn_g, ret_gn_b, w_branch, w_o, ln1_g, ln1_b, ff_w_gate, ff_w_up, ff_w_down, moe_router, moe_w_gate, moe_w_up, moe_w_down, ple_w_gate, ple_w_proj, ln2_g, ln2_b):
    bsz, seq, _ = x.shape
    t = bsz * seq
    depth = w_in.shape[0]
    x = x.reshape(t, D_MODEL)
    p = p.reshape(depth, t, PLE_DIM)
    cos_m, sin_m, cos_r, sin_r = rope_tables(positions)
    lb_all = jnp.cumsum(jax.nn.softmax(hg_lb_raw.astype(F32), axis=0), axis=0)
    lb_all = lb_all - lb_all[0]
    rows = lambda a: a.astype(F32).reshape(depth, 1, -1)
    in_ws, w_gate = jax.vmap(split_in_weights)(w_in)
    s5_prm = jax.vmap(s5_params)(s5_a_re, s5_a_im, s5_log_dt, s5_b_re, s5_b_im, s5_c_re, s5_c_im, s5_d, s5_w_glu)
    mla_prm = jax.vmap(mla_params)(mla_q_norm, mla_kv_norm, mla_w_uq, mla_w_ukv)
    mix_prm = (w_gate, w_branch.astype(BF16), w_o.astype(BF16), rows(ln1_g), rows(ln1_b))
    post_prm = (ple_w_gate.astype(BF16), ple_w_proj.astype(BF16), rows(ln2_g), rows(ln2_b))
    ffn_w = (ff_w_gate.astype(BF16), ff_w_up.astype(BF16), ff_w_down.astype(BF16))
    lb_rows, hg_norm_rows, gn_g_rows, gn_b_rows = rows(lb_all), rows(hg_norm), rows(ret_gn_g), rows(ret_gn_b)
    for i in range(depth):
        h_mla, h_s5, h_hg, h_ret = in_projection(x, in_ws, i)
        y_a = s5_mixer(h_s5, bsz, seq, s5_prm, i)
        y_b = mla_mixer(h_mla, cos_m, sin_m, bsz, seq, mla_prm, i)
        y_c = hgrn2_mixer(h_hg, lb_rows, hg_norm_rows, bsz, seq, i)
        y_d = retention_mixer(h_ret, cos_r, sin_r, gn_g_rows, gn_b_rows, bsz, seq, i)
        x = mix_branches(x, (y_a, y_b, y_c, y_d), mix_prm, i)
        if i % 2 == 1:
            x = post_layer(x, p, i, post_prm, moe=moe_ffn(x, moe_router[i // 2], i // 2, moe_w_gate, moe_w_up,
                                                           moe_w_down))
        else:
            x = post_layer(x, p, i, post_prm, f=dense_ffn(x, *ffn_w, i // 2))
    return x.reshape(bsz, seq, D_MODEL)
```

```python
import functools
import math

import numpy as np
import jax
import jax.numpy as jnp
from jax import lax
from jax.experimental import pallas as pl
from jax.experimental.pallas import tpu as pltpu

F32 = jnp.float32
BF16 = jnp.bfloat16

D_MODEL = 1024
DEPTH = 4
PLE_DIM = 256
MIX_WIDTH = 256
N_BRANCH = 4
S5_GROUP = 16
S5_GROUPS = 16
S5_STATE = 64
MLA_HEADS = 4
MLA_NOPE = 64
MLA_ROPE = 32
MLA_V = 64
MLA_Q_LORA = 192
MLA_KV_LORA = 128
HEADS = 4
HEAD_DIM = 64
D_FF = 3584
N_EXPERTS = 8
ROPE_BASE = 10000.0
EPS = 1e-5
NEG_INF = -1e30
ALPHA = (2 * DEPTH) ** 0.25

LANES = 128
SUBLANES = 8
VMEM_LIMIT = 56 * 1024 * 1024

TM_PROJ = 512
TS_S5 = 64
TQ_ATT = 256
NB_ATT = 4
TT_HG = 128
NB_HG = 8
TT_RET = 256
NB_RET = 4
TM_FFN = 1024
TF_FFN = 512
TM_MOE = 1024
TM_ROUTE = 512
TD_DISPATCH = 512
DMA_UNROLL = 8


def _cparams(*sem):
    return pltpu.CompilerParams(dimension_semantics=sem, vmem_limit_bytes=VMEM_LIMIT)


def _const_spec(shape):
    nd = len(shape)
    return pl.BlockSpec(shape, lambda *_: (0,) * nd, pipeline_mode=pl.Buffered(1))


def _layer_spec(arr, layer):
    shape = arr.shape[1:]
    nd = len(shape)
    return pl.BlockSpec((None,) + shape, lambda *_: (layer,) + (0,) * nd, pipeline_mode=pl.Buffered(1))


def _dot(a, b):
    return jnp.dot(a, b, preferred_element_type=F32)


def _dot_nt(a, b):
    return lax.dot_general(a, b, (((1,), (1,)), ((), ())), preferred_element_type=F32)


def _dot_tn(a, b):
    return lax.dot_general(a, b, (((0,), (0,)), ((), ())), preferred_element_type=F32)


def _split2(x):
    hi = x.astype(BF16)
    lo = (x - hi.astype(F32)).astype(BF16)
    return hi, lo


def _split3(x):
    h1 = x.astype(BF16)
    r1 = x - h1.astype(F32)
    h2 = r1.astype(BF16)
    h3 = (r1 - h2.astype(F32)).astype(BF16)
    return h1, h2, h3


def _dot_exact_rhs(x, m):
    hi, lo = _split2(x)
    return _dot(hi, m) + _dot(lo, m)


def _dot_exact_lhs(m, x):
    hi, lo = _split2(x)
    return _dot(m, hi) + _dot(m, lo)


def _sigmoid(x):
    return 1.0 / (1.0 + jnp.exp(-x))


def _layer_norm_rows(z, g, b):
    mu = jnp.mean(z, axis=-1, keepdims=True)
    zc = z - mu
    var = jnp.mean(zc * zc, axis=-1, keepdims=True)
    return zc * lax.rsqrt(var + EPS) * g + b


_HALF_M = MLA_ROPE // 2
_HALF_R = HEAD_DIM // 2


def _rope_kernel(pos_ref, freq_ref, cm_ref, sm_ref, cr_ref, sr_ref):
    ang = pos_ref[...] * freq_ref[...]
    lane = lax.broadcasted_iota(jnp.int32, ang.shape, 1)

    def mla_layout(table, nope_value):
        first = pltpu.roll(table, MLA_NOPE, axis=1)
        second = pltpu.roll(table, MLA_NOPE + _HALF_M, axis=1)
        return jnp.where(lane < MLA_NOPE, nope_value,
                         jnp.where(lane < MLA_NOPE + _HALF_M, first,
                                   jnp.where(lane < MLA_NOPE + MLA_ROPE, second, 0.0)))

    def ret_layout(table):
        quarter = [pltpu.roll(table, (q * _HALF_R - _HALF_M) % LANES, axis=1) for q in range(LANES // _HALF_R)]
        half = jnp.where(lane < _HALF_R, quarter[0],
                         jnp.where(lane < 2 * _HALF_R, quarter[1],
                                   jnp.where(lane < 3 * _HALF_R, quarter[2], quarter[3])))
        return jnp.concatenate([half, half], axis=1)

    cos, sin = jnp.cos(ang), jnp.sin(ang)
    cm_ref[...] = mla_layout(cos, 1.0)
    sm_ref[...] = mla_layout(sin, 0.0)
    cr_ref[...] = ret_layout(cos)
    sr_ref[...] = ret_layout(sin)


def rope_tables(positions):
    t = positions.size
    tm = TM_PROJ
    pos = positions.reshape(t, 1).astype(F32)
    inv_m = ROPE_BASE ** (-jnp.arange(_HALF_M, dtype=F32) / _HALF_M)
    inv_r = ROPE_BASE ** (-jnp.arange(_HALF_R, dtype=F32) / _HALF_R)
    freq = jnp.concatenate([inv_m, inv_r, jnp.zeros((LANES - _HALF_M - _HALF_R,), F32)]).reshape(1, LANES)
    row = lambda w: pl.BlockSpec((tm, w), lambda i: (i, 0))
    return pl.pallas_call(
        _rope_kernel,
        grid=(t // tm,),
        in_specs=[row(1), _const_spec((1, LANES))],
        out_specs=[row(LANES), row(LANES), row(256), row(256)],
        out_shape=[jax.ShapeDtypeStruct((t, LANES), F32), jax.ShapeDtypeStruct((t, LANES), F32),
                   jax.ShapeDtypeStruct((t, 256), F32), jax.ShapeDtypeStruct((t, 256), F32)],
        compiler_params=_cparams("parallel"),
        name="rope_tables",
    )(pos, freq)


W_MLA_IN = 640
W_RET_IN = 1536
_N_CHUNK = 256


def _inproj_kernel(x_ref, *refs):
    n = len(refs) // 2
    xb = x_ref[...].astype(BF16)
    for w_ref, o_ref in zip(refs[:n], refs[n:]):
        width = w_ref.shape[1]
        for c0 in range(0, width, _N_CHUNK):
            c1 = min(c0 + _N_CHUNK, width)
            o_ref[:, c0:c1] = _dot(xb, w_ref[:, c0:c1])


def in_projection(x, weights, layer):
    t = x.shape[0]
    tm = TM_PROJ
    return pl.pallas_call(
        _inproj_kernel,
        grid=(t // tm,),
        in_specs=[pl.BlockSpec((tm, D_MODEL), lambda i: (i, 0))] + [_layer_spec(w, layer) for w in weights],
        out_specs=[pl.BlockSpec((tm, w.shape[2]), lambda i: (i, 0)) for w in weights],
        out_shape=[jax.ShapeDtypeStruct((t, w.shape[2]), F32) for w in weights],
        compiler_params=_cparams("parallel"),
        name="in_projection",
    )(x, *weights)


S5_HALF = S5_GROUPS * S5_STATE


def _s5_kernel(u_ref, perm_ref, permt_ref, bmat_ref, ar_ref, ai_ref, cmat_ref, d_ref, wglu_ref, y_ref,
               bu_ref, hs_ref, h_ref):
    nb, ts, width = u_ref.shape
    n_half = S5_HALF // LANES

    @pl.when(pl.program_id(0) == 0)
    def _():
        h_ref[...] = jnp.zeros_like(h_ref)

    u = u_ref[...].reshape(nb * ts, width)
    u_tm = _dot(perm_ref[...], u.astype(BF16)).astype(BF16)
    bu_ref[...] = _dot(u_tm, bmat_ref[...])
    ar = [jnp.broadcast_to(ar_ref[:, c * LANES:(c + 1) * LANES], (nb, LANES)) for c in range(n_half)]
    ai = [jnp.broadcast_to(ai_ref[:, c * LANES:(c + 1) * LANES], (nb, LANES)) for c in range(n_half)]
    h_re = [h_ref[:, c * LANES:(c + 1) * LANES] for c in range(n_half)]
    h_im = [h_ref[:, S5_HALF + c * LANES:S5_HALF + (c + 1) * LANES] for c in range(n_half)]
    for n in range(ts):
        rows = slice(n * nb, (n + 1) * nb)
        for c in range(n_half):
            re_cols = slice(c * LANES, (c + 1) * LANES)
            im_cols = slice(S5_HALF + c * LANES, S5_HALF + (c + 1) * LANES)
            new_re = ar[c] * h_re[c] - ai[c] * h_im[c] + bu_ref[rows, re_cols]
            new_im = ar[c] * h_im[c] + ai[c] * h_re[c] + bu_ref[rows, im_cols]
            h_re[c], h_im[c] = new_re, new_im
            hs_ref[rows, re_cols] = new_re
            hs_ref[rows, im_cols] = new_im
    for c in range(n_half):
        h_ref[:, c * LANES:(c + 1) * LANES] = h_re[c]
        h_ref[:, S5_HALF + c * LANES:S5_HALF + (c + 1) * LANES] = h_im[c]
    y_tm = _dot(hs_ref[...].astype(BF16), cmat_ref[...])
    y = _dot_exact_lhs(permt_ref[...], y_tm) + d_ref[...] * u
    y = jax.nn.gelu(y)
    y = y * _sigmoid(_dot(y.astype(BF16), wglu_ref[...]))
    y_ref[...] = y.reshape(nb, ts, width)


def _step_major_permutation(nb, ts):
    p = np.zeros((nb * ts, nb * ts), np.float32)
    b, n = np.meshgrid(np.arange(nb), np.arange(ts), indexing="ij")
    p[(n * nb + b).ravel(), (b * ts + n).ravel()] = 1.0
    return jnp.asarray(p, BF16), jnp.asarray(p.T, BF16)


def s5_mixer(u, bsz, seq, prm, layer):
    ts = TS_S5
    u3 = u.reshape(bsz, seq, MIX_WIDTH)
    blk = pl.BlockSpec((bsz, ts, MIX_WIDTH), lambda i: (0, i, 0))
    perm, perm_t = _step_major_permutation(bsz, ts)
    names = ("bmat", "a_re", "a_im", "cmat", "d", "w_glu")
    y = pl.pallas_call(
        _s5_kernel,
        grid=(seq // ts,),
        in_specs=[blk, _const_spec(perm.shape), _const_spec(perm.shape)] +
                 [_layer_spec(prm[n], layer) for n in names],
        out_specs=blk,
        out_shape=jax.ShapeDtypeStruct((bsz, seq, MIX_WIDTH), F32),
        scratch_shapes=[pltpu.VMEM((bsz * ts, 2 * S5_HALF), F32), pltpu.VMEM((bsz * ts, 2 * S5_HALF), F32),
                        pltpu.VMEM((bsz, 2 * S5_HALF), F32)],
        compiler_params=_cparams("arbitrary"),
        name="s5_mixer",
    )(u3, perm, perm_t, *[prm[n] for n in names])
    return y.reshape(bsz * seq, MIX_WIDTH)


def s5_params(a_re, a_im, log_dt, b_re, b_im, c_re, c_im, d_skip, w_glu):
    dt = jnp.exp(log_dt)[:, None]
    mag = jnp.exp(a_re * dt)
    abar_re, abar_im = mag * jnp.cos(a_im * dt), mag * jnp.sin(a_im * dt)
    den = a_re * a_re + a_im * a_im
    num_re, num_im = abar_re - 1.0, abar_im
    coef_re = (num_re * a_re + num_im * a_im) / den
    coef_im = (num_im * a_re - num_re * a_im) / den
    bbar_re = coef_re[..., None] * b_re - coef_im[..., None] * b_im
    bbar_im = coef_re[..., None] * b_im + coef_im[..., None] * b_re
    eye = jnp.eye(S5_GROUPS, dtype=F32)
    b_bd_re = jnp.einsum("gpc,gh->gchp", bbar_re, eye).reshape(MIX_WIDTH, S5_HALF)
    b_bd_im = jnp.einsum("gpc,gh->gchp", bbar_im, eye).reshape(MIX_WIDTH, S5_HALF)
    c_bd_re = jnp.einsum("gcp,gh->gphc", c_re, eye).reshape(S5_HALF, MIX_WIDTH)
    c_bd_im = jnp.einsum("gcp,gh->gphc", c_im, eye).reshape(S5_HALF, MIX_WIDTH)
    return {
        "bmat": jnp.concatenate([b_bd_re, b_bd_im], axis=1).astype(BF16),
        "cmat": jnp.concatenate([c_bd_re, -c_bd_im], axis=0).astype(BF16),
        "a_re": abar_re.reshape(1, S5_HALF), "a_im": abar_im.reshape(1, S5_HALF),
        "d": d_skip.reshape(1, MIX_WIDTH), "w_glu": w_glu.astype(BF16),
    }


MLA_QK_PAD = 128


def _mla_prep_kernel(h_ref, cos_ref, sin_ref, gq_ref, gkv_ref, wq_ref, wqr_ref, wk_ref, wv_ref,
                     q_ref, k_ref, v_ref):
    h = h_ref[...]
    cos = cos_ref[...]
    sin = sin_ref[...]
    c_q = h[:, 0:256]
    c_kv = h[:, 256:384]
    k_pe = h[:, 384:512] * cos + h[:, 512:640] * sin
    qn = c_q * lax.rsqrt(jnp.sum(c_q * c_q, axis=-1, keepdims=True) / MLA_Q_LORA + EPS) * gq_ref[...]
    kvn = c_kv * lax.rsqrt(jnp.mean(c_kv * c_kv, axis=-1, keepdims=True) + EPS) * gkv_ref[...]
    qn = qn.astype(BF16)
    kvn = kvn.astype(BF16)
    scale = (MLA_NOPE + MLA_ROPE) ** -0.5 * math.log2(math.e)
    heads = lambda a: jnp.concatenate([a] * MLA_HEADS, axis=-1)
    q = (_dot(qn, wq_ref[...]) * heads(cos) + _dot(qn, wqr_ref[...]) * heads(sin)) * scale
    k = _dot(kvn, wk_ref[...]) + heads(k_pe)
    v = _dot(kvn, wv_ref[...])
    for hd in range(MLA_HEADS):
        q_ref[hd] = q[:, hd * MLA_QK_PAD:(hd + 1) * MLA_QK_PAD].astype(BF16)
        k_ref[hd] = k[:, hd * MLA_QK_PAD:(hd + 1) * MLA_QK_PAD].astype(BF16)
        v_ref[hd] = v[:, hd * MLA_V:(hd + 1) * MLA_V].astype(BF16)


def _flash_kernel(q_ref, k_ref, v_ref, o_ref, m_ref, l_ref, acc_ref):
    qi = pl.program_id(1)
    _, nb, tq, _ = q_ref.shape
    n_part = tq // LANES
    row = lax.broadcasted_iota(jnp.int32, (tq, tq), 0)
    col = lax.broadcasted_iota(jnp.int32, (tq, tq), 1)
    units = [(hd, r) for r in range(nb) for hd in range(MLA_HEADS)]

    def scores(hd, r, rows, masked):
        s = _dot_nt(q_ref[hd, r], k_ref[hd, r, rows, :])
        if masked:
            s = jnp.where(col <= row, s, NEG_INF)
        return [s[:, c * LANES:(c + 1) * LANES] for c in range(n_part)]

    def chunk(j, masked):
        rows = pl.ds(pl.multiple_of(j * tq, tq), tq)
        nxt = scores(*units[0], rows, masked)
        for u, (hd, r) in enumerate(units):
            parts = nxt
            if u + 1 < len(units):
                nxt = scores(*units[u + 1], rows, masked)
            m_old = m_ref[u]
            row_max = jnp.max(functools.reduce(jnp.maximum, parts), axis=-1, keepdims=True)
            m_new = jnp.maximum(m_old, row_max)
            p = [jnp.exp2(part - m_new) for part in parts]
            corr = jnp.exp2(m_old - m_new)
            l_ref[u] = corr * l_ref[u] + functools.reduce(jnp.add, p)
            m_ref[u] = m_new
            prob = jnp.concatenate(p, axis=-1).astype(BF16)
            acc_ref[u] = corr[:, :MLA_V] * acc_ref[u] + _dot(prob, v_ref[hd, r, rows, :])

    def body(j, carry):
        chunk(j, False)
        return carry

    m_ref[...] = jnp.full_like(m_ref, NEG_INF)
    l_ref[...] = jnp.zeros_like(l_ref)
    acc_ref[...] = jnp.zeros_like(acc_ref)
    lax.fori_loop(0, qi, body, 0)
    chunk(qi, True)
    for r in range(nb):
        o_ref[r] = jnp.concatenate(
            [acc_ref[r * MLA_HEADS + hd] / jnp.sum(l_ref[r * MLA_HEADS + hd], axis=-1, keepdims=True)
             for hd in range(MLA_HEADS)], axis=-1)


def mla_mixer(h_mla, cos_m, sin_m, bsz, seq, prm, layer):
    t = bsz * seq
    tm = TM_PROJ
    row = lambda w: pl.BlockSpec((tm, w), lambda i: (i, 0))
    hrow = lambda w: pl.BlockSpec((MLA_HEADS, tm, w), lambda i: (0, i, 0))
    names = ("gq", "gkv", "wq", "wqr", "wk", "wv")
    q, k, v = pl.pallas_call(
        _mla_prep_kernel,
        grid=(t // tm,),
        in_specs=[row(W_MLA_IN), row(LANES), row(LANES)] + [_layer_spec(prm[n], layer) for n in names],
        out_specs=[hrow(MLA_QK_PAD), hrow(MLA_QK_PAD), hrow(MLA_V)],
        out_shape=[jax.ShapeDtypeStruct((MLA_HEADS, t, MLA_QK_PAD), BF16),
                   jax.ShapeDtypeStruct((MLA_HEADS, t, MLA_QK_PAD), BF16),
                   jax.ShapeDtypeStruct((MLA_HEADS, t, MLA_V), BF16)],
        compiler_params=_cparams("parallel"),
        name="mla_prep",
    )(h_mla, cos_m, sin_m, *[prm[n] for n in names])

    tq = TQ_ATT
    nb = NB_ATT
    assert bsz % nb == 0 and seq % tq == 0
    n_unit = nb * MLA_HEADS
    by_row = lambda a: a.reshape(MLA_HEADS, bsz, seq, a.shape[-1])
    q_spec = pl.BlockSpec((MLA_HEADS, nb, tq, MLA_QK_PAD), lambda b, i: (0, b, i, 0))
    k_spec = pl.BlockSpec((MLA_HEADS, nb, seq, MLA_QK_PAD), lambda b, i: (0, b, 0, 0))
    v_spec = pl.BlockSpec((MLA_HEADS, nb, seq, MLA_V), lambda b, i: (0, b, 0, 0))
    y = pl.pallas_call(
        _flash_kernel,
        grid=(bsz // nb, seq // tq),
        in_specs=[q_spec, k_spec, v_spec],
        out_specs=pl.BlockSpec((nb, tq, MLA_HEADS * MLA_V), lambda b, i: (b, i, 0)),
        out_shape=jax.ShapeDtypeStruct((bsz, seq, MLA_HEADS * MLA_V), F32),
        scratch_shapes=[pltpu.VMEM((n_unit, tq, LANES), F32), pltpu.VMEM((n_unit, tq, LANES), F32),
                        pltpu.VMEM((n_unit, tq, MLA_V), F32)],
        compiler_params=_cparams("parallel", "arbitrary"),
        name="mla_flash",
    )(by_row(q), by_row(k), by_row(v))
    return y.reshape(t, MLA_HEADS * MLA_V)


def _rot_half_cols(w, width):
    shp = w.shape
    w = w.reshape(shp[0], -1, 2, width // 2)
    return jnp.stack([-w[:, :, 1], w[:, :, 0]], axis=2).reshape(shp)


def mla_params(q_norm, kv_norm, w_uq, w_ukv):
    dq = MLA_NOPE + MLA_ROPE
    row_pad = 256 - MLA_Q_LORA
    w_uq_h = w_uq.reshape(MLA_Q_LORA, MLA_HEADS, dq)
    wq = jnp.pad(w_uq_h, ((0, row_pad), (0, 0), (0, MLA_QK_PAD - dq)))
    pe = w_uq_h[:, :, MLA_NOPE:]
    pe_rot = jnp.concatenate([-pe[..., MLA_ROPE // 2:], pe[..., :MLA_ROPE // 2]], axis=-1)
    wqr = jnp.pad(pe_rot, ((0, row_pad), (0, 0), (MLA_NOPE, MLA_QK_PAD - dq)))
    w_kv_h = w_ukv.reshape(MLA_KV_LORA, MLA_HEADS, MLA_NOPE + MLA_V)
    wk = jnp.pad(w_kv_h[:, :, :MLA_NOPE], ((0, 0), (0, 0), (0, MLA_QK_PAD - MLA_NOPE)))
    wv = w_kv_h[:, :, MLA_NOPE:]
    gq = jnp.pad(q_norm, (0, row_pad)).reshape(1, 256)
    flat = lambda w: w.reshape(w.shape[0], -1).astype(BF16)
    return {"wq": flat(wq), "wqr": flat(wqr), "wk": flat(wk), "wv": flat(wv),
            "gq": gq, "gkv": kv_norm.reshape(1, MLA_KV_LORA)}


def _head_mask(shape, hd, axis):
    return (lax.broadcasted_iota(jnp.int32, shape, axis) // HEAD_DIM) == hd


def _hgrn_kernel(h_ref, lb_ref, ng_ref, aall_ref, lvl_ref, hm_ref, blk_ref, o_ref, st_ref, e_ref):
    @pl.when(pl.program_id(1) == 0)
    def _():
        st_ref[...] = jnp.zeros_like(st_ref)

    for row in range(h_ref.shape[0]):
        _hgrn_tile(h_ref.at[row], lb_ref, ng_ref, aall_ref, lvl_ref, hm_ref, blk_ref, o_ref.at[row],
                   st_ref.at[row], e_ref.at[row])


def _hgrn_tile(h_ref, lb_ref, ng_ref, aall_ref, lvl_ref, hm_ref, blk_ref, o_ref, st_ref, e_ref):
    tt = h_ref.shape[0]
    width = HEADS * HEAD_DIM
    n_lev = aall_ref.shape[0] // tt - 1

    q = h_ref[:, 0:width]
    f = h_ref[:, width:2 * width]
    v = h_ref[:, 2 * width:3 * width]
    g = h_ref[:, 3 * width:4 * width]
    lb = lb_ref[...]
    log_sig = jnp.minimum(f, 0.0) - jnp.log1p(jnp.exp(-jnp.abs(f)))
    a = jnp.log(lb)
    b = jnp.log1p(-lb) + log_sig
    log_f = jnp.maximum(a, b) + jnp.log1p(jnp.exp(-jnp.abs(a - b)))
    k = (1.0 - lb) * _sigmoid(-f)
    qf = q * _sigmoid(q)
    e2 = _dot(aall_ref[...], jnp.concatenate(_split2(log_f), axis=-1))
    e_ref[...] = e2[:, 0:width] + e2[:, width:2 * width]

    t_row = lax.broadcasted_iota(jnp.int32, (tt, width), 0)
    lvl = lvl_ref[...]
    attn = [jnp.zeros((tt, tt), F32) for _ in range(HEADS)]
    for lev in range(n_lev):
        x = jnp.exp(e_ref[lev * tt:(lev + 1) * tt, :])
        right = ((t_row >> lev) & 1) == 1
        qt = jnp.where(right, qf * x, 0.0).astype(BF16)
        kt = jnp.where(right, 0.0, k * x).astype(BF16)
        q_heads = jnp.concatenate([qt * hm_ref[hd] for hd in range(HEADS)], axis=0)
        s = _dot_nt(q_heads, kt)
        here = lvl == lev
        for hd in range(HEADS):
            attn[hd] = jnp.where(here, s[hd * tt:(hd + 1) * tt, :], attn[hd])
    vb = v.astype(BF16)
    o = jnp.zeros((tt, width), F32)
    for hd in range(HEADS):
        o = o + _dot(attn[hd].astype(BF16), vb * hm_ref[hd])
    ones_blk = blk_ref[...]
    o = o + _dot_exact_rhs(qf * k, ones_blk) * v
    bc = e_ref[n_lev * tt:(n_lev + 1) * tt, :]
    st = st_ref[...]
    st_m = jnp.where(ones_blk > 0, st, 0.0).astype(BF16)
    o = o + _dot_nt((qf * jnp.exp(bc)).astype(BF16), st_m)
    b_last = bc[tt - 1:tt, :]
    k_out = (k * jnp.exp(b_last - bc)).astype(BF16)
    st_ref[...] = st * jnp.exp(b_last) + _dot_tn(vb, k_out)
    ms = _dot_exact_rhs(o * o, ones_blk) * (1.0 / HEAD_DIM)
    o = o * lax.rsqrt(ms + EPS) * ng_ref[...]
    o_ref[...] = o * (g * _sigmoid(g))


def _hgrn_level_matrices(tt):
    n_lev = int(math.log2(tt))
    t = np.arange(tt)[:, None]
    j = np.arange(tt)[None, :]
    mats = []
    for lev in range(n_lev):
        half = 1 << lev
        mid = (t >> (lev + 1) << (lev + 1)) + half
        right = ((t >> lev) & 1) == 1
        mats.append(np.where(right, (j >= mid) & (j <= t), (j > t) & (j < mid)).astype(np.float32))
    mats.append((j <= t).astype(np.float32))
    level = np.full((tt, tt), -1, np.int32)
    lower = np.broadcast_to(j < t, (tt, tt))
    level[lower] = np.floor(np.log2((t ^ j)[lower])).astype(np.int32)
    return jnp.asarray(np.concatenate(mats, axis=0), BF16), jnp.asarray(level)


def _head_block_ones():
    i = np.arange(HEADS * HEAD_DIM)
    return jnp.asarray((i[:, None] // HEAD_DIM == i[None, :] // HEAD_DIM).astype(np.float32), BF16)


def _head_lane_masks(rows):
    i = np.arange(HEADS * HEAD_DIM)
    m = (i[None, :] // HEAD_DIM == np.arange(HEADS)[:, None]).astype(np.float32)
    return jnp.asarray(np.broadcast_to(m[:, None, :], (HEADS, rows, HEADS * HEAD_DIM)), BF16)


def hgrn2_mixer(h_hg, lb, norm_g, bsz, seq, layer):
    t = bsz * seq
    tt = TT_HG
    width = HEADS * HEAD_DIM
    aall, level = _hgrn_level_matrices(tt)
    nt = seq // tt
    nb = NB_HG
    assert bsz % nb == 0 and seq % tt == 0
    y = pl.pallas_call(
        _hgrn_kernel,
        grid=(bsz // nb, nt),
        in_specs=[pl.BlockSpec((nb, tt, 4 * width), lambda b, i: (b, i, 0)),
                  _layer_spec(lb, layer), _layer_spec(norm_g, layer),
                  _const_spec(aall.shape), _const_spec(level.shape), _const_spec((HEADS, tt, width)),
                  _const_spec((width, width))],
        out_specs=pl.BlockSpec((nb, tt, width), lambda b, i: (b, i, 0)),
        out_shape=jax.ShapeDtypeStruct((bsz, seq, width), F32),
        scratch_shapes=[pltpu.VMEM((nb, width, width), F32), pltpu.VMEM((nb, aall.shape[0], width), F32)],
        compiler_params=_cparams("parallel", "arbitrary"),
        name="hgrn2_mixer",
    )(h_hg.reshape(bsz, seq, 4 * width), lb, norm_g, aall, level, _head_lane_masks(tt), _head_block_ones())
    return y.reshape(t, width)


def _ret_kernel(h_ref, cos_ref, sin_ref, dmat_ref, qdec_ref, kdec_ref, cdec_ref, blk_ref, g_ref, b_ref,
                o_ref, st_ref):
    @pl.when(pl.program_id(1) == 0)
    def _():
        st_ref[...] = jnp.zeros_like(st_ref)

    for row in range(h_ref.shape[0]):
        _ret_tile(h_ref.at[row], cos_ref.at[row], sin_ref.at[row], dmat_ref, qdec_ref, kdec_ref, cdec_ref,
                  blk_ref, g_ref, b_ref, o_ref.at[row], st_ref.at[row])


def _ret_tile(h_ref, cos_ref, sin_ref, dmat_ref, qdec_ref, kdec_ref, cdec_ref, blk_ref, g_ref, b_ref,
              o_ref, st_ref):
    tt = h_ref.shape[0]
    width = HEADS * HEAD_DIM

    cos = cos_ref[...]
    sin = sin_ref[...]
    q = (h_ref[:, 0:width] * cos + h_ref[:, 4 * width:5 * width] * sin) * (HEAD_DIM ** -0.5)
    k = h_ref[:, width:2 * width] * cos + h_ref[:, 5 * width:6 * width] * sin
    v = h_ref[:, 2 * width:3 * width]
    g = h_ref[:, 3 * width:4 * width]
    kb = k.astype(BF16)
    vb = v.astype(BF16)
    ones_blk = blk_ref[...]
    o = jnp.zeros((tt, width), F32)
    for hd in range(HEADS):
        hm = _head_mask((tt, width), hd, 1)
        s = _dot_nt(jnp.where(hm, q, 0.0).astype(BF16), kb) * dmat_ref[hd]
        o = o + _dot(s.astype(BF16), jnp.where(hm, vb, jnp.zeros_like(vb)))
    st = st_ref[...]
    st_m = jnp.where(ones_blk > 0, st, 0.0).astype(BF16)
    o = o + _dot((q * qdec_ref[...]).astype(BF16), st_m)
    st_ref[...] = st * cdec_ref[...] + _dot_tn((k * kdec_ref[...]).astype(BF16), vb)
    mu = _dot_exact_rhs(o, ones_blk) * (1.0 / HEAD_DIM)
    oc = o - mu
    var = _dot_exact_rhs(oc * oc, ones_blk) * (1.0 / HEAD_DIM)
    o = oc * lax.rsqrt(var + EPS) * g_ref[...] + b_ref[...]
    o_ref[...] = (g * _sigmoid(g)) * o


def _retention_tables(tt):
    log_gamma = jnp.log(1.0 - 2.0 ** (-5.0 - jnp.arange(HEADS, dtype=F32)))
    t_idx = jnp.arange(tt, dtype=F32)
    rel = t_idx[:, None] - t_idx[None, :]
    dmat = jnp.where(rel >= 0, jnp.exp(jnp.maximum(rel, 0.0)[None] * log_gamma[:, None, None]), 0.0)
    lanes = lambda a: jnp.repeat(a, HEAD_DIM, axis=-1)
    qdec = lanes(jnp.exp((t_idx + 1.0)[:, None] * log_gamma[None, :]))
    kdec = lanes(jnp.exp((tt - 1 - t_idx)[:, None] * log_gamma[None, :]))
    cdec = lanes(jnp.exp(tt * log_gamma)[None, :])
    return dmat, qdec, kdec, cdec


def retention_mixer(h_ret, cos_r, sin_r, gn_g, gn_b, bsz, seq, layer):
    t = bsz * seq
    tt = TT_RET
    width = HEADS * HEAD_DIM
    nt = seq // tt
    dmat, qdec, kdec, cdec = _retention_tables(tt)
    nb = NB_RET
    assert bsz % nb == 0 and seq % tt == 0
    row = lambda w: pl.BlockSpec((nb, tt, w), lambda b, i: (b, i, 0))
    y = pl.pallas_call(
        _ret_kernel,
        grid=(bsz // nb, nt),
        in_specs=[row(W_RET_IN), row(width), row(width), _const_spec((HEADS, tt, tt)), _const_spec((tt, width)),
                  _const_spec((tt, width)), _const_spec((1, width)), _const_spec((width, width)),
                  _layer_spec(gn_g, layer), _layer_spec(gn_b, layer)],
        out_specs=row(width),
        out_shape=jax.ShapeDtypeStruct((bsz, seq, width), F32),
        scratch_shapes=[pltpu.VMEM((nb, width, width), F32)],
        compiler_params=_cparams("parallel", "arbitrary"),
        name="retention_mixer",
    )(h_ret.reshape(bsz, seq, W_RET_IN), cos_r.reshape(bsz, seq, width), sin_r.reshape(bsz, seq, width),
      dmat, qdec, kdec, cdec, _head_block_ones(), gn_g, gn_b)
    return y.reshape(t, width)


def _store_token_tiles(ref, val):
    tm = val.shape[0]
    for j in range(D_MODEL // LANES):
        ref[pl.ds(j, tm, stride=SUBLANES), :] = val[:, j * LANES:(j + 1) * LANES]


def _load_token_tiles(ref, tm):
    return jnp.concatenate([ref[pl.ds(j, tm, stride=SUBLANES), :] for j in range(D_MODEL // LANES)], axis=-1)


def _mix_kernel(x_ref, ya_ref, yb_ref, yc_ref, yd_ref, wg_ref, wb_ref, wo_ref, g_ref, b_ref, o_ref):
    x = x_ref[...]
    xb = x.astype(BF16)
    acc = jnp.zeros(x.shape, F32)
    for n, y_ref in enumerate((ya_ref, yb_ref, yc_ref, yd_ref)):
        gate = _sigmoid(_dot(xb, wg_ref[:, n * D_MODEL:(n + 1) * D_MODEL]))
        acc = acc + gate * _dot(y_ref[...].astype(BF16), wb_ref[n])
    z = ALPHA * x + _dot(acc.astype(BF16), wo_ref[...])
    o_ref[...] = _layer_norm_rows(z, g_ref[...], b_ref[...])


def mix_branches(x, ys, params, layer):
    t = x.shape[0]
    tm = TM_PROJ
    row = lambda w: pl.BlockSpec((tm, w), lambda i: (i, 0))
    return pl.pallas_call(
        _mix_kernel,
        grid=(t // tm,),
        in_specs=[row(D_MODEL)] + [row(MIX_WIDTH)] * 4 + [_layer_spec(a, layer) for a in params],
        out_specs=row(D_MODEL),
        out_shape=jax.ShapeDtypeStruct((t, D_MODEL), F32),
        compiler_params=_cparams("parallel"),
        name="mix_branches",
    )(x, *ys, *params)


def _ffn_kernel(x_ref, wg_ref, wu_ref, wd_ref, p_ref, wpg_ref, wpp_ref, g_ref, b_ref, o_ref, xb_ref, acc_ref):
    c = pl.program_id(1)

    @pl.when(c == 0)
    def _():
        xb_ref[...] = x_ref[...].astype(BF16)
        acc_ref[...] = jnp.zeros_like(acc_ref)

    xb = xb_ref[...]
    gate = _dot(xb, wg_ref[...])
    act = gate * _sigmoid(gate) * _dot(xb, wu_ref[...])
    acc_ref[...] += _dot(act.astype(BF16), wd_ref[...])

    @pl.when(c == pl.num_programs(1) - 1)
    def _():
        ple = _sigmoid(_dot(xb, wpg_ref[...])) * _dot(p_ref[...].astype(BF16), wpp_ref[...])
        o_ref[...] = _layer_norm_rows(ALPHA * x_ref[...] + acc_ref[...] + ple, g_ref[...], b_ref[...])


def dense_ffn_layer(x, ffn_w, ffn_layer, p, layer, post_params):
    t = x.shape[0]
    tm, tf = TM_FFN, TF_FFN
    return pl.pallas_call(
        _ffn_kernel,
        grid=(t // tm, D_FF // tf),
        in_specs=[pl.BlockSpec((tm, D_MODEL), lambda i, c: (i, 0)),
                  pl.BlockSpec((None, D_MODEL, tf), lambda i, c: (ffn_layer, 0, c)),
                  pl.BlockSpec((None, D_MODEL, tf), lambda i, c: (ffn_layer, 0, c)),
                  pl.BlockSpec((None, tf, D_MODEL), lambda i, c: (ffn_layer, c, 0)),
                  pl.BlockSpec((None, tm, PLE_DIM), lambda i, c: (layer, i, 0))] +
                 [_layer_spec(a, layer) for a in post_params],
        out_specs=pl.BlockSpec((tm, D_MODEL), lambda i, c: (i, 0)),
        out_shape=jax.ShapeDtypeStruct((t, D_MODEL), F32),
        scratch_shapes=[pltpu.VMEM((tm, D_MODEL), BF16), pltpu.VMEM((tm, D_MODEL), F32)],
        compiler_params=_cparams("parallel", "arbitrary"),
        name="dense_ffn",
    )(x, *ffn_w, p, *post_params)


def _router_kernel(x_ref, wh_ref, wl_ref, upper_ref, meta_ref, cnt_ref, slots_ref, carry_ref):
    tm = x_ref.shape[0]
    slots_ref[...] = jnp.zeros_like(slots_ref)

    @pl.when(pl.program_id(0) == 0)
    def _():
        carry_ref[...] = jnp.zeros_like(carry_ref)

    xh, xl = _split2(x_ref[...])
    wh = wh_ref[...]
    logits = _dot_nt(wh, xh) + _dot_nt(wh, xl) + _dot_nt(wl_ref[...], xh)
    e_idx = lax.broadcasted_iota(jnp.int32, logits.shape, 0)
    m1 = jnp.max(logits, axis=0, keepdims=True)
    i1 = jnp.min(jnp.where(logits == m1, e_idx, N_EXPERTS), axis=0, keepdims=True)
    rest = jnp.where(e_idx == i1, -jnp.inf, logits)
    m2 = jnp.max(rest, axis=0, keepdims=True)
    i2 = jnp.min(jnp.where(rest == m2, e_idx, N_EXPERTS), axis=0, keepdims=True)
    ex = jnp.exp(m2 - m1)
    w1 = 1.0 / (1.0 + ex)
    w2 = ex / (1.0 + ex)
    sel1 = e_idx == i1
    sel2 = e_idx == i2
    onehot = jnp.where(sel1 | sel2, 1.0, 0.0)
    carry = carry_ref[...]
    ranks = _dot(onehot.astype(BF16), upper_ref[...]) + carry[:, 0:1]
    r1 = jnp.sum(jnp.where(sel1, ranks, 0.0), axis=0, keepdims=True)
    r2 = jnp.sum(jnp.where(sel2, ranks, 0.0), axis=0, keepdims=True)
    row = lax.broadcasted_iota(jnp.int32, (SUBLANES, tm), 0)
    vals = (i1.astype(F32), i2.astype(F32), w1, w2, r1, r2)
    meta = jnp.zeros((SUBLANES, tm), F32)
    for n, val in enumerate(vals):
        meta = jnp.where(row == n, val, meta)
    meta_ref[...] = meta
    carry = carry + jnp.sum(onehot, axis=1, keepdims=True)
    carry_ref[...] = carry
    cnt_ref[...] = carry


def moe_router(x, w_router, p_len):
    t = x.shape[0]
    tm = TM_ROUTE
    slot_rows = p_len * SUBLANES // (t // tm)
    assert slot_rows % SUBLANES == 0 and slot_rows * (t // tm) == p_len * SUBLANES
    wt = w_router.T
    wh = wt.astype(BF16)
    wl = (wt - wh.astype(F32)).astype(BF16)
    upper = jnp.asarray(np.triu(np.ones((tm, tm), np.float32), 1), BF16)
    return pl.pallas_call(
        _router_kernel,
        grid=(t // tm,),
        in_specs=[pl.BlockSpec((tm, D_MODEL), lambda i: (i, 0)), _const_spec((N_EXPERTS, D_MODEL)),
                  _const_spec((N_EXPERTS, D_MODEL)), _const_spec((tm, tm))],
        out_specs=[pl.BlockSpec((SUBLANES, tm), lambda i: (0, i)), pl.BlockSpec((N_EXPERTS, LANES), lambda i: (0, 0)),
                   pl.BlockSpec((slot_rows, LANES), lambda i: (i, 0))],
        out_shape=[jax.ShapeDtypeStruct((SUBLANES, t), F32), jax.ShapeDtypeStruct((N_EXPERTS, LANES), F32),
                   jax.ShapeDtypeStruct((p_len * SUBLANES, LANES), F32)],
        scratch_shapes=[pltpu.VMEM((N_EXPERTS, LANES), F32)],
        compiler_params=_cparams("arbitrary"),
        name="moe_router",
    )(x, wh, wl, upper)


def _token_rows(ref, tok):
    return ref.at[pl.ds(pl.multiple_of(tok * SUBLANES, SUBLANES), SUBLANES), :]


def _start_then_wait(n, copies):
    def start(r, carry):
        for idx, cp in enumerate(copies(r)):
            cp.start(priority=idx % 2)
        return carry

    def wait(r, carry):
        for cp in copies(r):
            cp.wait()
        return carry

    lax.fori_loop(0, n, start, 0, unroll=DMA_UNROLL)
    return lambda: lax.fori_loop(0, n, wait, 0, unroll=DMA_UNROLL)


def _dispatch_kernel(dest_ref, x_ref, init_ref, xs_ref, buf_ref, sem):
    del init_ref
    td = x_ref.shape[0]
    _store_token_tiles(buf_ref, x_ref[...])

    def copies(r):
        src = _token_rows(buf_ref, r)
        return (pltpu.make_async_copy(src, _token_rows(xs_ref, dest_ref[0, 0, 2 * r]), sem),
                pltpu.make_async_copy(src, _token_rows(xs_ref, dest_ref[0, 0, 2 * r + 1]), sem))

    _start_then_wait(td, copies)()


def moe_dispatch(x, dest3, init):
    t = x.shape[0]
    td = dest3.shape[2] // 2
    return pl.pallas_call(
        _dispatch_kernel,
        grid=(t // td,),
        in_specs=[pl.BlockSpec((1, 1, 2 * td), lambda i: (i, 0, 0), memory_space=pltpu.SMEM),
                  pl.BlockSpec((td, D_MODEL), lambda i: (i, 0)), pl.BlockSpec(memory_space=pl.ANY)],
        out_specs=pl.BlockSpec(memory_space=pl.ANY),
        out_shape=jax.ShapeDtypeStruct(init.shape, F32),
        scratch_shapes=[pltpu.VMEM((td * SUBLANES, LANES), F32), pltpu.SemaphoreType.DMA(())],
        input_output_aliases={2: 0},
        compiler_params=pltpu.CompilerParams(dimension_semantics=("arbitrary",), has_side_effects=True),
        name="moe_dispatch",
    )(dest3, x, init)


def _expert_kernel(te_ref, nv_ref, tr_ref, xs_ref, wg_ref, wu_ref, wd_ref, ys_ref, xb_ref, acc_ref):
    i = pl.program_id(0)
    c = pl.program_id(1)
    tm = xb_ref.shape[0]
    valid = i < nv_ref[0]
    upper_used = tr_ref[i] > tm // 2

    @pl.when(valid & (c == 0))
    def _():
        xb_ref[...] = _load_token_tiles(xs_ref, tm).astype(BF16)
        acc_ref[...] = jnp.zeros_like(acc_ref)

    def swiglu_rows(m):
        xb = xb_ref[:m, :]
        gate = _dot(xb, wg_ref[...].astype(BF16))
        act = gate * _sigmoid(gate) * _dot(xb, wu_ref[...].astype(BF16))
        acc_ref[:m, :] += _dot(act.astype(BF16), wd_ref[...].astype(BF16))

    @pl.when(valid & upper_used)
    def _():
        swiglu_rows(tm)

    @pl.when(valid & jnp.logical_not(upper_used))
    def _():
        swiglu_rows(tm // 2)

    @pl.when(c == pl.num_programs(1) - 1)
    def _():
        _store_token_tiles(ys_ref, jnp.where(valid, acc_ref[...], 0.0))


def moe_experts(xs, tile_expert, n_valid, tile_rows, layer, w_gate, w_up, w_down):
    tm, tf = TM_MOE, TF_FFN
    n_tiles = xs.shape[0] // (tm * SUBLANES)
    nc = D_FF // tf

    def chunk(i, c, nv):
        return jnp.where(i < nv[0], c, nc - 1)

    grid_spec = pltpu.PrefetchScalarGridSpec(
        num_scalar_prefetch=3,
        grid=(n_tiles, nc),
        in_specs=[pl.BlockSpec((tm * SUBLANES, LANES), lambda i, c, te, nv, tr: (jnp.minimum(i, nv[0] - 1), 0)),
                  pl.BlockSpec((None, None, D_MODEL, tf), lambda i, c, te, nv, tr: (layer, te[i], 0, chunk(i, c, nv))),
                  pl.BlockSpec((None, None, D_MODEL, tf), lambda i, c, te, nv, tr: (layer, te[i], 0, chunk(i, c, nv))),
                  pl.BlockSpec((None, None, tf, D_MODEL), lambda i, c, te, nv, tr: (layer, te[i], chunk(i, c, nv), 0))],
        out_specs=pl.BlockSpec((tm * SUBLANES, LANES), lambda i, c, te, nv, tr: (i, 0)),
        scratch_shapes=[pltpu.VMEM((tm, D_MODEL), BF16), pltpu.VMEM((tm, D_MODEL), F32)],
    )
    return pl.pallas_call(
        _expert_kernel,
        grid_spec=grid_spec,
        out_shape=jax.ShapeDtypeStruct(xs.shape, F32),
        compiler_params=_cparams("arbitrary", "arbitrary"),
        name="moe_experts",
    )(tile_expert, n_valid, tile_rows, xs, w_gate, w_up, w_down)


def moe_ffn(x, w_router, layer, w_gate, w_up, w_down):
    t = x.shape[0]
    tm = TM_MOE
    n_tiles = (2 * t) // tm + N_EXPERTS
    meta, cnt, slots = moe_router(x, w_router, n_tiles * tm)
    counts = cnt[:, 0].astype(jnp.int32)
    padded = (counts + tm - 1) // tm * tm
    pad_end = jnp.cumsum(padded)
    pad_start = pad_end - padded
    experts = meta[0:2].astype(jnp.int32)
    start_of = jnp.sum(jnp.where(experts[..., None] == jnp.arange(N_EXPERTS), pad_start, 0), axis=-1)
    dest = (start_of + meta[4:6].astype(jnp.int32)).T
    dest3 = dest.reshape(t // TD_DISPATCH, 1, 2 * TD_DISPATCH)
    n_valid = (pad_end[-1] // tm).astype(jnp.int32).reshape(1)
    tile_start = jnp.arange(n_tiles, dtype=jnp.int32) * tm
    tile_expert = jnp.minimum(jnp.sum(pad_end[None, :] <= tile_start[:, None], axis=1), N_EXPERTS - 1)
    last_expert = jnp.sum(jnp.where(jnp.arange(n_tiles) == n_valid[0] - 1, tile_expert, 0))
    tile_expert = jnp.where(jnp.arange(n_tiles) < n_valid[0], tile_expert, last_expert).astype(jnp.int32)
    of_tile = lambda per_expert: jnp.sum(jnp.where(tile_expert[:, None] == jnp.arange(N_EXPERTS), per_expert, 0), -1)
    tile_rows = jnp.clip(of_tile(pad_start + counts) - tile_start, 0, tm).astype(jnp.int32)
    xs = moe_dispatch(x, dest3, slots)
    ys = moe_experts(xs, tile_expert, n_valid, tile_rows, layer, w_gate, w_up, w_down)
    return ys, dest3, meta[2:4].T


def _post_moe_kernel(dest_ref, x_ref, p_ref, wpg_ref, wpp_ref, g_ref, b_ref, w_ref, ys_ref, o_ref,
                     buf0_ref, buf1_ref, sem):
    x = x_ref[...]
    tm = x.shape[0]

    def copies(r):
        return (pltpu.make_async_copy(_token_rows(ys_ref, dest_ref[0, 0, 2 * r]), _token_rows(buf0_ref, r), sem),
                pltpu.make_async_copy(_token_rows(ys_ref, dest_ref[0, 0, 2 * r + 1]), _token_rows(buf1_ref, r), sem))

    wait_all = _start_then_wait(tm, copies)
    ple = _sigmoid(_dot(x.astype(BF16), wpg_ref[...])) * _dot(p_ref[...].astype(BF16), wpp_ref[...])
    wait_all()
    w = w_ref[...]
    f = w[:, 0:1] * _load_token_tiles(buf0_ref, tm) + w[:, 1:2] * _load_token_tiles(buf1_ref, tm)
    o_ref[...] = _layer_norm_rows(ALPHA * x + f + ple, g_ref[...], b_ref[...])


def moe_post_layer(x, p, layer, params, moe):
    t = x.shape[0]
    tm = TD_DISPATCH
    ys, dest3, w2 = moe
    row = lambda w: pl.BlockSpec((tm, w), lambda i: (i, 0))
    return pl.pallas_call(
        _post_moe_kernel,
        grid=(t // tm,),
        in_specs=[pl.BlockSpec((1, 1, 2 * tm), lambda i: (i, 0, 0), memory_space=pltpu.SMEM),
                  row(D_MODEL), pl.BlockSpec((None, tm, PLE_DIM), lambda i: (layer, i, 0))] +
                 [_layer_spec(a, layer) for a in params] + [row(2), pl.BlockSpec(memory_space=pl.ANY)],
        out_specs=row(D_MODEL),
        out_shape=jax.ShapeDtypeStruct((t, D_MODEL), F32),
        scratch_shapes=[pltpu.VMEM((tm * SUBLANES, LANES), F32), pltpu.VMEM((tm * SUBLANES, LANES), F32),
                        pltpu.SemaphoreType.DMA(())],
        compiler_params=_cparams("arbitrary"),
        name="moe_post_layer",
    )(dest3, x, p, *params, w2, ys)


def split_in_weights(w):
    sizes = (MIX_WIDTH, MLA_Q_LORA, MLA_KV_LORA, MLA_ROPE) + (HEADS * HEAD_DIM,) * 8 + (N_BRANCH * D_MODEL,)
    parts, start = [], 0
    for n in sizes:
        parts.append(w[:, start:start + n])
        start += n
    u_s5, c_q, c_kv, k_rope, hq, hf, hi, hg, rq, rk, rv, rg, gate = parts
    k_rot = jnp.concatenate([-k_rope[:, MLA_ROPE // 2:], k_rope[:, :MLA_ROPE // 2]], axis=1)
    zeros = lambda n: jnp.zeros((D_MODEL, n), w.dtype)
    rope_slot = lambda cols: [zeros(MLA_NOPE), cols, zeros(MLA_QK_PAD - MLA_NOPE - MLA_ROPE)]
    w_mla = jnp.concatenate([c_q, zeros(256 - MLA_Q_LORA), c_kv] + rope_slot(k_rope) + rope_slot(k_rot), axis=1)
    w_hg = jnp.concatenate([hq, hf, hi, hg], axis=1)
    w_ret = jnp.concatenate([rq, rk, rv, rg, _rot_half_cols(rq, HEAD_DIM), _rot_half_cols(rk, HEAD_DIM)], axis=1)
    return [a.astype(BF16) for a in (w_mla, u_s5, w_hg, w_ret)], gate.astype(BF16)


def kernel(x, p, positions, w_in, s5_a_re, s5_a_im, s5_log_dt, s5_b_re, s5_b_im, s5_c_re, s5_c_im, s5_d, s5_w_glu, mla_q_norm, mla_kv_norm, mla_w_uq, mla_w_ukv, hg_lb_raw, hg_norm, ret_gn_g, ret_gn_b, w_branch, w_o, ln1_g, ln1_b, ff_w_gate, ff_w_up, ff_w_down, moe_router, moe_w_gate, moe_w_up, moe_w_down, ple_w_gate, ple_w_proj, ln2_g, ln2_b):
    bsz, seq, _ = x.shape
    t = bsz * seq
    depth = w_in.shape[0]
    x = x.reshape(t, D_MODEL)
    p = p.reshape(depth, t, PLE_DIM)
    cos_m, sin_m, cos_r, sin_r = rope_tables(positions)
    lb_all = jnp.cumsum(jax.nn.softmax(hg_lb_raw.astype(F32), axis=0), axis=0)
    lb_all = lb_all - lb_all[0]
    rows = lambda a: a.astype(F32).reshape(depth, 1, -1)
    in_ws, w_gate = jax.vmap(split_in_weights)(w_in)
    s5_prm = jax.vmap(s5_params)(s5_a_re, s5_a_im, s5_log_dt, s5_b_re, s5_b_im, s5_c_re, s5_c_im, s5_d, s5_w_glu)
    mla_prm = jax.vmap(mla_params)(mla_q_norm, mla_kv_norm, mla_w_uq, mla_w_ukv)
    mix_prm = (w_gate, w_branch.astype(BF16), w_o.astype(BF16), rows(ln1_g), rows(ln1_b))
    post_prm = (ple_w_gate.astype(BF16), ple_w_proj.astype(BF16), rows(ln2_g), rows(ln2_b))
    ffn_w = (ff_w_gate.astype(BF16), ff_w_up.astype(BF16), ff_w_down.astype(BF16))
    lb_rows, hg_norm_rows, gn_g_rows, gn_b_rows = rows(lb_all), rows(hg_norm), rows(ret_gn_g), rows(ret_gn_b)
    for i in range(depth):
        h_mla, h_s5, h_hg, h_ret = in_projection(x, in_ws, i)
        y_a = s5_mixer(h_s5, bsz, seq, s5_prm, i)
        y_b = mla_mixer(h_mla, cos_m, sin_m, bsz, seq, mla_prm, i)
        y_c = hgrn2_mixer(h_hg, lb_rows, hg_norm_rows, bsz, seq, i)
        y_d = retention_mixer(h_ret, cos_r, sin_r, gn_g_rows, gn_b_rows, bsz, seq, i)
        x = mix_branches(x, (y_a, y_b, y_c, y_d), mix_prm, i)
        if i % 2 == 1:
            x = moe_post_layer(x, p, i, post_prm, moe_ffn(x, moe_router[i // 2], i // 2, moe_w_gate, moe_w_up,
                                                          moe_w_down))
        else:
            x = dense_ffn_layer(x, ffn_w, i // 2, p, i, post_prm)
    return x.reshape(bsz, seq, D_MODEL)
```

```python
import functools
import math

import numpy as np
import jax
import jax.numpy as jnp
from jax import lax
from jax.experimental import pallas as pl
from jax.experimental.pallas import tpu as pltpu

F32 = jnp.float32
BF16 = jnp.bfloat16

D_MODEL = 1024
DEPTH = 4
PLE_DIM = 256
MIX_WIDTH = 256
N_BRANCH = 4
S5_GROUP = 16
S5_GROUPS = 16
S5_STATE = 64
MLA_HEADS = 4
MLA_NOPE = 64
MLA_ROPE = 32
MLA_V = 64
MLA_Q_LORA = 192
MLA_KV_LORA = 128
HEADS = 4
HEAD_DIM = 64
D_FF = 3584
N_EXPERTS = 8
ROPE_BASE = 10000.0
EPS = 1e-5
NEG_INF = -1e30
ALPHA = (2 * DEPTH) ** 0.25

LANES = 128
SUBLANES = 8
VMEM_LIMIT = 56 * 1024 * 1024

TM_PROJ = 512
TS_S5 = 64
TQ_ATT = 256
NB_ATT = 4
TT_HG = 128
NB_HG = 8
TT_RET = 256
NB_RET = 4
TM_FFN = 1024
TF_FFN = 512
TM_MOE = 1024
TM_ROUTE = 512
TD_DISPATCH = 512
DMA_UNROLL = 8


def _cparams(*sem):
    return pltpu.CompilerParams(dimension_semantics=sem, vmem_limit_bytes=VMEM_LIMIT)


def _const_spec(shape):
    nd = len(shape)
    return pl.BlockSpec(shape, lambda *_: (0,) * nd, pipeline_mode=pl.Buffered(1))


def _layer_spec(arr, layer):
    shape = arr.shape[1:]
    nd = len(shape)
    return pl.BlockSpec((None,) + shape, lambda *_: (layer,) + (0,) * nd, pipeline_mode=pl.Buffered(1))


def _dot(a, b):
    return jnp.dot(a, b, preferred_element_type=F32)


def _dot_nt(a, b):
    return lax.dot_general(a, b, (((1,), (1,)), ((), ())), preferred_element_type=F32)


def _dot_tn(a, b):
    return lax.dot_general(a, b, (((0,), (0,)), ((), ())), preferred_element_type=F32)


def _split2(x):
    hi = x.astype(BF16)
    lo = (x - hi.astype(F32)).astype(BF16)
    return hi, lo


def _split3(x):
    h1 = x.astype(BF16)
    r1 = x - h1.astype(F32)
    h2 = r1.astype(BF16)
    h3 = (r1 - h2.astype(F32)).astype(BF16)
    return h1, h2, h3


def _dot_exact_rhs(x, m):
    hi, lo = _split2(x)
    return _dot(hi, m) + _dot(lo, m)


def _dot_exact_lhs(m, x):
    hi, lo = _split2(x)
    return _dot(m, hi) + _dot(m, lo)


def _sigmoid(x):
    return 1.0 / (1.0 + jnp.exp(-x))


def _layer_norm_rows(z, g, b):
    mu = jnp.mean(z, axis=-1, keepdims=True)
    zc = z - mu
    var = jnp.mean(zc * zc, axis=-1, keepdims=True)
    return zc * lax.rsqrt(var + EPS) * g + b


_HALF_M = MLA_ROPE // 2
_HALF_R = HEAD_DIM // 2


def _rope_kernel(pos_ref, freq_ref, cm_ref, sm_ref, cr_ref, sr_ref):
    ang = pos_ref[...] * freq_ref[...]
    lane = lax.broadcasted_iota(jnp.int32, ang.shape, 1)

    def mla_layout(table, nope_value):
        first = pltpu.roll(table, MLA_NOPE, axis=1)
        second = pltpu.roll(table, MLA_NOPE + _HALF_M, axis=1)
        return jnp.where(lane < MLA_NOPE, nope_value,
                         jnp.where(lane < MLA_NOPE + _HALF_M, first,
                                   jnp.where(lane < MLA_NOPE + MLA_ROPE, second, 0.0)))

    def ret_layout(table):
        quarter = [pltpu.roll(table, (q * _HALF_R - _HALF_M) % LANES, axis=1) for q in range(LANES // _HALF_R)]
        half = jnp.where(lane < _HALF_R, quarter[0],
                         jnp.where(lane < 2 * _HALF_R, quarter[1],
                                   jnp.where(lane < 3 * _HALF_R, quarter[2], quarter[3])))
        return jnp.concatenate([half, half], axis=1)

    cos, sin = jnp.cos(ang), jnp.sin(ang)
    cm_ref[...] = mla_layout(cos, 1.0)
    sm_ref[...] = mla_layout(sin, 0.0)
    cr_ref[...] = ret_layout(cos)
    sr_ref[...] = ret_layout(sin)


def rope_tables(positions):
    t = positions.size
    tm = TM_PROJ
    pos = positions.reshape(t, 1).astype(F32)
    inv_m = ROPE_BASE ** (-jnp.arange(_HALF_M, dtype=F32) / _HALF_M)
    inv_r = ROPE_BASE ** (-jnp.arange(_HALF_R, dtype=F32) / _HALF_R)
    freq = jnp.concatenate([inv_m, inv_r, jnp.zeros((LANES - _HALF_M - _HALF_R,), F32)]).reshape(1, LANES)
    row = lambda w: pl.BlockSpec((tm, w), lambda i: (i, 0))
    return pl.pallas_call(
        _rope_kernel,
        grid=(t // tm,),
        in_specs=[row(1), _const_spec((1, LANES))],
        out_specs=[row(LANES), row(LANES), row(256), row(256)],
        out_shape=[jax.ShapeDtypeStruct((t, LANES), F32), jax.ShapeDtypeStruct((t, LANES), F32),
                   jax.ShapeDtypeStruct((t, 256), F32), jax.ShapeDtypeStruct((t, 256), F32)],
        compiler_params=_cparams("parallel"),
        name="rope_tables",
    )(pos, freq)


W_MLA_IN = 640
W_RET_IN = 1536
_N_CHUNK = 256


def _inproj_kernel(x_ref, *refs):
    n = len(refs) // 2
    xb = x_ref[...].astype(BF16)
    for w_ref, o_ref in zip(refs[:n], refs[n:]):
        width = w_ref.shape[1]
        for c0 in range(0, width, _N_CHUNK):
            c1 = min(c0 + _N_CHUNK, width)
            o_ref[:, c0:c1] = _dot(xb, w_ref[:, c0:c1])


def in_projection(x, weights, layer):
    t = x.shape[0]
    tm = TM_PROJ
    return pl.pallas_call(
        _inproj_kernel,
        grid=(t // tm,),
        in_specs=[pl.BlockSpec((tm, D_MODEL), lambda i: (i, 0))] + [_layer_spec(w, layer) for w in weights],
        out_specs=[pl.BlockSpec((tm, w.shape[2]), lambda i: (i, 0)) for w in weights],
        out_shape=[jax.ShapeDtypeStruct((t, w.shape[2]), F32) for w in weights],
        compiler_params=_cparams("parallel"),
        name="in_projection",
    )(x, *weights)


S5_HALF = S5_GROUPS * S5_STATE


def _s5_kernel(u_ref, perm_ref, permt_ref, bmat_ref, ar_ref, ai_ref, cmat_ref, d_ref, wglu_ref, y_ref,
               bu_ref, hs_ref, h_ref):
    nb, ts, width = u_ref.shape
    n_half = S5_HALF // LANES

    @pl.when(pl.program_id(0) == 0)
    def _():
        h_ref[...] = jnp.zeros_like(h_ref)

    u = u_ref[...].reshape(nb * ts, width)
    u_tm = _dot(perm_ref[...], u.astype(BF16)).astype(BF16)
    bu_ref[...] = _dot(u_tm, bmat_ref[...])
    ar = [jnp.broadcast_to(ar_ref[:, c * LANES:(c + 1) * LANES], (nb, LANES)) for c in range(n_half)]
    ai = [jnp.broadcast_to(ai_ref[:, c * LANES:(c + 1) * LANES], (nb, LANES)) for c in range(n_half)]
    h_re = [h_ref[:, c * LANES:(c + 1) * LANES] for c in range(n_half)]
    h_im = [h_ref[:, S5_HALF + c * LANES:S5_HALF + (c + 1) * LANES] for c in range(n_half)]
    for n in range(ts):
        rows = slice(n * nb, (n + 1) * nb)
        for c in range(n_half):
            re_cols = slice(c * LANES, (c + 1) * LANES)
            im_cols = slice(S5_HALF + c * LANES, S5_HALF + (c + 1) * LANES)
            new_re = ar[c] * h_re[c] - ai[c] * h_im[c] + bu_ref[rows, re_cols]
            new_im = ar[c] * h_im[c] + ai[c] * h_re[c] + bu_ref[rows, im_cols]
            h_re[c], h_im[c] = new_re, new_im
            hs_ref[rows, re_cols] = new_re
            hs_ref[rows, im_cols] = new_im
    for c in range(n_half):
        h_ref[:, c * LANES:(c + 1) * LANES] = h_re[c]
        h_ref[:, S5_HALF + c * LANES:S5_HALF + (c + 1) * LANES] = h_im[c]
    y_tm = _dot(hs_ref[...].astype(BF16), cmat_ref[...])
    y = _dot_exact_lhs(permt_ref[...], y_tm) + d_ref[...] * u
    y = jax.nn.gelu(y)
    y = y * _sigmoid(_dot(y.astype(BF16), wglu_ref[...]))
    y_ref[...] = y.reshape(nb, ts, width)


def _step_major_permutation(nb, ts):
    p = np.zeros((nb * ts, nb * ts), np.float32)
    b, n = np.meshgrid(np.arange(nb), np.arange(ts), indexing="ij")
    p[(n * nb + b).ravel(), (b * ts + n).ravel()] = 1.0
    return jnp.asarray(p, BF16), jnp.asarray(p.T, BF16)


def s5_mixer(u, bsz, seq, prm, layer):
    ts = TS_S5
    u3 = u.reshape(bsz, seq, MIX_WIDTH)
    blk = pl.BlockSpec((bsz, ts, MIX_WIDTH), lambda i: (0, i, 0))
    perm, perm_t = _step_major_permutation(bsz, ts)
    names = ("bmat", "a_re", "a_im", "cmat", "d", "w_glu")
    y = pl.pallas_call(
        _s5_kernel,
        grid=(seq // ts,),
        in_specs=[blk, _const_spec(perm.shape), _const_spec(perm.shape)] +
                 [_layer_spec(prm[n], layer) for n in names],
        out_specs=blk,
        out_shape=jax.ShapeDtypeStruct((bsz, seq, MIX_WIDTH), F32),
        scratch_shapes=[pltpu.VMEM((bsz * ts, 2 * S5_HALF), F32), pltpu.VMEM((bsz * ts, 2 * S5_HALF), F32),
                        pltpu.VMEM((bsz, 2 * S5_HALF), F32)],
        compiler_params=_cparams("arbitrary"),
        name="s5_mixer",
    )(u3, perm, perm_t, *[prm[n] for n in names])
    return y.reshape(bsz * seq, MIX_WIDTH)


def s5_params(a_re, a_im, log_dt, b_re, b_im, c_re, c_im, d_skip, w_glu):
    dt = jnp.exp(log_dt)[:, None]
    mag = jnp.exp(a_re * dt)
    abar_re, abar_im = mag * jnp.cos(a_im * dt), mag * jnp.sin(a_im * dt)
    den = a_re * a_re + a_im * a_im
    num_re, num_im = abar_re - 1.0, abar_im
    coef_re = (num_re * a_re + num_im * a_im) / den
    coef_im = (num_im * a_re - num_re * a_im) / den
    bbar_re = coef_re[..., None] * b_re - coef_im[..., None] * b_im
    bbar_im = coef_re[..., None] * b_im + coef_im[..., None] * b_re
    eye = jnp.eye(S5_GROUPS, dtype=F32)
    b_bd_re = jnp.einsum("gpc,gh->gchp", bbar_re, eye).reshape(MIX_WIDTH, S5_HALF)
    b_bd_im = jnp.einsum("gpc,gh->gchp", bbar_im, eye).reshape(MIX_WIDTH, S5_HALF)
    c_bd_re = jnp.einsum("gcp,gh->gphc", c_re, eye).reshape(S5_HALF, MIX_WIDTH)
    c_bd_im = jnp.einsum("gcp,gh->gphc", c_im, eye).reshape(S5_HALF, MIX_WIDTH)
    return {
        "bmat": jnp.concatenate([b_bd_re, b_bd_im], axis=1).astype(BF16),
        "cmat": jnp.concatenate([c_bd_re, -c_bd_im], axis=0).astype(BF16),
        "a_re": abar_re.reshape(1, S5_HALF), "a_im": abar_im.reshape(1, S5_HALF),
        "d": d_skip.reshape(1, MIX_WIDTH), "w_glu": w_glu.astype(BF16),
    }


MLA_QK_PAD = 128


def _mla_prep_kernel(h_ref, cos_ref, sin_ref, gq_ref, gkv_ref, wq_ref, wqr_ref, wk_ref, wv_ref,
                     q_ref, k_ref, v_ref):
    h = h_ref[...]
    cos = cos_ref[...]
    sin = sin_ref[...]
    c_q = h[:, 0:256]
    c_kv = h[:, 256:384]
    k_pe = h[:, 384:512] * cos + h[:, 512:640] * sin
    qn = c_q * lax.rsqrt(jnp.sum(c_q * c_q, axis=-1, keepdims=True) / MLA_Q_LORA + EPS) * gq_ref[...]
    kvn = c_kv * lax.rsqrt(jnp.mean(c_kv * c_kv, axis=-1, keepdims=True) + EPS) * gkv_ref[...]
    qn = qn.astype(BF16)
    kvn = kvn.astype(BF16)
    scale = (MLA_NOPE + MLA_ROPE) ** -0.5 * math.log2(math.e)
    heads = lambda a: jnp.concatenate([a] * MLA_HEADS, axis=-1)
    q = (_dot(qn, wq_ref[...]) * heads(cos) + _dot(qn, wqr_ref[...]) * heads(sin)) * scale
    k = _dot(kvn, wk_ref[...]) + heads(k_pe)
    v = _dot(kvn, wv_ref[...])
    for hd in range(MLA_HEADS):
        q_ref[hd] = q[:, hd * MLA_QK_PAD:(hd + 1) * MLA_QK_PAD].astype(BF16)
        k_ref[hd] = k[:, hd * MLA_QK_PAD:(hd + 1) * MLA_QK_PAD].astype(BF16)
        v_ref[hd] = v[:, hd * MLA_V:(hd + 1) * MLA_V].astype(BF16)


def _flash_kernel(q_ref, k_ref, v_ref, o_ref, m_ref, l_ref, acc_ref):
    qi = pl.program_id(1)
    _, nb, tq, _ = q_ref.shape
    n_part = tq // LANES
    row = lax.broadcasted_iota(jnp.int32, (tq, tq), 0)
    col = lax.broadcasted_iota(jnp.int32, (tq, tq), 1)
    units = [(hd, r) for r in range(nb) for hd in range(MLA_HEADS)]

    def scores(hd, r, rows, masked):
        s = _dot_nt(q_ref[hd, r], k_ref[hd, r, rows, :])
        if masked:
            s = jnp.where(col <= row, s, NEG_INF)
        return [s[:, c * LANES:(c + 1) * LANES] for c in range(n_part)]

    def chunk(j, masked):
        rows = pl.ds(pl.multiple_of(j * tq, tq), tq)
        nxt = scores(*units[0], rows, masked)
        for u, (hd, r) in enumerate(units):
            parts = nxt
            if u + 1 < len(units):
                nxt = scores(*units[u + 1], rows, masked)
            m_old = m_ref[u]
            row_max = jnp.max(functools.reduce(jnp.maximum, parts), axis=-1, keepdims=True)
            m_new = jnp.maximum(m_old, row_max)
            p = [jnp.exp2(part - m_new) for part in parts]
            corr = jnp.exp2(m_old - m_new)
            l_ref[u] = corr * l_ref[u] + functools.reduce(jnp.add, p)
            m_ref[u] = m_new
            prob = jnp.concatenate(p, axis=-1).astype(BF16)
            acc_ref[u] = corr[:, :MLA_V] * acc_ref[u] + _dot(prob, v_ref[hd, r, rows, :])

    def body(j, carry):
        chunk(j, False)
        return carry

    m_ref[...] = jnp.full_like(m_ref, NEG_INF)
    l_ref[...] = jnp.zeros_like(l_ref)
    acc_ref[...] = jnp.zeros_like(acc_ref)
    lax.fori_loop(0, qi, body, 0)
    chunk(qi, True)
    for r in range(nb):
        o_ref[r] = jnp.concatenate(
            [acc_ref[r * MLA_HEADS + hd] / jnp.sum(l_ref[r * MLA_HEADS + hd], axis=-1, keepdims=True)
             for hd in range(MLA_HEADS)], axis=-1)


def mla_mixer(h_mla, cos_m, sin_m, bsz, seq, prm, layer):
    t = bsz * seq
    tm = TM_PROJ
    row = lambda w: pl.BlockSpec((tm, w), lambda i: (i, 0))
    hrow = lambda w: pl.BlockSpec((MLA_HEADS, tm, w), lambda i: (0, i, 0))
    names = ("gq", "gkv", "wq", "wqr", "wk", "wv")
    q, k, v = pl.pallas_call(
        _mla_prep_kernel,
        grid=(t // tm,),
        in_specs=[row(W_MLA_IN), row(LANES), row(LANES)] + [_layer_spec(prm[n], layer) for n in names],
        out_specs=[hrow(MLA_QK_PAD), hrow(MLA_QK_PAD), hrow(MLA_V)],
        out_shape=[jax.ShapeDtypeStruct((MLA_HEADS, t, MLA_QK_PAD), BF16),
                   jax.ShapeDtypeStruct((MLA_HEADS, t, MLA_QK_PAD), BF16),
                   jax.ShapeDtypeStruct((MLA_HEADS, t, MLA_V), BF16)],
        compiler_params=_cparams("parallel"),
        name="mla_prep",
    )(h_mla, cos_m, sin_m, *[prm[n] for n in names])

    tq = TQ_ATT
    nb = NB_ATT
    assert bsz % nb == 0 and seq % tq == 0
    n_unit = nb * MLA_HEADS
    by_row = lambda a: a.reshape(MLA_HEADS, bsz, seq, a.shape[-1])
    q_spec = pl.BlockSpec((MLA_HEADS, nb, tq, MLA_QK_PAD), lambda b, i: (0, b, i, 0))
    k_spec = pl.BlockSpec((MLA_HEADS, nb, seq, MLA_QK_PAD), lambda b, i: (0, b, 0, 0))
    v_spec = pl.BlockSpec((MLA_HEADS, nb, seq, MLA_V), lambda b, i: (0, b, 0, 0))
    y = pl.pallas_call(
        _flash_kernel,
        grid=(bsz // nb, seq // tq),
        in_specs=[q_spec, k_spec, v_spec],
        out_specs=pl.BlockSpec((nb, tq, MLA_HEADS * MLA_V), lambda b, i: (b, i, 0)),
        out_shape=jax.ShapeDtypeStruct((bsz, seq, MLA_HEADS * MLA_V), F32),
        scratch_shapes=[pltpu.VMEM((n_unit, tq, LANES), F32), pltpu.VMEM((n_unit, tq, LANES), F32),
                        pltpu.VMEM((n_unit, tq, MLA_V), F32)],
        compiler_params=_cparams("parallel", "arbitrary"),
        name="mla_flash",
    )(by_row(q), by_row(k), by_row(v))
    return y.reshape(t, MLA_HEADS * MLA_V)


def _rot_half_cols(w, width):
    shp = w.shape
    w = w.reshape(shp[0], -1, 2, width // 2)
    return jnp.stack([-w[:, :, 1], w[:, :, 0]], axis=2).reshape(shp)


def mla_params(q_norm, kv_norm, w_uq, w_ukv):
    dq = MLA_NOPE + MLA_ROPE
    row_pad = 256 - MLA_Q_LORA
    w_uq_h = w_uq.reshape(MLA_Q_LORA, MLA_HEADS, dq)
    wq = jnp.pad(w_uq_h, ((0, row_pad), (0, 0), (0, MLA_QK_PAD - dq)))
    pe = w_uq_h[:, :, MLA_NOPE:]
    pe_rot = jnp.concatenate([-pe[..., MLA_ROPE // 2:], pe[..., :MLA_ROPE // 2]], axis=-1)
    wqr = jnp.pad(pe_rot, ((0, row_pad), (0, 0), (MLA_NOPE, MLA_QK_PAD - dq)))
    w_kv_h = w_ukv.reshape(MLA_KV_LORA, MLA_HEADS, MLA_NOPE + MLA_V)
    wk = jnp.pad(w_kv_h[:, :, :MLA_NOPE], ((0, 0), (0, 0), (0, MLA_QK_PAD - MLA_NOPE)))
    wv = w_kv_h[:, :, MLA_NOPE:]
    gq = jnp.pad(q_norm, (0, row_pad)).reshape(1, 256)
    flat = lambda w: w.reshape(w.shape[0], -1).astype(BF16)
    return {"wq": flat(wq), "wqr": flat(wqr), "wk": flat(wk), "wv": flat(wv),
            "gq": gq, "gkv": kv_norm.reshape(1, MLA_KV_LORA)}


def _head_mask(shape, hd, axis):
    return (lax.broadcasted_iota(jnp.int32, shape, axis) // HEAD_DIM) == hd


def _hgrn_kernel(h_ref, lb_ref, ng_ref, aall_ref, lvl_ref, hm_ref, blk_ref, o_ref, st_ref, e_ref):
    @pl.when(pl.program_id(1) == 0)
    def _():
        st_ref[...] = jnp.zeros_like(st_ref)

    for row in range(h_ref.shape[0]):
        _hgrn_tile(h_ref.at[row], lb_ref, ng_ref, aall_ref, lvl_ref, hm_ref, blk_ref, o_ref.at[row],
                   st_ref.at[row], e_ref.at[row])


def _hgrn_tile(h_ref, lb_ref, ng_ref, aall_ref, lvl_ref, hm_ref, blk_ref, o_ref, st_ref, e_ref):
    tt = h_ref.shape[0]
    width = HEADS * HEAD_DIM
    n_lev = aall_ref.shape[0] // tt - 1

    q = h_ref[:, 0:width]
    f = h_ref[:, width:2 * width]
    v = h_ref[:, 2 * width:3 * width]
    g = h_ref[:, 3 * width:4 * width]
    lb = lb_ref[...]
    log_sig = jnp.minimum(f, 0.0) - jnp.log1p(jnp.exp(-jnp.abs(f)))
    a = jnp.log(lb)
    b = jnp.log1p(-lb) + log_sig
    log_f = jnp.maximum(a, b) + jnp.log1p(jnp.exp(-jnp.abs(a - b)))
    k = (1.0 - lb) * _sigmoid(-f)
    qf = q * _sigmoid(q)
    e2 = _dot(aall_ref[...], jnp.concatenate(_split2(log_f), axis=-1))
    e_ref[...] = e2[:, 0:width] + e2[:, width:2 * width]

    t_row = lax.broadcasted_iota(jnp.int32, (tt, width), 0)
    lvl = lvl_ref[...]
    attn = [jnp.zeros((tt, tt), F32) for _ in range(HEADS)]
    for lev in range(n_lev):
        x = jnp.exp(e_ref[lev * tt:(lev + 1) * tt, :])
        right = ((t_row >> lev) & 1) == 1
        qt = jnp.where(right, qf * x, 0.0).astype(BF16)
        kt = jnp.where(right, 0.0, k * x).astype(BF16)
        q_heads = jnp.concatenate([qt * hm_ref[hd] for hd in range(HEADS)], axis=0)
        s = _dot_nt(q_heads, kt)
        here = lvl == lev
        for hd in range(HEADS):
            attn[hd] = jnp.where(here, s[hd * tt:(hd + 1) * tt, :], attn[hd])
    vb = v.astype(BF16)
    o = jnp.zeros((tt, width), F32)
    for hd in range(HEADS):
        o = o + _dot(attn[hd].astype(BF16), vb * hm_ref[hd])
    ones_blk = blk_ref[...]
    o = o + _dot_exact_rhs(qf * k, ones_blk) * v
    bc = e_ref[n_lev * tt:(n_lev + 1) * tt, :]
    st = st_ref[...]
    st_m = jnp.where(ones_blk > 0, st, 0.0).astype(BF16)
    o = o + _dot_nt((qf * jnp.exp(bc)).astype(BF16), st_m)
    b_last = bc[tt - 1:tt, :]
    k_out = (k * jnp.exp(b_last - bc)).astype(BF16)
    st_ref[...] = st * jnp.exp(b_last) + _dot_tn(vb, k_out)
    ms = _dot_exact_rhs(o * o, ones_blk) * (1.0 / HEAD_DIM)
    o = o * lax.rsqrt(ms + EPS) * ng_ref[...]
    o_ref[...] = o * (g * _sigmoid(g))


def _hgrn_level_matrices(tt):
    n_lev = int(math.log2(tt))
    t = np.arange(tt)[:, None]
    j = np.arange(tt)[None, :]
    mats = []
    for lev in range(n_lev):
        half = 1 << lev
        mid = (t >> (lev + 1) << (lev + 1)) + half
        right = ((t >> lev) & 1) == 1
        mats.append(np.where(right, (j >= mid) & (j <= t), (j > t) & (j < mid)).astype(np.float32))
    mats.append((j <= t).astype(np.float32))
    level = np.full((tt, tt), -1, np.int32)
    lower = np.broadcast_to(j < t, (tt, tt))
    level[lower] = np.floor(np.log2((t ^ j)[lower])).astype(np.int32)
    return jnp.asarray(np.concatenate(mats, axis=0), BF16), jnp.asarray(level)


def _head_block_ones():
    i = np.arange(HEADS * HEAD_DIM)
    return jnp.asarray((i[:, None] // HEAD_DIM == i[None, :] // HEAD_DIM).astype(np.float32), BF16)


def _head_lane_masks(rows):
    i = np.arange(HEADS * HEAD_DIM)
    m = (i[None, :] // HEAD_DIM == np.arange(HEADS)[:, None]).astype(np.float32)
    return jnp.asarray(np.broadcast_to(m[:, None, :], (HEADS, rows, HEADS * HEAD_DIM)), BF16)


def hgrn2_mixer(h_hg, lb, norm_g, bsz, seq, layer):
    t = bsz * seq
    tt = TT_HG
    width = HEADS * HEAD_DIM
    aall, level = _hgrn_level_matrices(tt)
    nt = seq // tt
    nb = NB_HG
    assert bsz % nb == 0 and seq % tt == 0
    y = pl.pallas_call(
        _hgrn_kernel,
        grid=(bsz // nb, nt),
        in_specs=[pl.BlockSpec((nb, tt, 4 * width), lambda b, i: (b, i, 0)),
                  _layer_spec(lb, layer), _layer_spec(norm_g, layer),
                  _const_spec(aall.shape), _const_spec(level.shape), _const_spec((HEADS, tt, width)),
                  _const_spec((width, width))],
        out_specs=pl.BlockSpec((nb, tt, width), lambda b, i: (b, i, 0)),
        out_shape=jax.ShapeDtypeStruct((bsz, seq, width), F32),
        scratch_shapes=[pltpu.VMEM((nb, width, width), F32), pltpu.VMEM((nb, aall.shape[0], width), F32)],
        compiler_params=_cparams("parallel", "arbitrary"),
        name="hgrn2_mixer",
    )(h_hg.reshape(bsz, seq, 4 * width), lb, norm_g, aall, level, _head_lane_masks(tt), _head_block_ones())
    return y.reshape(t, width)


def _ret_kernel(h_ref, cos_ref, sin_ref, dmat_ref, qdec_ref, kdec_ref, cdec_ref, blk_ref, g_ref, b_ref,
                o_ref, st_ref):
    @pl.when(pl.program_id(1) == 0)
    def _():
        st_ref[...] = jnp.zeros_like(st_ref)

    for row in range(h_ref.shape[0]):
        _ret_tile(h_ref.at[row], cos_ref.at[row], sin_ref.at[row], dmat_ref, qdec_ref, kdec_ref, cdec_ref,
                  blk_ref, g_ref, b_ref, o_ref.at[row], st_ref.at[row])


def _ret_tile(h_ref, cos_ref, sin_ref, dmat_ref, qdec_ref, kdec_ref, cdec_ref, blk_ref, g_ref, b_ref,
              o_ref, st_ref):
    tt = h_ref.shape[0]
    width = HEADS * HEAD_DIM

    cos = cos_ref[...]
    sin = sin_ref[...]
    q = (h_ref[:, 0:width] * cos + h_ref[:, 4 * width:5 * width] * sin) * (HEAD_DIM ** -0.5)
    k = h_ref[:, width:2 * width] * cos + h_ref[:, 5 * width:6 * width] * sin
    v = h_ref[:, 2 * width:3 * width]
    g = h_ref[:, 3 * width:4 * width]
    kb = k.astype(BF16)
    vb = v.astype(BF16)
    ones_blk = blk_ref[...]
    o = jnp.zeros((tt, width), F32)
    for hd in range(HEADS):
        hm = _head_mask((tt, width), hd, 1)
        s = _dot_nt(jnp.where(hm, q, 0.0).astype(BF16), kb) * dmat_ref[hd]
        o = o + _dot(s.astype(BF16), jnp.where(hm, vb, jnp.zeros_like(vb)))
    st = st_ref[...]
    st_m = jnp.where(ones_blk > 0, st, 0.0).astype(BF16)
    o = o + _dot((q * qdec_ref[...]).astype(BF16), st_m)
    st_ref[...] = st * cdec_ref[...] + _dot_tn((k * kdec_ref[...]).astype(BF16), vb)
    mu = _dot_exact_rhs(o, ones_blk) * (1.0 / HEAD_DIM)
    oc = o - mu
    var = _dot_exact_rhs(oc * oc, ones_blk) * (1.0 / HEAD_DIM)
    o = oc * lax.rsqrt(var + EPS) * g_ref[...] + b_ref[...]
    o_ref[...] = (g * _sigmoid(g)) * o


def _retention_tables(tt):
    log_gamma = jnp.log(1.0 - 2.0 ** (-5.0 - jnp.arange(HEADS, dtype=F32)))
    t_idx = jnp.arange(tt, dtype=F32)
    rel = t_idx[:, None] - t_idx[None, :]
    dmat = jnp.where(rel >= 0, jnp.exp(jnp.maximum(rel, 0.0)[None] * log_gamma[:, None, None]), 0.0)
    lanes = lambda a: jnp.repeat(a, HEAD_DIM, axis=-1)
    qdec = lanes(jnp.exp((t_idx + 1.0)[:, None] * log_gamma[None, :]))
    kdec = lanes(jnp.exp((tt - 1 - t_idx)[:, None] * log_gamma[None, :]))
    cdec = lanes(jnp.exp(tt * log_gamma)[None, :])
    return dmat, qdec, kdec, cdec


def retention_mixer(h_ret, cos_r, sin_r, gn_g, gn_b, bsz, seq, layer):
    t = bsz * seq
    tt = TT_RET
    width = HEADS * HEAD_DIM
    nt = seq // tt
    dmat, qdec, kdec, cdec = _retention_tables(tt)
    nb = NB_RET
    assert bsz % nb == 0 and seq % tt == 0
    row = lambda w: pl.BlockSpec((nb, tt, w), lambda b, i: (b, i, 0))
    y = pl.pallas_call(
        _ret_kernel,
        grid=(bsz // nb, nt),
        in_specs=[row(W_RET_IN), row(width), row(width), _const_spec((HEADS, tt, tt)), _const_spec((tt, width)),
                  _const_spec((tt, width)), _const_spec((1, width)), _const_spec((width, width)),
                  _layer_spec(gn_g, layer), _layer_spec(gn_b, layer)],
        out_specs=row(width),
        out_shape=jax.ShapeDtypeStruct((bsz, seq, width), F32),
        scratch_shapes=[pltpu.VMEM((nb, width, width), F32)],
        compiler_params=_cparams("parallel", "arbitrary"),
        name="retention_mixer",
    )(h_ret.reshape(bsz, seq, W_RET_IN), cos_r.reshape(bsz, seq, width), sin_r.reshape(bsz, seq, width),
      dmat, qdec, kdec, cdec, _head_block_ones(), gn_g, gn_b)
    return y.reshape(t, width)


def _store_token_tiles(ref, val):
    tm = val.shape[0]
    for j in range(D_MODEL // LANES):
        ref[pl.ds(j, tm, stride=SUBLANES), :] = val[:, j * LANES:(j + 1) * LANES]


def _load_token_tiles(ref, tm):
    return jnp.concatenate([ref[pl.ds(j, tm, stride=SUBLANES), :] for j in range(D_MODEL // LANES)], axis=-1)


def _mix_kernel(x_ref, ya_ref, yb_ref, yc_ref, yd_ref, wg_ref, wb_ref, wo_ref, g_ref, b_ref, o_ref):
    x = x_ref[...]
    xb = x.astype(BF16)
    acc = jnp.zeros(x.shape, F32)
    for n, y_ref in enumerate((ya_ref, yb_ref, yc_ref, yd_ref)):
        gate = _sigmoid(_dot(xb, wg_ref[:, n * D_MODEL:(n + 1) * D_MODEL]))
        acc = acc + gate * _dot(y_ref[...].astype(BF16), wb_ref[n])
    z = ALPHA * x + _dot(acc.astype(BF16), wo_ref[...])
    o_ref[...] = _layer_norm_rows(z, g_ref[...], b_ref[...])


def mix_branches(x, ys, params, layer):
    t = x.shape[0]
    tm = TM_PROJ
    row = lambda w: pl.BlockSpec((tm, w), lambda i: (i, 0))
    return pl.pallas_call(
        _mix_kernel,
        grid=(t // tm,),
        in_specs=[row(D_MODEL)] + [row(MIX_WIDTH)] * 4 + [_layer_spec(a, layer) for a in params],
        out_specs=row(D_MODEL),
        out_shape=jax.ShapeDtypeStruct((t, D_MODEL), F32),
        compiler_params=_cparams("parallel"),
        name="mix_branches",
    )(x, *ys, *params)


def _ffn_kernel(x_ref, wg_ref, wu_ref, wd_ref, p_ref, wpg_ref, wpp_ref, g_ref, b_ref, o_ref, xb_ref, acc_ref):
    c = pl.program_id(1)

    @pl.when(c == 0)
    def _():
        xb_ref[...] = x_ref[...].astype(BF16)
        acc_ref[...] = jnp.zeros_like(acc_ref)

    xb = xb_ref[...]
    gate = _dot(xb, wg_ref[...])
    act = gate * _sigmoid(gate) * _dot(xb, wu_ref[...])
    acc_ref[...] += _dot(act.astype(BF16), wd_ref[...])

    @pl.when(c == pl.num_programs(1) - 1)
    def _():
        ple = _sigmoid(_dot(xb, wpg_ref[...])) * _dot(p_ref[...].astype(BF16), wpp_ref[...])
        o_ref[...] = _layer_norm_rows(ALPHA * x_ref[...] + acc_ref[...] + ple, g_ref[...], b_ref[...])


def dense_ffn_layer(x, ffn_w, ffn_layer, p, layer, post_params):
    t = x.shape[0]
    tm, tf = TM_FFN, TF_FFN
    return pl.pallas_call(
        _ffn_kernel,
        grid=(t // tm, D_FF // tf),
        in_specs=[pl.BlockSpec((tm, D_MODEL), lambda i, c: (i, 0)),
                  pl.BlockSpec((None, D_MODEL, tf), lambda i, c: (ffn_layer, 0, c)),
                  pl.BlockSpec((None, D_MODEL, tf), lambda i, c: (ffn_layer, 0, c)),
                  pl.BlockSpec((None, tf, D_MODEL), lambda i, c: (ffn_layer, c, 0)),
                  pl.BlockSpec((None, tm, PLE_DIM), lambda i, c: (layer, i, 0))] +
                 [_layer_spec(a, layer) for a in post_params],
        out_specs=pl.BlockSpec((tm, D_MODEL), lambda i, c: (i, 0)),
        out_shape=jax.ShapeDtypeStruct((t, D_MODEL), F32),
        scratch_shapes=[pltpu.VMEM((tm, D_MODEL), BF16), pltpu.VMEM((tm, D_MODEL), F32)],
        compiler_params=_cparams("parallel", "arbitrary"),
        name="dense_ffn",
    )(x, *ffn_w, p, *post_params)


def _router_kernel(x_ref, wh_ref, wl_ref, upper_ref, meta_ref, cnt_ref, slots_ref, carry_ref):
    tm = x_ref.shape[0]
    slots_ref[...] = jnp.zeros_like(slots_ref)

    @pl.when(pl.program_id(0) == 0)
    def _():
        carry_ref[...] = jnp.zeros_like(carry_ref)

    xh, xl = _split2(x_ref[...])
    wh = wh_ref[...]
    logits = _dot_nt(wh, xh) + _dot_nt(wh, xl) + _dot_nt(wl_ref[...], xh)
    e_idx = lax.broadcasted_iota(jnp.int32, logits.shape, 0)
    m1 = jnp.max(logits, axis=0, keepdims=True)
    i1 = jnp.min(jnp.where(logits == m1, e_idx, N_EXPERTS), axis=0, keepdims=True)
    rest = jnp.where(e_idx == i1, -jnp.inf, logits)
    m2 = jnp.max(rest, axis=0, keepdims=True)
    i2 = jnp.min(jnp.where(rest == m2, e_idx, N_EXPERTS), axis=0, keepdims=True)
    ex = jnp.exp(m2 - m1)
    w1 = 1.0 / (1.0 + ex)
    w2 = ex / (1.0 + ex)
    sel1 = e_idx == i1
    sel2 = e_idx == i2
    onehot = jnp.where(sel1 | sel2, 1.0, 0.0)
    carry = carry_ref[...]
    ranks = _dot(onehot.astype(BF16), upper_ref[...]) + carry[:, 0:1]
    r1 = jnp.sum(jnp.where(sel1, ranks, 0.0), axis=0, keepdims=True)
    r2 = jnp.sum(jnp.where(sel2, ranks, 0.0), axis=0, keepdims=True)
    row = lax.broadcasted_iota(jnp.int32, (SUBLANES, tm), 0)
    vals = (i1.astype(F32), i2.astype(F32), w1, w2, r1, r2)
    meta = jnp.zeros((SUBLANES, tm), F32)
    for n, val in enumerate(vals):
        meta = jnp.where(row == n, val, meta)
    meta_ref[...] = meta
    carry = carry + jnp.sum(onehot, axis=1, keepdims=True)
    carry_ref[...] = carry
    cnt_ref[...] = carry


def moe_router(x, w_router, p_len):
    t = x.shape[0]
    tm = TM_ROUTE
    slot_rows = p_len * SUBLANES // (t // tm)
    assert slot_rows % SUBLANES == 0 and slot_rows * (t // tm) == p_len * SUBLANES
    wt = w_router.T
    wh = wt.astype(BF16)
    wl = (wt - wh.astype(F32)).astype(BF16)
    upper = jnp.asarray(np.triu(np.ones((tm, tm), np.float32), 1), BF16)
    return pl.pallas_call(
        _router_kernel,
        grid=(t // tm,),
        in_specs=[pl.BlockSpec((tm, D_MODEL), lambda i: (i, 0)), _const_spec((N_EXPERTS, D_MODEL)),
                  _const_spec((N_EXPERTS, D_MODEL)), _const_spec((tm, tm))],
        out_specs=[pl.BlockSpec((SUBLANES, tm), lambda i: (0, i)), pl.BlockSpec((N_EXPERTS, LANES), lambda i: (0, 0)),
                   pl.BlockSpec((slot_rows, LANES), lambda i: (i, 0))],
        out_shape=[jax.ShapeDtypeStruct((SUBLANES, t), F32), jax.ShapeDtypeStruct((N_EXPERTS, LANES), F32),
                   jax.ShapeDtypeStruct((p_len * SUBLANES, LANES), F32)],
        scratch_shapes=[pltpu.VMEM((N_EXPERTS, LANES), F32)],
        compiler_params=_cparams("arbitrary"),
        name="moe_router",
    )(x, wh, wl, upper)


def _token_rows(ref, tok):
    return ref.at[pl.ds(pl.multiple_of(tok * SUBLANES, SUBLANES), SUBLANES), :]


def _start_all(n, copies):
    def start(r, carry):
        for idx, cp in enumerate(copies(r)):
            cp.start(priority=idx % 2)
        return carry

    lax.fori_loop(0, n, start, 0, unroll=DMA_UNROLL)


def _wait_all(n, copies):
    def wait(r, carry):
        for cp in copies(r):
            cp.wait()
        return carry

    lax.fori_loop(0, n, wait, 0, unroll=DMA_UNROLL)


def _dispatch_kernel(dest_ref, x_ref, init_ref, xs_ref, buf_ref, sem):
    del init_ref
    td = x_ref.shape[0]
    _store_token_tiles(buf_ref, x_ref[...])

    def copies(r):
        src = _token_rows(buf_ref, r)
        return (pltpu.make_async_copy(src, _token_rows(xs_ref, dest_ref[0, 0, 2 * r]), sem),
                pltpu.make_async_copy(src, _token_rows(xs_ref, dest_ref[0, 0, 2 * r + 1]), sem))

    _start_all(td, copies)
    _wait_all(td, copies)


def moe_dispatch(x, dest3, init):
    t = x.shape[0]
    td = dest3.shape[2] // 2
    return pl.pallas_call(
        _dispatch_kernel,
        grid=(t // td,),
        in_specs=[pl.BlockSpec((1, 1, 2 * td), lambda i: (i, 0, 0), memory_space=pltpu.SMEM),
                  pl.BlockSpec((td, D_MODEL), lambda i: (i, 0)), pl.BlockSpec(memory_space=pl.ANY)],
        out_specs=pl.BlockSpec(memory_space=pl.ANY),
        out_shape=jax.ShapeDtypeStruct(init.shape, F32),
        scratch_shapes=[pltpu.VMEM((td * SUBLANES, LANES), F32), pltpu.SemaphoreType.DMA(())],
        input_output_aliases={2: 0},
        compiler_params=pltpu.CompilerParams(dimension_semantics=("arbitrary",), has_side_effects=True),
        name="moe_dispatch",
    )(dest3, x, init)


def _expert_kernel(te_ref, nv_ref, tr_ref, xs_ref, wg_ref, wu_ref, wd_ref, ys_ref, xb_ref, acc_ref):
    i = pl.program_id(0)
    c = pl.program_id(1)
    tm = xb_ref.shape[0]
    valid = i < nv_ref[0]
    upper_used = tr_ref[i] > tm // 2

    @pl.when(valid & (c == 0))
    def _():
        xb_ref[...] = _load_token_tiles(xs_ref, tm).astype(BF16)
        acc_ref[...] = jnp.zeros_like(acc_ref)

    def swiglu_rows(m):
        xb = xb_ref[:m, :]
        gate = _dot(xb, wg_ref[...].astype(BF16))
        act = gate * _sigmoid(gate) * _dot(xb, wu_ref[...].astype(BF16))
        acc_ref[:m, :] += _dot(act.astype(BF16), wd_ref[...].astype(BF16))

    @pl.when(valid & upper_used)
    def _():
        swiglu_rows(tm)

    @pl.when(valid & jnp.logical_not(upper_used))
    def _():
        swiglu_rows(tm // 2)

    @pl.when(c == pl.num_programs(1) - 1)
    def _():
        _store_token_tiles(ys_ref, jnp.where(valid, acc_ref[...], 0.0))


def moe_experts(xs, tile_expert, n_valid, tile_rows, layer, w_gate, w_up, w_down):
    tm, tf = TM_MOE, TF_FFN
    n_tiles = xs.shape[0] // (tm * SUBLANES)
    nc = D_FF // tf

    def chunk(i, c, nv):
        return jnp.where(i < nv[0], c, nc - 1)

    grid_spec = pltpu.PrefetchScalarGridSpec(
        num_scalar_prefetch=3,
        grid=(n_tiles, nc),
        in_specs=[pl.BlockSpec((tm * SUBLANES, LANES), lambda i, c, te, nv, tr: (jnp.minimum(i, nv[0] - 1), 0)),
                  pl.BlockSpec((None, None, D_MODEL, tf), lambda i, c, te, nv, tr: (layer, te[i], 0, chunk(i, c, nv))),
                  pl.BlockSpec((None, None, D_MODEL, tf), lambda i, c, te, nv, tr: (layer, te[i], 0, chunk(i, c, nv))),
                  pl.BlockSpec((None, None, tf, D_MODEL), lambda i, c, te, nv, tr: (layer, te[i], chunk(i, c, nv), 0))],
        out_specs=pl.BlockSpec((tm * SUBLANES, LANES), lambda i, c, te, nv, tr: (i, 0)),
        scratch_shapes=[pltpu.VMEM((tm, D_MODEL), BF16), pltpu.VMEM((tm, D_MODEL), F32)],
    )
    return pl.pallas_call(
        _expert_kernel,
        grid_spec=grid_spec,
        out_shape=jax.ShapeDtypeStruct(xs.shape, F32),
        compiler_params=_cparams("arbitrary", "arbitrary"),
        name="moe_experts",
    )(tile_expert, n_valid, tile_rows, xs, w_gate, w_up, w_down)


def moe_ffn(x, w_router, layer, w_gate, w_up, w_down):
    t = x.shape[0]
    tm = TM_MOE
    n_tiles = (2 * t) // tm + N_EXPERTS
    meta, cnt, slots = moe_router(x, w_router, n_tiles * tm)
    counts = cnt[:, 0].astype(jnp.int32)
    padded = (counts + tm - 1) // tm * tm
    pad_end = jnp.cumsum(padded)
    pad_start = pad_end - padded
    experts = meta[0:2].astype(jnp.int32)
    start_of = jnp.sum(jnp.where(experts[..., None] == jnp.arange(N_EXPERTS), pad_start, 0), axis=-1)
    dest = (start_of + meta[4:6].astype(jnp.int32)).T
    dest3 = dest.reshape(t // TD_DISPATCH, 1, 2 * TD_DISPATCH)
    n_valid = (pad_end[-1] // tm).astype(jnp.int32).reshape(1)
    tile_start = jnp.arange(n_tiles, dtype=jnp.int32) * tm
    tile_expert = jnp.minimum(jnp.sum(pad_end[None, :] <= tile_start[:, None], axis=1), N_EXPERTS - 1)
    last_expert = jnp.sum(jnp.where(jnp.arange(n_tiles) == n_valid[0] - 1, tile_expert, 0))
    tile_expert = jnp.where(jnp.arange(n_tiles) < n_valid[0], tile_expert, last_expert).astype(jnp.int32)
    of_tile = lambda per_expert: jnp.sum(jnp.where(tile_expert[:, None] == jnp.arange(N_EXPERTS), per_expert, 0), -1)
    tile_rows = jnp.clip(of_tile(pad_start + counts) - tile_start, 0, tm).astype(jnp.int32)
    xs = moe_dispatch(x, dest3, slots)
    ys = moe_experts(xs, tile_expert, n_valid, tile_rows, layer, w_gate, w_up, w_down)
    return ys, dest3, meta[2:4].T


def _post_moe_kernel(dest_ref, next_dest_ref, x_ref, p_ref, wpg_ref, wpp_ref, g_ref, b_ref, w_ref, ys_ref, o_ref,
                     buf_ref, sem):
    i = pl.program_id(0)
    slot = lax.rem(i, 2)
    x = x_ref[...]
    tm = x.shape[0]

    def gathers(d_ref, s):
        def copies(r):
            return tuple(pltpu.make_async_copy(_token_rows(ys_ref, d_ref[0, 0, 2 * r + k]),
                                               _token_rows(buf_ref.at[s, k], r), sem.at[s]) for k in range(2))
        return copies

    @pl.when(i == 0)
    def _():
        _start_all(tm, gathers(dest_ref, 0))

    @pl.when(i + 1 < pl.num_programs(0))
    def _():
        _start_all(tm, gathers(next_dest_ref, 1 - slot))

    ple = _sigmoid(_dot(x.astype(BF16), wpg_ref[...])) * _dot(p_ref[...].astype(BF16), wpp_ref[...])
    _wait_all(tm, gathers(dest_ref, slot))
    w = w_ref[...]
    f = (w[:, 0:1] * _load_token_tiles(buf_ref.at[slot, 0], tm) +
         w[:, 1:2] * _load_token_tiles(buf_ref.at[slot, 1], tm))
    o_ref[...] = _layer_norm_rows(ALPHA * x + f + ple, g_ref[...], b_ref[...])


def moe_post_layer(x, p, layer, params, moe):
    t = x.shape[0]
    tm = TD_DISPATCH
    ys, dest3, w2 = moe
    row = lambda w: pl.BlockSpec((tm, w), lambda i: (i, 0))
    n_steps = t // tm
    dest_spec = lambda ahead: pl.BlockSpec((1, 1, 2 * tm), lambda i: (jnp.minimum(i + ahead, n_steps - 1), 0, 0),
                                           memory_space=pltpu.SMEM)
    return pl.pallas_call(
        _post_moe_kernel,
        grid=(n_steps,),
        in_specs=[dest_spec(0), dest_spec(1), row(D_MODEL),
                  pl.BlockSpec((None, tm, PLE_DIM), lambda i: (layer, i, 0))] +
                 [_layer_spec(a, layer) for a in params] + [row(2), pl.BlockSpec(memory_space=pl.ANY)],
        out_specs=row(D_MODEL),
        out_shape=jax.ShapeDtypeStruct((t, D_MODEL), F32),
        scratch_shapes=[pltpu.VMEM((2, 2, tm * SUBLANES, LANES), F32), pltpu.SemaphoreType.DMA((2,))],
        compiler_params=_cparams("arbitrary"),
        name="moe_post_layer",
    )(dest3, dest3, x, p, *params, w2, ys)


def split_in_weights(w):
    sizes = (MIX_WIDTH, MLA_Q_LORA, MLA_KV_LORA, MLA_ROPE) + (HEADS * HEAD_DIM,) * 8 + (N_BRANCH * D_MODEL,)
    parts, start = [], 0
    for n in sizes:
        parts.append(w[:, start:start + n])
        start += n
    u_s5, c_q, c_kv, k_rope, hq, hf, hi, hg, rq, rk, rv, rg, gate = parts
    k_rot = jnp.concatenate([-k_rope[:, MLA_ROPE // 2:], k_rope[:, :MLA_ROPE // 2]], axis=1)
    zeros = lambda n: jnp.zeros((D_MODEL, n), w.dtype)
    rope_slot = lambda cols: [zeros(MLA_NOPE), cols, zeros(MLA_QK_PAD - MLA_NOPE - MLA_ROPE)]
    w_mla = jnp.concatenate([c_q, zeros(256 - MLA_Q_LORA), c_kv] + rope_slot(k_rope) + rope_slot(k_rot), axis=1)
    w_hg = jnp.concatenate([hq, hf, hi, hg], axis=1)
    w_ret = jnp.concatenate([rq, rk, rv, rg, _rot_half_cols(rq, HEAD_DIM), _rot_half_cols(rk, HEAD_DIM)], axis=1)
    return [a.astype(BF16) for a in (w_mla, u_s5, w_hg, w_ret)], gate.astype(BF16)


def kernel(x, p, positions, w_in, s5_a_re, s5_a_im, s5_log_dt, s5_b_re, s5_b_im, s5_c_re, s5_c_im, s5_d, s5_w_glu, mla_q_norm, mla_kv_norm, mla_w_uq, mla_w_ukv, hg_lb_raw, hg_norm, ret_gn_g, ret_gn_b, w_branch, w_o, ln1_g, ln1_b, ff_w_gate, ff_w_up, ff_w_down, moe_router, moe_w_gate, moe_w_up, moe_w_down, ple_w_gate, ple_w_proj, ln2_g, ln2_b):
    bsz, seq, _ = x.shape
    t = bsz * seq
    depth = w_in.shape[0]
    x = x.reshape(t, D_MODEL)
    p = p.reshape(depth, t, PLE_DIM)
    cos_m, sin_m, cos_r, sin_r = rope_tables(positions)
    lb_all = jnp.cumsum(jax.nn.softmax(hg_lb_raw.astype(F32), axis=0), axis=0)
    lb_all = lb_all - lb_all[0]
    rows = lambda a: a.astype(F32).reshape(depth, 1, -1)
    in_ws, w_gate = jax.vmap(split_in_weights)(w_in)
    s5_prm = jax.vmap(s5_params)(s5_a_re, s5_a_im, s5_log_dt, s5_b_re, s5_b_im, s5_c_re, s5_c_im, s5_d, s5_w_glu)
    mla_prm = jax.vmap(mla_params)(mla_q_norm, mla_kv_norm, mla_w_uq, mla_w_ukv)
    mix_prm = (w_gate, w_branch.astype(BF16), w_o.astype(BF16), rows(ln1_g), rows(ln1_b))
    post_prm = (ple_w_gate.astype(BF16), ple_w_proj.astype(BF16), rows(ln2_g), rows(ln2_b))
    ffn_w = (ff_w_gate.astype(BF16), ff_w_up.astype(BF16), ff_w_down.astype(BF16))
    lb_rows, hg_norm_rows, gn_g_rows, gn_b_rows = rows(lb_all), rows(hg_norm), rows(ret_gn_g), rows(ret_gn_b)
    for i in range(depth):
        h_mla, h_s5, h_hg, h_ret = in_projection(x, in_ws, i)
        y_a = s5_mixer(h_s5, bsz, seq, s5_prm, i)
        y_b = mla_mixer(h_mla, cos_m, sin_m, bsz, seq, mla_prm, i)
        y_c = hgrn2_mixer(h_hg, lb_rows, hg_norm_rows, bsz, seq, i)
        y_d = retention_mixer(h_ret, cos_r, sin_r, gn_g_rows, gn_b_rows, bsz, seq, i)
        x = mix_branches(x, (y_a, y_b, y_c, y_d), mix_prm, i)
        if i % 2 == 1:
            x = moe_post_layer(x, p, i, post_prm, moe_ffn(x, moe_router[i // 2], i // 2, moe_w_gate, moe_w_up,
                                                          moe_w_down))
        else:
            x = dense_ffn_layer(x, ffn_w, i // 2, p, i, post_prm)
    return x.reshape(bsz, seq, D_MODEL)
```
